```python
import math
import jax, jax.numpy as jnp
from jax import lax
import numpy as np

D_MODEL = 2048
BATCH = 16
SEQ = 2048
DEPTH = 1

N_META = 16
GRID_W = 64
DN_HEADS = D_MODEL // 256
DN_HEAD_DIM = 128
DN_WIDTH = DN_HEADS * DN_HEAD_DIM
DN_CONV = 5
DN_CHUNK = 64
NA_HEADS = D_MODEL // 256
NA_HEAD_DIM = 128
NA_WIDTH = NA_HEADS * NA_HEAD_DIM
NA_WIN_R = 8
NA_WIN_C = 16
NA_QBLK_C = 16
NA_KBLK_C = NA_QBLK_C + NA_WIN_C
N_EXPERTS = 32
TOP_K = 4
D_EXPERT = D_MODEL
SWIGLU_ALPHA = 1.702
SWIGLU_LIMIT = 7.0
MOE_BLOCK = 512
NORM_EPS = 1e-6
IN_SPLITS = (DN_WIDTH, DN_WIDTH, DN_WIDTH, DN_WIDTH, 2 * DN_HEADS, 2 * DN_HEADS,
             NA_WIDTH, NA_WIDTH, NA_WIDTH, D_MODEL, D_MODEL)
IN_COLS = sum(IN_SPLITS)

kernel_name = 'hybrid_gdn_natten_moe_encoder'


def rms_norm(x, gain):
    xf = x.astype(jnp.float32)
    y = xf * lax.rsqrt(jnp.mean(xf * xf, axis=-1, keepdims=True) + NORM_EPS)
    return (y * gain.astype(jnp.float32)).astype(x.dtype)


def l2_normalize(x):
    xf = x.astype(jnp.float32)
    return xf * lax.rsqrt(jnp.sum(xf * xf, axis=-1, keepdims=True) + 1e-6)


def centred_depthwise_conv(x, w):
    k = w.shape[0]
    return lax.conv_general_dilated(
        x, w[:, None, :].astype(x.dtype), window_strides=(1,), padding=[(k // 2, k // 2)],
        dimension_numbers=('NWC', 'WIO', 'NWC'), feature_group_count=x.shape[-1])


def chunk_gated_delta(q, k, v, g, beta):
    bsz, nh, l, dk = q.shape
    dv = v.shape[-1]
    nc = l // DN_CHUNK
    q = q.reshape(bsz, nh, nc, DN_CHUNK, dk)
    k = k.reshape(bsz, nh, nc, DN_CHUNK, dk)
    v = v.reshape(bsz, nh, nc, DN_CHUNK, dv)
    g = g.reshape(bsz, nh, nc, DN_CHUNK)
    beta = beta.reshape(bsz, nh, nc, DN_CHUNK)
    gam = jnp.cumsum(g, axis=-1)
    incl = np.tril(np.ones((DN_CHUNK, DN_CHUNK), dtype=bool))
    strict = np.tril(np.ones((DN_CHUNK, DN_CHUNK), dtype=bool), k=-1)
    decay = jnp.exp(jnp.where(incl, gam[..., :, None] - gam[..., None, :], -jnp.inf))
    kk = jnp.einsum('bhntd,bhnsd->bhnts', k, k)
    tri = jnp.eye(DN_CHUNK, dtype=jnp.float32) + jnp.where(strict, beta[..., :, None] * kk * decay, 0.0)
    rhs = jnp.concatenate([beta[..., None] * v, (beta * jnp.exp(gam))[..., None] * k], axis=-1)
    sol = lax.linalg.triangular_solve(tri, rhs, left_side=True, lower=True)
    u, w = sol[..., :dv], sol[..., dv:]
    qk = jnp.einsum('bhntd,bhnsd->bhnts', q, k) * decay
    q_dec = q * jnp.exp(gam)[..., None]
    k_dec = k * jnp.exp(gam[..., -1:] - gam)[..., None]
    chunk_decay = jnp.exp(gam[..., -1])
    xs = tuple(jnp.moveaxis(t, 2, 0) for t in (u, w, qk, q_dec, k_dec, chunk_decay))

    def step(state, inp):
        u_c, w_c, qk_c, qd_c, kd_c, cd_c = inp
        v_new = u_c - jnp.einsum('bhtk,bhkv->bhtv', w_c, state)
        o_c = jnp.einsum('bhtk,bhkv->bhtv', qd_c, state) + jnp.einsum('bhts,bhsv->bhtv', qk_c, v_new)
        state = state * cd_c[..., None, None] + jnp.einsum('bhsk,bhsv->bhkv', kd_c, v_new)
        return state, o_c

    _, o = lax.scan(step, jnp.zeros((bsz, nh, dk, dv), jnp.float32), xs)
    return jnp.moveaxis(o, 0, 2).reshape(bsz, nh, l, dv)


def deltanet_branch(q, k, v, z, b_logits, a_logits, conv_w, a_log, dt_bias, out_norm):
    bsz, l, _ = q.shape
    qkv = jax.nn.silu(centred_depthwise_conv(jnp.concatenate([q, k, v], axis=-1), conv_w))
    q, k, v = jnp.split(qkv, 3, axis=-1)

    def heads(t):
        return t.reshape(bsz, l, DN_HEADS, DN_HEAD_DIM).transpose(0, 2, 1, 3)

    q = l2_normalize(heads(q)) * (DN_HEAD_DIM ** -0.5)
    k = l2_normalize(heads(k))
    v = heads(v).astype(jnp.float32)
    beta = jax.nn.sigmoid(b_logits.astype(jnp.float32)).reshape(bsz, l, 2, DN_HEADS).transpose(2, 0, 3, 1)
    a_in = a_logits.astype(jnp.float32).reshape(bsz, l, 2, DN_HEADS).transpose(2, 0, 3, 1)
    g = -jnp.exp(a_log.astype(jnp.float32))[:, None, :, None] * jax.nn.softplus(
        a_in + dt_bias.astype(jnp.float32)[:, None, :, None])
    pad = (-N_META) % DN_CHUNK
    pad4 = ((0, 0), (0, 0), (pad, 0), (0, 0))
    pad3 = ((0, 0), (0, 0), (pad, 0))
    qp, kp, vp = jnp.pad(q, pad4), jnp.pad(k, pad4), jnp.pad(v, pad4)
    g_f, g_b = jnp.pad(g[0], pad3), jnp.pad(g[1], pad3)
    be_f, be_b = jnp.pad(beta[0], pad3), jnp.pad(beta[1], pad3)
    o_f = chunk_gated_delta(qp, kp, vp, g_f, be_f)
    rev = lambda t: jnp.flip(t, axis=2)
    o_b = rev(chunk_gated_delta(rev(qp), rev(kp), rev(vp), rev(g_b), rev(be_b)))
    o = (o_f + o_b)[:, :, pad:].transpose(0, 2, 1, 3)
    o = o * lax.rsqrt(jnp.mean(o * o, axis=-1, keepdims=True) + NORM_EPS) * out_norm.astype(jnp.float32)
    o = o * jax.nn.silu(z.astype(jnp.float32).reshape(bsz, l, DN_HEADS, DN_HEAD_DIM))
    return o.reshape(bsz, l, DN_WIDTH).astype(z.dtype)


def na_column_tables():
    n_cb = GRID_W // NA_QBLK_C
    q_cols = np.arange(GRID_W).reshape(n_cb, NA_QBLK_C)
    q_start = np.clip(q_cols - NA_WIN_C // 2, 0, GRID_W - NA_WIN_C)
    blk_start = np.clip(np.arange(n_cb) * NA_QBLK_C - NA_WIN_C // 2, 0, GRID_W - NA_KBLK_C)
    key_cols = blk_start[:, None] + np.arange(NA_KBLK_C)
    rel = key_cols[:, None, :] - q_start[:, :, None]
    valid = (rel >= 0) & (rel < NA_WIN_C)
    dc_idx = np.clip(key_cols[:, None, :] - q_cols[:, :, None] + NA_WIN_C - 1, 0, 2 * NA_WIN_C - 2)
    return key_cols, valid, dc_idx


def neighbourhood_branch(q, k, v, rpb):
    bsz, l, _ = q.shape
    n_tok = l - N_META
    rows = n_tok // GRID_W
    wr = min(NA_WIN_R, rows)
    n_cb = GRID_W // NA_QBLK_C
    kb = wr * NA_KBLK_C

    def heads(t):
        return t.reshape(bsz, l, NA_HEADS, NA_HEAD_DIM).transpose(0, 2, 1, 3)

    q = heads(q) * (NA_HEAD_DIM ** -0.5)
    k, v = heads(k), heads(v)
    qm, km, vm = q[:, :, :N_META], k[:, :, :N_META], v[:, :, :N_META]
    grid = lambda t: t[:, :, N_META:].reshape(bsz, NA_HEADS, rows, GRID_W, NA_HEAD_DIM)
    qg, kg, vg = grid(q), grid(k), grid(v)
    key_cols, valid, dc_idx = na_column_tables()
    mask = np.broadcast_to(valid[:, :, None, :], (n_cb, NA_QBLK_C, wr, NA_KBLK_C)).reshape(n_cb, NA_QBLK_C, kb)

    def attend_row(r):
        rs = jnp.clip(r - wr // 2, 0, rows - wr)
        k_rows = lax.dynamic_slice_in_dim(kg, rs, wr, axis=2)
        v_rows = lax.dynamic_slice_in_dim(vg, rs, wr, axis=2)
        k_blk = k_rows[:, :, :, key_cols].transpose(0, 1, 3, 2, 4, 5).reshape(bsz, NA_HEADS, n_cb, kb, NA_HEAD_DIM)
        v_blk = v_rows[:, :, :, key_cols].transpose(0, 1, 3, 2, 4, 5).reshape(bsz, NA_HEADS, n_cb, kb, NA_HEAD_DIM)
        q_row = lax.dynamic_index_in_dim(qg, r, axis=2, keepdims=False).reshape(
            bsz, NA_HEADS, n_cb, NA_QBLK_C, NA_HEAD_DIM)
        dr_idx = rs - r + jnp.arange(wr) + NA_WIN_R - 1
        bias = rpb[:, dr_idx][:, :, dc_idx]
        bias = bias.transpose(0, 2, 3, 1, 4).reshape(NA_HEADS, n_cb, NA_QBLK_C, kb)
        bias = jnp.where(mask, bias.astype(jnp.float32), -jnp.inf)
        s_win = jnp.einsum('bhnqd,bhnkd->bhnqk', q_row, k_blk).astype(jnp.float32) + bias
        s_meta = jnp.einsum('bhnqd,bhmd->bhnqm', q_row, km).astype(jnp.float32)
        p = jax.nn.softmax(jnp.concatenate([s_win, s_meta], axis=-1), axis=-1).astype(v.dtype)
        o = (jnp.einsum('bhnqk,bhnkd->bhnqd', p[..., :kb], v_blk)
             + jnp.einsum('bhnqm,bhmd->bhnqd', p[..., kb:], vm))
        return o.reshape(bsz, NA_HEADS, GRID_W, NA_HEAD_DIM)

    o_grid = lax.map(attend_row, jnp.arange(rows))
    o_grid = o_grid.transpose(1, 0, 3, 2, 4).reshape(bsz, n_tok, NA_WIDTH)
    p_mm = jax.nn.softmax(jnp.einsum('bhqd,bhkd->bhqk', qm, km).astype(jnp.float32), axis=-1).astype(v.dtype)
    o_meta = jnp.einsum('bhqk,bhkd->bhqd', p_mm, vm).transpose(0, 2, 1, 3).reshape(bsz, N_META, NA_WIDTH)
    return jnp.concatenate([o_meta, o_grid], axis=1)


def moe_ffn(u, w_router, b_router, w_in, b_in, w_out, b_out):
    n_tok = u.shape[0]
    logits = u.astype(jnp.float32) @ w_router.astype(jnp.float32) + b_router.astype(jnp.float32)
    top_val, top_idx = lax.top_k(logits, TOP_K)
    gates = jax.nn.softmax(top_val, axis=-1)
    n_asg = n_tok * TOP_K
    n_pad = (-n_asg) % MOE_BLOCK
    asg_exp = jnp.concatenate([top_idx.reshape(-1).astype(jnp.int32), jnp.full((n_pad,), N_EXPERTS, jnp.int32)])
    asg_tok = jnp.concatenate([jnp.arange(n_asg, dtype=jnp.int32) // TOP_K, jnp.zeros((n_pad,), jnp.int32)])
    asg_gate = jnp.concatenate([gates.reshape(-1), jnp.zeros((n_pad,), jnp.float32)])
    order = jnp.argsort(asg_exp)
    exp_s, tok_s, gate_s = asg_exp[order], asg_tok[order], asg_gate[order]
    n_blocks = (n_asg + n_pad) // MOE_BLOCK
    blk_first = jnp.minimum(exp_s[::MOE_BLOCK], N_EXPERTS - 1)
    blk_last = jnp.minimum(exp_s[MOE_BLOCK - 1::MOE_BLOCK], N_EXPERTS - 1)
    counts = blk_last - blk_first + 1
    ends = jnp.cumsum(counts)
    starts = ends - counts
    n_pairs = n_blocks + N_EXPERTS - 1
    pair = jnp.arange(n_pairs, dtype=jnp.int32)
    pair_blk = jnp.clip(jnp.searchsorted(starts, pair, side='right') - 1, 0, n_blocks - 1).astype(jnp.int32)
    pair_exp = jnp.minimum(blk_first[pair_blk] + pair - starts[pair_blk], N_EXPERTS - 1).astype(jnp.int32)
    pair_live = pair < ends[-1]

    def step(y, inp):
        blk, e, live = inp
        row0 = blk * MOE_BLOCK
        tok = lax.dynamic_slice_in_dim(tok_s, row0, MOE_BLOCK)
        e_rows = lax.dynamic_slice_in_dim(exp_s, row0, MOE_BLOCK)
        g_rows = lax.dynamic_slice_in_dim(gate_s, row0, MOE_BLOCK)
        wgt = jnp.where((e_rows == e) & live, g_rows, 0.0)
        hcat = u[tok] @ w_in[e] + b_in[e]
        x_glu, x_lin = jnp.split(hcat, 2, axis=-1)
        x_glu = jnp.minimum(x_glu, SWIGLU_LIMIT)
        x_lin = jnp.clip(x_lin, -SWIGLU_LIMIT, SWIGLU_LIMIT)
        act = x_glu * jax.nn.sigmoid(SWIGLU_ALPHA * x_glu) * (x_lin + 1.0)
        out = (act @ w_out[e] + b_out[e]).astype(jnp.float32)
        y = y.at[tok].add(out * wgt[:, None])
        return y, None

    y, _ = lax.scan(step, jnp.zeros(u.shape, jnp.float32), (pair_blk, pair_exp, pair_live))
    return y.astype(u.dtype)


def setup_inputs(seed: int = 0) -> dict:
    key = jax.random.key(seed)
    ks = jax.random.split(key, 20)
    f32 = jnp.float32

    def nrm(k, shape, scale):
        return jax.random.normal(k, shape, f32) * scale

    dt = jnp.exp(jax.random.uniform(ks[6], (DEPTH, 2, DN_HEADS), f32, math.log(1e-3), math.log(1e-1)))
    return {
        'x': nrm(ks[0], (BATCH, SEQ, D_MODEL), 1.0),
        'meta_tokens': nrm(ks[1], (N_META, D_MODEL), 1.0),
        'norm_mix': 1.0 + nrm(ks[2], (DEPTH, D_MODEL), 0.02),
        'w_in': nrm(ks[3], (DEPTH, D_MODEL, IN_COLS), D_MODEL ** -0.5),
        'dn_conv': nrm(ks[4], (DEPTH, DN_CONV, 3 * DN_WIDTH), DN_CONV ** -0.5),
        'dn_a_log': jnp.log(jax.random.uniform(ks[5], (DEPTH, 2, DN_HEADS), f32, 1.0, 16.0)),
        'dn_dt_bias': dt + jnp.log(-jnp.expm1(-dt)),
        'dn_out_norm': 1.0 + nrm(ks[7], (DEPTH, DN_HEAD_DIM), 0.02),
        'na_rpb': nrm(ks[8], (DEPTH, NA_HEADS, 2 * NA_WIN_R - 1, 2 * NA_WIN_C - 1), 0.1),
        'w_branch_dn': nrm(ks[9], (DEPTH, DN_WIDTH, D_MODEL), DN_WIDTH ** -0.5),
        'w_branch_na': nrm(ks[10], (DEPTH, NA_WIDTH, D_MODEL), NA_WIDTH ** -0.5),
        'w_out': nrm(ks[11], (DEPTH, D_MODEL, D_MODEL), D_MODEL ** -0.5),
        'norm_ffn': 1.0 + nrm(ks[12], (DEPTH, D_MODEL), 0.02),
        'w_router': nrm(ks[13], (DEPTH, D_MODEL, N_EXPERTS), D_MODEL ** -0.5),
        'b_router': nrm(ks[14], (DEPTH, N_EXPERTS), 0.01),
        'w_exp_in': nrm(ks[15], (DEPTH, N_EXPERTS, D_MODEL, 2 * D_EXPERT), D_MODEL ** -0.5),
        'b_exp_in': nrm(ks[16], (DEPTH, N_EXPERTS, 2 * D_EXPERT), 0.01),
        'w_exp_out': nrm(ks[17], (DEPTH, N_EXPERTS, D_EXPERT, D_MODEL), D_EXPERT ** -0.5),
        'b_exp_out': nrm(ks[18], (DEPTH, N_EXPERTS, D_MODEL), 0.01),
        'norm_final': 1.0 + nrm(ks[19], (D_MODEL,), 0.02),
    }


def reference(x, meta_tokens, norm_mix, w_in, dn_conv, dn_a_log, dn_dt_bias, dn_out_norm, na_rpb,
              w_branch_dn, w_branch_na, w_out, norm_ffn, w_router, b_router, w_exp_in, b_exp_in,
              w_exp_out, b_exp_out, norm_final):
    bsz = x.shape[0]
    meta = jnp.broadcast_to(meta_tokens[None].astype(x.dtype), (bsz, N_META, D_MODEL))
    h = jnp.concatenate([meta, x], axis=1)
    l = h.shape[1]
    split_at = [int(s) for s in np.cumsum(IN_SPLITS)[:-1]]
    for layer in range(DEPTH):
        u = rms_norm(h, norm_mix[layer])
        proj = jnp.einsum('bld,de->ble', u, w_in[layer])
        dq, dk, dv, dz, db, da, nq, nk, nv, ga, gb = jnp.split(proj, split_at, axis=-1)
        y_dn = deltanet_branch(dq, dk, dv, dz, db, da, dn_conv[layer], dn_a_log[layer],
                               dn_dt_bias[layer], dn_out_norm[layer])
        y_na = neighbourhood_branch(nq, nk, nv, na_rpb[layer])
        mix = (jax.nn.sigmoid(ga) * jnp.einsum('blc,cd->bld', y_dn, w_branch_dn[layer])
               + jax.nn.sigmoid(gb) * jnp.einsum('blc,cd->bld', y_na, w_branch_na[layer]))
        h = h + jnp.einsum('bld,de->ble', mix, w_out[layer])
        u = rms_norm(h, norm_ffn[layer])
        ffn = moe_ffn(u.reshape(bsz * l, D_MODEL), w_router[layer], b_router[layer], w_exp_in[layer],
                      b_exp_in[layer], w_exp_out[layer], b_exp_out[layer])
        h = h + ffn.reshape(bsz, l, D_MODEL)
    return rms_norm(h, norm_final)[:, N_META:]
```

```python
import functools

import jax
import jax.numpy as jnp
import numpy as np
from jax import lax
from jax.experimental import pallas as pl
from jax.experimental.pallas import tpu as pltpu

F32 = jnp.float32
BF16 = jnp.bfloat16
I32 = jnp.int32
U32 = jnp.uint32

D_MODEL = 2048
N_META = 16
GRID_W = 64
DN_HEADS = 8
DN_HEAD_DIM = 128
DN_WIDTH = 1024
DN_CONV = 5
DN_CHUNK = 64
NA_HEADS = 8
NA_HEAD_DIM = 128
NA_WIDTH = 1024
NA_WIN_R = 8
NA_WIN_C = 16
N_EXPERTS = 32
TOP_K = 4
D_EXPERT = 2048
SWIGLU_ALPHA = 1.702
SWIGLU_LIMIT = 7.0
NORM_EPS = 1e-6

LANES = 128
VMEM_LIMIT = 56 * 1024 * 1024

COL_DQ, COL_DK, COL_DV, COL_DZ = 0, 8, 16, 24
COL_GA, COL_GB = 32, 48
COL_NQ, COL_NK, COL_NV = 64, 72, 80
N_MAIN_COLS = 88 * LANES

DN_NH = 2
DN_NG = DN_HEADS // DN_NH
DN_NCH = 2 * DN_NH
DN_PAD = (-N_META) % DN_CHUNK

MOE_TM = 512
MOE_TF = 512
MOE_NF = D_EXPERT // MOE_TF
HALF = D_MODEL // 2


def _dot(a, b, **kw):
    return jnp.dot(a, b, preferred_element_type=F32, **kw)


def _dot_nt(a, b, **kw):
    return lax.dot_general(a, b, (((1,), (1,)), ((), ())), preferred_element_type=F32, **kw)


def _dot_tn(a, b, **kw):
    return lax.dot_general(a, b, (((0,), (0,)), ((), ())), preferred_element_type=F32, **kw)


def _sigmoid(x):
    return 1.0 / (1.0 + jnp.exp(-x))


def _pack_halves(x):
    n = x.shape[1] // 2
    lo = pltpu.bitcast(x[:, :n].astype(BF16).astype(F32), U32)
    hi = pltpu.bitcast(x[:, n:].astype(BF16).astype(F32), U32)
    return (hi & jnp.uint32(0xFFFF0000)) | (lo >> 16)


def _unpack_halves(p):
    lo = pltpu.bitcast(p << 16, F32)
    hi = pltpu.bitcast(p & jnp.uint32(0xFFFF0000), F32)
    return lo, hi


def _inproj_body(h_ref, g_ref, w_ref, wba_ref, o_ref, oba_ref, u_ref):
    @pl.when(pl.program_id(1) == 0)
    def _():
        x = h_ref[...]
        y = x * lax.rsqrt(jnp.mean(x * x, axis=-1, keepdims=True) + NORM_EPS) * g_ref[...]
        ub = y.astype(BF16)
        u_ref[...] = ub
        oba_ref[...] = _dot(ub, wba_ref[...])

    o_ref[...] = _dot(u_ref[...], w_ref[...]).astype(BF16)


def _inproj(h, gain, w_main, w_ba, tm, tn):
    t, d = h.shape
    n = w_main.shape[1]
    nba = w_ba.shape[1]
    return pl.pallas_call(
        _inproj_body,
        grid=(t // tm, n // tn),
        in_specs=[
            pl.BlockSpec((tm, d), lambda i, j: (i, 0)),
            pl.BlockSpec((1, d), lambda i, j: (0, 0)),
            pl.BlockSpec((d, tn), lambda i, j: (0, j)),
            pl.BlockSpec((d, nba), lambda i, j: (0, 0)),
        ],
        out_specs=[
            pl.BlockSpec((tm, tn), lambda i, j: (i, j)),
            pl.BlockSpec((tm, nba), lambda i, j: (i, 0)),
        ],
        out_shape=[jax.ShapeDtypeStruct((t, n), BF16), jax.ShapeDtypeStruct((t, nba), F32)],
        scratch_shapes=[pltpu.VMEM((tm, d), BF16)],
        compiler_params=pltpu.CompilerParams(
            dimension_semantics=("arbitrary", "arbitrary"), vmem_limit_bytes=VMEM_LIMIT),
        name="inproj",
    )(h, gain, w_main, w_ba)


def _neumann_inverse(a):
    c = a.shape[0]
    eye = (lax.broadcasted_iota(I32, (c, c), 0) == lax.broadcasted_iota(I32, (c, c), 1)).astype(F32)
    n = -a
    p = eye + n
    steps = int(np.log2(c)) - 1
    for _ in range(steps):
        nb = n.astype(BF16)
        n = _dot(nb, nb)
        p = p + _dot(p.astype(BF16), n.astype(BF16))
    return p


def _dn_body(q_ref, k_ref, v_ref, z_ref, ba_ref, cq_ref, ck_ref, cv_ref, gp_ref, on_ref, y_ref,
             xq, xk, xv, qn, kn, vn, gbs, us, ws, qds, kdts, qks, cds, of_s, ob_s, s_s, *, seq_len):
    nh = DN_NH
    wd = nh * DN_HEAD_DIM
    c64 = DN_CHUNK
    n_chunks = (seq_len + DN_PAD) // c64
    n_rows = n_chunks * c64
    lead = 8 + DN_PAD
    copy_rows = 48
    n_copy = seq_len // copy_rows

    for x in (xq, xk, xv):
        x[0:lead, :] = jnp.zeros((lead, wd), F32)
        x[lead + seq_len:lead + seq_len + 8, :] = jnp.zeros((8, wd), F32)
    gbs[0:DN_PAD, :] = jnp.zeros((DN_PAD, LANES), F32)

    lane = lax.broadcasted_iota(I32, (copy_rows, LANES), 1)
    neg_a = -jnp.exp(gp_ref[0, 0:1, :])
    dt_b = gp_ref[0, 1:2, :]

    def copy_step(j, carry):
        src = pl.ds(pl.multiple_of(j * copy_rows, 16), copy_rows)
        dst = pl.ds(pl.multiple_of(lead + j * copy_rows, 8), copy_rows)
        xq[dst, :] = q_ref[0, src, :].astype(F32)
        xk[dst, :] = k_ref[0, src, :].astype(F32)
        xv[dst, :] = v_ref[0, src, :].astype(F32)
        ba = ba_ref[0, src, :]
        beta = _sigmoid(ba)
        sp_in = ba + dt_b
        softplus = jnp.maximum(sp_in, 0.0) + jnp.log(1.0 + jnp.exp(-jnp.abs(sp_in)))
        val = jnp.where(lane < 2 * nh, beta, jnp.where(lane < 4 * nh, neg_a * softplus, 0.0))
        gbs[pl.ds(pl.multiple_of(DN_PAD + j * copy_rows, 8), copy_rows), :] = val
        return carry

    lax.fori_loop(0, n_copy, copy_step, 0)

    row_iota = lax.broadcasted_iota(I32, (c64, wd), 0)

    def conv_step(c, carry):
        r0 = pl.multiple_of(c * c64, c64)
        live = (row_iota + r0) >= DN_PAD
        for x, cw, dst, kind in ((xq, cq_ref, qn, "q"), (xk, ck_ref, kn, "k"), (xv, cv_ref, vn, "v")):
            win = x[pl.ds(r0, c64 + 16), :]
            acc = jnp.zeros((c64, wd), F32)
            for j in range(DN_CONV):
                off = 8 - DN_CONV // 2 + j
                acc = acc + win[off:off + c64, :] * cw[j:j + 1, :]
            y = jnp.where(live, acc * _sigmoid(acc), 0.0)
            if kind == "v":
                dst[pl.ds(r0, c64), :] = y
            else:
                scale = DN_HEAD_DIM ** -0.5 if kind == "q" else 1.0
                for hh in range(nh):
                    yh = y[:, hh * LANES:(hh + 1) * LANES]
                    yn = yh * lax.rsqrt(jnp.sum(yh * yh, axis=-1, keepdims=True) + 1e-6)
                    dst[pl.ds(r0, c64), hh * LANES:(hh + 1) * LANES] = yn * scale
        return carry

    lax.fori_loop(0, n_chunks, conv_step, 0)

    ri = lax.broadcasted_iota(I32, (c64, c64), 0)
    ci = lax.broadcasted_iota(I32, (c64, c64), 1)
    incl = (ri >= ci, ri <= ci)
    strict = (ri > ci, ri < ci)
    cum = tuple(m.astype(F32) for m in incl)

    def chunk_step(c, carry):
        r0 = pl.multiple_of(c * c64, c64)
        gb = gbs[pl.ds(r0, c64), :]
        gams = tuple(_dot(cum[d], gb, precision=lax.Precision.HIGHEST) for d in range(2))
        gam_ts = tuple(g.T for g in gams)
        for hh in range(nh):
            hs = slice(hh * LANES, (hh + 1) * LANES)
            q = qn[pl.ds(r0, c64), hs]
            k = kn[pl.ds(r0, c64), hs]
            v = vn[pl.ds(r0, c64), hs]
            kb = k.astype(BF16)
            k_t = k.T
            kk = _dot_nt(kb, kb)
            qk = _dot_nt(q.astype(BF16), kb)
            for d in range(2):
                ch = hh * 2 + d
                bcol = d * nh + hh
                gcol = 2 * nh + d * nh + hh
                beta_c = gb[:, bcol:bcol + 1]
                gam_c = gams[d][:, gcol:gcol + 1]
                gam_r = gam_ts[d][gcol:gcol + 1, :]
                last = c64 - 1 if d == 0 else 0
                gam_last = gam_c[last:last + 1, :]
                decay = jnp.exp(jnp.where(incl[d], gam_c - gam_r, -jnp.inf))
                a = jnp.where(strict[d], beta_c * kk * decay, 0.0)
                t_inv = _neumann_inverse(a)
                e_g = jnp.exp(gam_c)
                rhs = jnp.concatenate([beta_c * v, (beta_c * e_g) * k], axis=1)
                sol = _dot(t_inv.astype(BF16), rhs.astype(BF16))
                us[ch, pl.ds(r0, c64), :] = sol[:, :LANES].astype(BF16)
                ws[ch, pl.ds(r0, c64), :] = sol[:, LANES:].astype(BF16)
                qds[ch, pl.ds(r0, c64), :] = (q * e_g).astype(BF16)
                qks[ch, pl.ds(r0, c64), :] = (qk * decay).astype(BF16)
                kdts[ch, pl.ds(pl.multiple_of(c * LANES, LANES), LANES), :] = (
                    k_t * jnp.exp(gam_last - gam_r)).astype(BF16)
                cds[ch, pl.ds(pl.multiple_of(c * 8, 8), 8), :] = jnp.broadcast_to(
                    jnp.exp(gam_last), (8, LANES))
        return carry

    lax.fori_loop(0, n_chunks, chunk_step, 0)

    s_s[...] = jnp.zeros(s_s.shape, F32)

    def rec_step(i, carry):
        for hh in range(nh):
            hs = slice(hh * LANES, (hh + 1) * LANES)
            for d in range(2):
                ch = hh * 2 + d
                c = i if d == 0 else n_chunks - 1 - i
                r0 = pl.multiple_of(c * c64, c64)
                state = s_s[ch]
                sb = state.astype(BF16)
                v_new = us[ch, pl.ds(r0, c64), :].astype(F32) - _dot(ws[ch, pl.ds(r0, c64), :], sb)
                vb = v_new.astype(BF16)
                o = _dot(qds[ch, pl.ds(r0, c64), :], sb) + _dot(qks[ch, pl.ds(r0, c64), :], vb)
                cd = cds[ch, pl.ds(pl.multiple_of(c * 8, 8), 8), :][0:1, :]
                s_s[ch] = state * cd + _dot(kdts[ch, pl.ds(pl.multiple_of(c * LANES, LANES), LANES), :], vb)
                (of_s if d == 0 else ob_s)[pl.ds(r0, c64), hs] = o
        return carry

    lax.fori_loop(0, n_chunks, rec_step, 0)

    gain = on_ref[...]

    def emit(o, z):
        outs = []
        for hh in range(nh):
            oh = o[:, hh * LANES:(hh + 1) * LANES]
            outs.append(oh * lax.rsqrt(jnp.mean(oh * oh, axis=-1, keepdims=True) + NORM_EPS) * gain)
        on = jnp.concatenate(outs, axis=1)
        zf = z.astype(F32)
        return (on * (zf * _sigmoid(zf))).astype(BF16)

    n0 = c64 - DN_PAD
    o0 = of_s[DN_PAD:c64, :] + ob_s[DN_PAD:c64, :]
    y_ref[0, 0:n0, :] = emit(o0, z_ref[0, 0:n0, :])

    def out_step(c, carry):
        r0 = pl.multiple_of(c * c64, c64)
        l0 = pl.multiple_of(c * c64 - DN_PAD, 16)
        o = of_s[pl.ds(r0, c64), :] + ob_s[pl.ds(r0, c64), :]
        y_ref[0, pl.ds(l0, c64), :] = emit(o, z_ref[0, pl.ds(l0, c64), :])
        return carry

    lax.fori_loop(1, n_chunks, out_step, 0)


def _deltanet(proj3, ba3, conv_w, gpar, out_norm):
    b, l, _ = proj3.shape
    nh = DN_NH
    wd = nh * DN_HEAD_DIM
    n_rows = l + DN_PAD
    n_chunks = n_rows // DN_CHUNK
    ng = DN_NG
    blk = lambda col0: pl.BlockSpec((1, l, wd), lambda i, g, col0=col0: (i, 0, col0 // nh + g))
    cblk = lambda part: pl.BlockSpec((DN_CONV, wd), lambda i, g, part=part: (0, part * ng + g))
    return pl.pallas_call(
        functools.partial(_dn_body, seq_len=l),
        grid=(b, ng),
        in_specs=[
            blk(COL_DQ), blk(COL_DK), blk(COL_DV), blk(COL_DZ),
            pl.BlockSpec((1, l, LANES), lambda i, g: (i, 0, g)),
            cblk(0), cblk(1), cblk(2),
            pl.BlockSpec((1, 2, LANES), lambda i, g: (g, 0, 0)),
            pl.BlockSpec((1, LANES), lambda i, g: (0, 0)),
        ],
        out_specs=pl.BlockSpec((1, l, wd), lambda i, g: (i, 0, g)),
        out_shape=jax.ShapeDtypeStruct((b, l, DN_WIDTH), BF16),
        scratch_shapes=[
            pltpu.VMEM((n_rows + 16, wd), F32), pltpu.VMEM((n_rows + 16, wd), F32),
            pltpu.VMEM((n_rows + 16, wd), F32),
            pltpu.VMEM((n_rows, wd), F32), pltpu.VMEM((n_rows, wd), F32), pltpu.VMEM((n_rows, wd), F32),
            pltpu.VMEM((n_rows, LANES), F32),
            pltpu.VMEM((DN_NCH, n_rows, LANES), BF16), pltpu.VMEM((DN_NCH, n_rows, LANES), BF16),
            pltpu.VMEM((DN_NCH, n_rows, LANES), BF16),
            pltpu.VMEM((DN_NCH, n_chunks * LANES, DN_CHUNK), BF16),
            pltpu.VMEM((DN_NCH, n_rows, DN_CHUNK), BF16),
            pltpu.VMEM((DN_NCH, n_chunks * 8, LANES), F32),
            pltpu.VMEM((n_rows, wd), F32), pltpu.VMEM((n_rows, wd), F32),
            pltpu.VMEM((DN_NCH, LANES, LANES), F32),
        ],
        compiler_params=pltpu.CompilerParams(
            dimension_semantics=("arbitrary", "arbitrary"), vmem_limit_bytes=VMEM_LIMIT),
        name="deltanet",
    )(proj3, proj3, proj3, proj3, ba3, conv_w, conv_w, conv_w, gpar, out_norm)


def _na_body(q_ref, k_ref, v_ref, tb_ref, o_ref, *, rows):
    scale = NA_HEAD_DIM ** -0.5
    wr = NA_WIN_R
    nk = wr * GRID_W
    qm = q_ref[0, 0:N_META, :]
    km = k_ref[0, 0:N_META, :]
    vm = v_ref[0, 0:N_META, :]

    s = _dot_nt(qm, km) * scale
    p = jnp.exp(s - jnp.max(s, axis=-1, keepdims=True))
    o = _dot(p.astype(BF16), vm) / jnp.sum(p, axis=-1, keepdims=True)
    o_ref[0, 0:N_META, :] = o.astype(BF16)

    def row_step(r, carry):
        rs = jnp.clip(r - wr // 2, 0, rows - wr)
        i0 = rs - r + NA_WIN_R - 1
        q = q_ref[0, pl.ds(pl.multiple_of(N_META + r * GRID_W, 16), GRID_W), :]
        kw = k_ref[0, pl.ds(pl.multiple_of(N_META + rs * GRID_W, 16), nk), :]
        vw = v_ref[0, pl.ds(pl.multiple_of(N_META + rs * GRID_W, 16), nk), :]
        s_win = _dot_nt(q, kw) * scale + tb_ref[0, i0]
        s_meta = _dot_nt(q, km) * scale
        m = jnp.maximum(jnp.max(s_win, axis=-1, keepdims=True), jnp.max(s_meta, axis=-1, keepdims=True))
        p_win = jnp.exp(s_win - m)
        p_meta = jnp.exp(s_meta - m)
        den = jnp.sum(p_win, axis=-1, keepdims=True) + jnp.sum(p_meta, axis=-1, keepdims=True)
        o = (_dot(p_win.astype(BF16), vw) + _dot(p_meta.astype(BF16), vm)) / den
        o_ref[0, pl.ds(pl.multiple_of(N_META + r * GRID_W, 16), GRID_W), :] = o.astype(BF16)
        return carry

    lax.fori_loop(0, rows, row_step, 0)


def _na_bias_table(rpb):
    wr, wc = NA_WIN_R, NA_WIN_C
    qc = np.arange(GRID_W)[:, None]
    kc = np.arange(GRID_W)[None, :]
    q_start = np.clip(qc - wc // 2, 0, GRID_W - wc)
    valid = (kc - q_start >= 0) & (kc - q_start < wc)
    dc = np.clip(kc - qc + wc - 1, 0, 2 * wc - 2)
    i0 = np.arange(wr)[:, None]
    jj = np.arange(wr)[None, :]
    dr = i0 + jj
    tb = rpb.astype(F32)[:, dr][:, :, :, dc]
    tb = jnp.where(valid[None, None, None], tb, -jnp.inf)
    tb = tb.transpose(0, 1, 3, 2, 4)
    return tb.reshape(NA_HEADS, wr, GRID_W, wr * GRID_W)


def _natten(proj3, tb):
    b, l, _ = proj3.shape
    rows = (l - N_META) // GRID_W
    assert rows >= NA_WIN_R
    blk = lambda col0: pl.BlockSpec((1, l, LANES), lambda h, i, col0=col0: (i, 0, col0 + h))
    return pl.pallas_call(
        functools.partial(_na_body, rows=rows),
        grid=(NA_HEADS, b),
        in_specs=[
            blk(COL_NQ), blk(COL_NK), blk(COL_NV),
            pl.BlockSpec((1, NA_WIN_R, GRID_W, NA_WIN_R * GRID_W), lambda h, i: (h, 0, 0, 0)),
        ],
        out_specs=pl.BlockSpec((1, l, LANES), lambda h, i: (i, 0, h)),
        out_shape=jax.ShapeDtypeStruct((b, l, NA_WIDTH), BF16),
        compiler_params=pltpu.CompilerParams(
            dimension_semantics=("arbitrary", "arbitrary"), vmem_limit_bytes=VMEM_LIMIT),
        name="natten",
    )(proj3, proj3, proj3, tb)


def _mix_body(ydn_ref, yna_ref, ga_ref, gb_ref, h_ref, wdn_ref, wna_ref, wo_ref, gn_ref, wr_ref, br_ref,
              h2_ref, u2_ref, gate_ref, idx_ref):
    a = _dot(ydn_ref[...], wdn_ref[...])
    b = _dot(yna_ref[...], wna_ref[...])
    mix = _sigmoid(ga_ref[...].astype(F32)) * a + _sigmoid(gb_ref[...].astype(F32)) * b
    h2 = h_ref[...] + _dot(mix.astype(BF16), wo_ref[...])
    h2_ref[...] = h2
    u2 = h2 * lax.rsqrt(jnp.mean(h2 * h2, axis=-1, keepdims=True) + NORM_EPS) * gn_ref[...]
    u2_ref[...] = _pack_halves(u2)
    logits = _dot(u2, wr_ref[...], precision=lax.Precision.HIGHEST) + br_ref[...]
    lane = lax.broadcasted_iota(I32, logits.shape, 1)
    logits = jnp.where(lane < N_EXPERTS, logits, -jnp.inf)
    vals, idxs = [], []
    for _ in range(TOP_K):
        m = jnp.max(logits, axis=-1, keepdims=True)
        sel = jnp.min(jnp.where(logits == m, lane, LANES), axis=-1, keepdims=True)
        vals.append(m)
        idxs.append(sel)
        logits = jnp.where(lane == sel, -jnp.inf, logits)
    es = [jnp.exp(v - vals[0]) for v in vals]
    den = es[0] + es[1] + es[2] + es[3]
    gates = jnp.zeros(logits.shape, F32)
    idx = jnp.zeros(logits.shape, I32)
    for k in range(TOP_K):
        gates = jnp.where(lane == k, es[k] / den, gates)
        idx = jnp.where(lane == k, idxs[k], idx)
    gate_ref[...] = gates
    idx_ref[...] = idx


def _mix(y_dn, y_na, proj, h, w_dn, w_na, w_o, gain, w_r, b_r, tm):
    t, d = h.shape
    row = lambda w: pl.BlockSpec((tm, w), lambda i: (i, 0))
    full = lambda a: pl.BlockSpec(a.shape, lambda i: (0,) * a.ndim)
    return pl.pallas_call(
        _mix_body,
        grid=(t // tm,),
        in_specs=[
            row(DN_WIDTH), row(NA_WIDTH),
            pl.BlockSpec((tm, d), lambda i: (i, COL_GA * LANES // d)),
            pl.BlockSpec((tm, d), lambda i: (i, COL_GB * LANES // d)),
            row(d), full(w_dn), full(w_na), full(w_o), full(gain), full(w_r), full(b_r),
        ],
        out_specs=[row(d), row(d // 2), row(LANES), row(LANES)],
        out_shape=[
            jax.ShapeDtypeStruct((t, d), F32), jax.ShapeDtypeStruct((t, d // 2), U32),
            jax.ShapeDtypeStruct((t, LANES), F32), jax.ShapeDtypeStruct((t, LANES), I32),
        ],
        compiler_params=pltpu.CompilerParams(
            dimension_semantics=("arbitrary",), vmem_limit_bytes=VMEM_LIMIT),
        name="mix",
    )(y_dn, y_na, proj, proj, h, w_dn, w_na, w_o, gain, w_r, b_r)


def _moe_body(blk_exp, n_valid, idx_hbm, u_hbm, wg_ref, wl_ref, bg_ref, bl_ref, wo_ref, bo_ref, y_hbm,
              idx_s, xbuf, xs, acc, stage, isem, gsem, ssem, *, n_blocks):
    i = pl.program_id(0)
    f = pl.program_id(1)
    slot = i % 2
    nxt = (i + 1) % 2
    tm = MOE_TM

    def idx_copy(blk, sl):
        return pltpu.make_async_copy(idx_hbm.at[blk], idx_s.at[sl], isem.at[sl])

    def gather_rows(sl):
        def body(r, carry):
            tok = idx_s[sl, 0, r]
            pltpu.make_async_copy(u_hbm.at[pl.ds(tok, 1), :], xbuf.at[sl, pl.ds(r, 1), :], gsem.at[sl]).start()
            return carry
        lax.fori_loop(0, tm, body, 0)

    def gather_wait(sl):
        pltpu.make_async_copy(xbuf.at[sl], xbuf.at[sl], gsem.at[sl]).wait()

    def scatter_rows(sl, n):
        def body(r, carry):
            dst = idx_s[sl, 1, r]
            pltpu.make_async_copy(stage.at[pl.ds(r, 1), :], y_hbm.at[pl.ds(dst, 1), :], ssem).start()
            return carry
        lax.fori_loop(0, n, body, 0)

    def scatter_wait(n):
        def body(r, carry):
            pltpu.make_async_copy(stage.at[pl.ds(0, 1), :], y_hbm.at[pl.ds(0, 1), :], ssem).wait()
            return carry
        lax.fori_loop(0, n, body, 0)

    live = n_valid[i] > 0

    @pl.when((i == 0) & (f == 0))
    def _():
        idx_copy(0, 0).start()
        idx_copy(0, 0).wait()

        @pl.when(live)
        def _():
            gather_rows(0)

    @pl.when((f == 0) & live)
    def _():
        gather_wait(slot)
        lo, hi = _unpack_halves(xbuf[slot])
        xs[:, :HALF] = lo.astype(BF16)
        xs[:, HALF:] = hi.astype(BF16)

    @pl.when((f == 0) & (i + 1 < n_blocks))
    def _():
        idx_copy(i + 1, nxt).start()

    @pl.when((f == 1) & (i + 1 < n_blocks))
    def _():
        idx_copy(i + 1, nxt).wait()

        @pl.when(n_valid[jnp.minimum(i + 1, n_blocks - 1)] > 0)
        def _():
            gather_rows(nxt)

    @pl.when(live)
    def _():
        x = xs[...]
        hg = _dot(x, wg_ref[0]) + bg_ref[0]
        hl = _dot(x, wl_ref[0]) + bl_ref[0]
        glu = jnp.minimum(hg, SWIGLU_LIMIT)
        lin = jnp.clip(hl, -SWIGLU_LIMIT, SWIGLU_LIMIT)
        act = glu * _sigmoid(SWIGLU_ALPHA * glu) * (lin + 1.0)
        part = _dot(act.astype(BF16), wo_ref[0])

        @pl.when(f == 0)
        def _():
            acc[...] = part

        @pl.when(f > 0)
        def _():
            acc[...] += part

    @pl.when(f == MOE_NF - 1)
    def _():
        @pl.when(i > 0)
        def _():
            scatter_wait(n_valid[jnp.maximum(i - 1, 0)])

        @pl.when(live)
        def _():
            stage[...] = _pack_halves(acc[...] + bo_ref[0])
            scatter_rows(slot, n_valid[i])

        @pl.when(i == n_blocks - 1)
        def _():
            scatter_wait(n_valid[i])


def _moe(blk_exp, n_valid, idx_blocks, u2p, w_in, b_in, w_out, b_out, n_rows_out):
    n_blocks = idx_blocks.shape[0]
    d = D_MODEL
    nf = MOE_NF

    def wmap(off):
        def index_map(i, f, be, nv):
            fe = jnp.where(nv[i] > 0, f, nf - 1)
            return (be[i], 0, fe + off)
        return index_map

    def womap(i, f, be, nv):
        return (be[i], jnp.where(nv[i] > 0, f, nf - 1), 0)

    grid_spec = pltpu.PrefetchScalarGridSpec(
        num_scalar_prefetch=2,
        grid=(n_blocks, nf),
        in_specs=[
            pl.BlockSpec(memory_space=pl.ANY),
            pl.BlockSpec(memory_space=pl.ANY),
            pl.BlockSpec((1, d, MOE_TF), wmap(0)),
            pl.BlockSpec((1, d, MOE_TF), wmap(nf)),
            pl.BlockSpec((1, 1, MOE_TF), wmap(0)),
            pl.BlockSpec((1, 1, MOE_TF), wmap(nf)),
            pl.BlockSpec((1, MOE_TF, d), womap),
            pl.BlockSpec((1, 1, d), lambda i, f, be, nv: (be[i], 0, 0)),
        ],
        out_specs=pl.BlockSpec(memory_space=pl.ANY),
        scratch_shapes=[
            pltpu.SMEM((2, 2, MOE_TM), I32),
            pltpu.VMEM((2, MOE_TM, HALF), U32),
            pltpu.VMEM((MOE_TM, d), BF16),
            pltpu.VMEM((MOE_TM, d), F32),
            pltpu.VMEM((MOE_TM, HALF), U32),
            pltpu.SemaphoreType.DMA((2,)),
            pltpu.SemaphoreType.DMA((2,)),
            pltpu.SemaphoreType.DMA,
        ],
    )
    return pl.pallas_call(
        functools.partial(_moe_body, n_blocks=n_blocks),
        grid_spec=grid_spec,
        out_shape=jax.ShapeDtypeStruct((n_rows_out, HALF), U32),
        compiler_params=pltpu.CompilerParams(
            dimension_semantics=("arbitrary", "arbitrary"), vmem_limit_bytes=VMEM_LIMIT),
        name="moe",
    )(blk_exp, n_valid, idx_blocks, u2p, w_in, w_in, b_in, b_in, w_out, b_out)


def _routing_metadata(top_idx, n_blocks):
    tm = MOE_TM
    e = top_idx[:, :TOP_K].reshape(-1)
    n_asg = e.shape[0]
    onehot = (e[:, None] == jnp.arange(N_EXPERTS, dtype=I32)[None, :]).astype(I32)
    csum = jnp.cumsum(onehot, axis=0)
    rank = jnp.take_along_axis(csum, e[:, None], axis=1)[:, 0] - 1
    counts = csum[-1]
    nblk = (counts + tm - 1) // tm
    bend = jnp.cumsum(nblk)
    bstart = bend - nblk
    pos = bstart[e] * tm + rank
    asg = jnp.arange(n_asg, dtype=I32)
    n_slots = n_blocks * tm
    tok_sorted = jnp.zeros((n_slots,), I32).at[pos].set(asg // TOP_K)
    dst_sorted = jnp.zeros((n_slots,), I32).at[pos].set(asg)
    blk = jnp.arange(n_blocks, dtype=I32)
    blk_exp = jnp.minimum(jnp.searchsorted(bend, blk, side="right"), N_EXPERTS - 1).astype(I32)
    n_valid = jnp.clip(counts[blk_exp] - (blk - bstart[blk_exp]) * tm, 0, tm)
    n_valid = jnp.where(blk < bend[-1], n_valid, 0).astype(I32)
    idx_blocks = jnp.stack([tok_sorted.reshape(n_blocks, tm), dst_sorted.reshape(n_blocks, tm)], axis=1)
    return blk_exp, n_valid, idx_blocks


def _final_body(h_ref, y_ref, gate_ref, gn_ref, o_ref):
    h = h_ref[...]
    gates = gate_ref[...]
    lo_acc = jnp.zeros((h.shape[0], HALF), F32)
    hi_acc = jnp.zeros((h.shape[0], HALF), F32)
    for k in range(TOP_K):
        lo, hi = _unpack_halves(y_ref[:, k * HALF:(k + 1) * HALF])
        g = gates[:, k:k + 1]
        lo_acc = lo_acc + g * lo
        hi_acc = hi_acc + g * hi
    y = h + jnp.concatenate([lo_acc, hi_acc], axis=1)
    o_ref[...] = y * lax.rsqrt(jnp.mean(y * y, axis=-1, keepdims=True) + NORM_EPS) * gn_ref[...]


def _final(h2, y4, gates, gain, tm):
    t, d = h2.shape
    return pl.pallas_call(
        _final_body,
        grid=(t // tm,),
        in_specs=[
            pl.BlockSpec((tm, d), lambda i: (i, 0)),
            pl.BlockSpec((tm, TOP_K * HALF), lambda i: (i, 0)),
            pl.BlockSpec((tm, LANES), lambda i: (i, 0)),
            pl.BlockSpec((1, d), lambda i: (0, 0)),
        ],
        out_specs=pl.BlockSpec((tm, d), lambda i: (i, 0)),
        out_shape=jax.ShapeDtypeStruct((t, d), F32),
        compiler_params=pltpu.CompilerParams(
            dimension_semantics=("arbitrary",), vmem_limit_bytes=VMEM_LIMIT),
        name="final",
    )(h2, y4, gates, gain)


def _split_w_in(w_in):
    n_dn = 4 * DN_WIDTH
    n_na = n_dn + 4 * DN_HEADS
    w_main = jnp.concatenate(
        [w_in[:, :n_dn], w_in[:, n_na + 3 * NA_WIDTH:], w_in[:, n_na:n_na + 3 * NA_WIDTH]], axis=1).astype(BF16)
    wb = w_in[:, n_dn:n_dn + 2 * DN_HEADS]
    wa = w_in[:, n_dn + 2 * DN_HEADS:n_dn + 4 * DN_HEADS]
    return w_main, _group_lanes(wb, wa).astype(BF16)


def _group_lanes(b_part, a_part):
    nh, ng = DN_NH, DN_NG
    lead = b_part.shape[:-1]
    bp = b_part.reshape(lead + (2, ng, nh))
    ap = a_part.reshape(lead + (2, ng, nh))
    perm = tuple(range(len(lead))) + (len(lead) + 1, len(lead), len(lead) + 2)
    bp = bp.transpose(perm).reshape(lead + (ng, 2 * nh))
    ap = ap.transpose(perm).reshape(lead + (ng, 2 * nh))
    both = jnp.concatenate([bp, ap], axis=-1)
    pad = [(0, 0)] * (both.ndim - 1) + [(0, LANES - 4 * nh)]
    return jnp.pad(both, pad).reshape(lead + (ng * LANES,))


def kernel(x, meta_tokens, norm_mix, w_in, dn_conv, dn_a_log, dn_dt_bias, dn_out_norm, na_rpb,
           w_branch_dn, w_branch_na, w_out, norm_ffn, w_router, b_router, w_exp_in, b_exp_in,
           w_exp_out, b_exp_out, norm_final):
    bsz, seq, d = x.shape
    l = seq + N_META
    t = bsz * l
    meta = jnp.broadcast_to(meta_tokens[None].astype(x.dtype), (bsz, N_META, d))
    h = jnp.concatenate([meta, x], axis=1).reshape(t, d)
    depth = w_in.shape[0]
    assert depth == 1, "the final kernel fuses the last residual add with the final RMSNorm"
    tm_proj = 768 if t % 768 == 0 else 8 * (l // 8)
    tm_mix = 384 if t % 384 == 0 else 8 * (l // 8)
    n_blocks = t * TOP_K // MOE_TM + N_EXPERTS
    for layer in range(depth):
        w_main, w_ba = _split_w_in(w_in[layer])
        proj, ba = _inproj(h, norm_mix[layer][None, :], w_main, w_ba, tm_proj, 1024)
        proj3 = proj.reshape(bsz, l, N_MAIN_COLS)
        ba3 = ba.reshape(bsz, l, DN_NG * LANES)
        gpar = jnp.stack([_group_lanes(jnp.zeros_like(dn_a_log[layer]).reshape(-1), dn_a_log[layer].reshape(-1)),
                          _group_lanes(jnp.zeros_like(dn_dt_bias[layer]).reshape(-1),
                                       dn_dt_bias[layer].reshape(-1))], axis=0)
        gpar = gpar.reshape(2, DN_NG, LANES).transpose(1, 0, 2)
        y_dn = _deltanet(proj3, ba3, dn_conv[layer], gpar, dn_out_norm[layer][None, :])
        y_na = _natten(proj3, _na_bias_table(na_rpb[layer]))
        w_r = jnp.pad(w_router[layer], ((0, 0), (0, LANES - N_EXPERTS)))
        b_r = jnp.pad(b_router[layer], (0, LANES - N_EXPERTS))[None, :]
        h2, u2p, gates, top_idx = _mix(
            y_dn.reshape(t, DN_WIDTH), y_na.reshape(t, NA_WIDTH), proj, h,
            w_branch_dn[layer].astype(BF16), w_branch_na[layer].astype(BF16), w_out[layer].astype(BF16),
            norm_ffn[layer][None, :], w_r, b_r, tm_mix)
        blk_exp, n_valid, idx_blocks = _routing_metadata(top_idx, n_blocks)
        y4 = _moe(blk_exp, n_valid, idx_blocks, u2p,
                  w_exp_in[layer].astype(BF16), b_exp_in[layer][:, None, :],
                  w_exp_out[layer].astype(BF16), b_exp_out[layer][:, None, :], t * TOP_K)
        h = _final(h2, y4.reshape(t, TOP_K * HALF), gates, norm_final[None, :], tm_mix)
    return h.reshape(bsz, l, d)[:, N_META:]
```

```python
import functools

import jax
import jax.numpy as jnp
import numpy as np
from jax import lax
from jax.experimental import pallas as pl
from jax.experimental.pallas import tpu as pltpu

F32 = jnp.float32
BF16 = jnp.bfloat16
I32 = jnp.int32
U32 = jnp.uint32

D_MODEL = 2048
N_META = 16
GRID_W = 64
DN_HEADS = 8
DN_HEAD_DIM = 128
DN_WIDTH = 1024
DN_CONV = 5
DN_CHUNK = 64
NA_HEADS = 8
NA_HEAD_DIM = 128
NA_WIDTH = 1024
NA_WIN_R = 8
NA_WIN_C = 16
N_EXPERTS = 32
TOP_K = 4
D_EXPERT = 2048
SWIGLU_ALPHA = 1.702
SWIGLU_LIMIT = 7.0
NORM_EPS = 1e-6

LANES = 128
VMEM_LIMIT = 56 * 1024 * 1024

COL_DQ, COL_DK, COL_DV, COL_DZ = 0, 8, 16, 24
COL_GA, COL_GB = 32, 48
COL_NQ, COL_NK, COL_NV = 64, 72, 80
N_MAIN_COLS = 88 * LANES

DN_NH = 2
DN_NG = DN_HEADS // DN_NH
DN_NCH = 2 * DN_NH
DN_PAD = (-N_META) % DN_CHUNK
DN_CHUNK_UNROLL = 3

DMA_ISSUE_UNROLL = 8

MOE_TM = 512
MOE_TF = 512
MOE_NF = D_EXPERT // MOE_TF
HALF = D_MODEL // 2
SUBLANES = 8
assert HALF == SUBLANES * LANES


def _store_token_tiles(ref, packed, n_tok, base=0, group=SUBLANES):
    for s in range(SUBLANES):
        ref[pl.ds(base + s, n_tok, stride=group), :] = packed[:, s * LANES:(s + 1) * LANES]


def _load_token_tiles(ref, n_tok, base=0, group=SUBLANES):
    return jnp.concatenate([ref[pl.ds(base + s, n_tok, stride=group), :] for s in range(SUBLANES)], axis=1)


def _dot(a, b, **kw):
    return jnp.dot(a, b, preferred_element_type=F32, **kw)


def _dot_nt(a, b, **kw):
    return lax.dot_general(a, b, (((1,), (1,)), ((), ())), preferred_element_type=F32, **kw)


def _dot_tn(a, b, **kw):
    return lax.dot_general(a, b, (((0,), (0,)), ((), ())), preferred_element_type=F32, **kw)


def _sigmoid(x):
    return 1.0 / (1.0 + jnp.exp(-x))


def _pack_halves(x):
    n = x.shape[1] // 2
    lo = pltpu.bitcast(x[:, :n].astype(BF16).astype(F32), U32)
    hi = pltpu.bitcast(x[:, n:].astype(BF16).astype(F32), U32)
    return (hi & jnp.uint32(0xFFFF0000)) | (lo >> 16)


def _unpack_halves(p):
    lo = pltpu.bitcast(p << 16, F32)
    hi = pltpu.bitcast(p & jnp.uint32(0xFFFF0000), F32)
    return lo, hi


def _inproj_body(h_ref, g_ref, w_ref, wba_ref, o_ref, oba_ref, u_ref):
    @pl.when(pl.program_id(1) == 0)
    def _():
        x = h_ref[...]
        y = x * lax.rsqrt(jnp.mean(x * x, axis=-1, keepdims=True) + NORM_EPS) * g_ref[...]
        ub = y.astype(BF16)
        u_ref[...] = ub
        oba_ref[...] = _dot(ub, wba_ref[...])

    o_ref[...] = _dot(u_ref[...], w_ref[...]).astype(BF16)


def _inproj(h, gain, w_main, w_ba, tm, tn):
    t, d = h.shape
    n = w_main.shape[1]
    nba = w_ba.shape[1]
    return pl.pallas_call(
        _inproj_body,
        grid=(t // tm, n // tn),
        in_specs=[
            pl.BlockSpec((tm, d), lambda i, j: (i, 0)),
            pl.BlockSpec((1, d), lambda i, j: (0, 0)),
            pl.BlockSpec((d, tn), lambda i, j: (0, j)),
            pl.BlockSpec((d, nba), lambda i, j: (0, 0)),
        ],
        out_specs=[
            pl.BlockSpec((tm, tn), lambda i, j: (i, j)),
            pl.BlockSpec((tm, nba), lambda i, j: (i, 0)),
        ],
        out_shape=[jax.ShapeDtypeStruct((t, n), BF16), jax.ShapeDtypeStruct((t, nba), F32)],
        scratch_shapes=[pltpu.VMEM((tm, d), BF16)],
        compiler_params=pltpu.CompilerParams(
            dimension_semantics=("arbitrary", "arbitrary"), vmem_limit_bytes=VMEM_LIMIT),
        name="inproj",
    )(h, gain, w_main, w_ba)


def _pair_dot(lhs, rhs):
    w = rhs[0].shape[1]
    r0, r1 = rhs[0].astype(BF16), rhs[1].astype(BF16)
    zero = jnp.zeros(r0.shape, BF16)
    bd = jnp.concatenate([jnp.concatenate([r0, zero], axis=1), jnp.concatenate([zero, r1], axis=1)], axis=0)
    x = _dot(jnp.concatenate([lhs[0].astype(BF16), lhs[1].astype(BF16)], axis=1), bd)
    return x[:, :w], x[:, w:]


def _pair_neumann_inverse(a_pairs, nilpotency):
    c = a_pairs[0][0].shape[0]
    eye = (lax.broadcasted_iota(I32, (c, c), 0) == lax.broadcasted_iota(I32, (c, c), 1)).astype(F32)
    ns = [[-a for a in pair] for pair in a_pairs]
    ps = [[eye + x for x in n] for n in ns]
    ns = [list(_pair_dot(n, n)) for n in ns]
    steps = int(np.log2(nilpotency)) - 1
    for step in range(steps):
        if step < steps - 1:
            pns = [_pair_dot([jnp.concatenate([p[i], n[i]], axis=0) for i in range(2)], n) for p, n in zip(ps, ns)]
            ps = [[p[i] + pn[i][:c] for i in range(2)] for p, pn in zip(ps, pns)]
            ns = [[pn[i][c:] for i in range(2)] for pn in pns]
        else:
            pns = [_pair_dot(p, n) for p, n in zip(ps, ns)]
            ps = [[p[i] + pn[i] for i in range(2)] for p, pn in zip(ps, pns)]
    return ps


def _chunk_cumsum(x, half):
    rows = x.shape[0]
    r = lax.broadcasted_iota(I32, x.shape, 0)
    top = r < half
    s = 1
    while s < half:
        down = pltpu.roll(x, s, axis=0)
        up = pltpu.roll(x, rows - s, axis=0)
        x = x + jnp.where(top, jnp.where(r >= s, down, 0.0), jnp.where(r < rows - s, up, 0.0))
        s *= 2
    return x


def _dn_body(q_ref, k_ref, v_ref, z_ref, ba_ref, cq_ref, ck_ref, cv_ref, gp_ref, on_ref, y_ref,
             xq, xk, xv, qn, kn, vn, gbf, gbb, us, ws, qds, kdts, qks, cds, of_s, ob_s, s_s, *, seq_len):
    nh = DN_NH
    wd = nh * DN_HEAD_DIM
    c64 = DN_CHUNK
    n_chunks = (seq_len + DN_PAD) // c64
    n_rows = n_chunks * c64
    lead = 8 + DN_PAD
    copy_rows = 48
    n_copy = seq_len // copy_rows

    for x in (xq, xk, xv):
        x[0:lead, :] = jnp.zeros((lead, wd), F32)
        x[lead + seq_len:lead + seq_len + 8, :] = jnp.zeros((8, wd), F32)
    gbf[0:DN_PAD, :] = jnp.zeros((DN_PAD, LANES), F32)
    gbb[0:DN_PAD, :] = jnp.zeros((DN_PAD, LANES), F32)

    lane = lax.broadcasted_iota(I32, (copy_rows, LANES), 1)
    neg_a = -jnp.exp(gp_ref[0, 0:1, :])
    dt_b = gp_ref[0, 1:2, :]

    def copy_step(j, carry):
        src = pl.ds(pl.multiple_of(j * copy_rows, 16), copy_rows)
        dst = pl.ds(pl.multiple_of(lead + j * copy_rows, 8), copy_rows)
        xq[dst, :] = q_ref[0, src, :].astype(F32)
        xk[dst, :] = k_ref[0, src, :].astype(F32)
        xv[dst, :] = v_ref[0, src, :].astype(F32)
        ba = ba_ref[0, src, :]
        beta = _sigmoid(ba)
        sp_in = ba + dt_b
        softplus = jnp.maximum(sp_in, 0.0) + jnp.log(1.0 + jnp.exp(-jnp.abs(sp_in)))
        val = jnp.where(lane < 2 * nh, beta, jnp.where(lane < 4 * nh, neg_a * softplus, 0.0))
        dst_g = pl.ds(pl.multiple_of(DN_PAD + j * copy_rows, 8), copy_rows)
        gbf[dst_g, :] = val
        gbb[dst_g, :] = pltpu.roll(val, LANES - nh, axis=1)
        return carry

    lax.fori_loop(0, n_copy, copy_step, 0)

    row_iota = lax.broadcasted_iota(I32, (c64, wd), 0)

    def conv_step(c, carry):
        r0 = pl.multiple_of(c * c64, c64)
        live = (row_iota + r0) >= DN_PAD
        for x, cw, dst, kind in ((xq, cq_ref, qn, "q"), (xk, ck_ref, kn, "k"), (xv, cv_ref, vn, "v")):
            win = x[pl.ds(r0, c64 + 16), :]
            acc = jnp.zeros((c64, wd), F32)
            for j in range(DN_CONV):
                off = 8 - DN_CONV // 2 + j
                acc = acc + win[off:off + c64, :] * cw[j:j + 1, :]
            y = jnp.where(live, acc * _sigmoid(acc), 0.0)
            if kind == "v":
                dst[pl.ds(r0, c64), :] = y
            else:
                scale = DN_HEAD_DIM ** -0.5 if kind == "q" else 1.0
                for hh in range(nh):
                    yh = y[:, hh * LANES:(hh + 1) * LANES]
                    yn = yh * lax.rsqrt(jnp.sum(yh * yh, axis=-1, keepdims=True) + 1e-6)
                    dst[pl.ds(r0, c64), hh * LANES:(hh + 1) * LANES] = yn * scale
        return carry

    lax.fori_loop(0, n_chunks, conv_step, 0)

    c2 = 2 * c64
    ri = lax.broadcasted_iota(I32, (c2, c2), 0)
    ci = lax.broadcasted_iota(I32, (c2, c2), 1)
    fwd_blk = (ri < c64) & (ci < c64)
    bwd_blk = (ri >= c64) & (ci >= c64)
    incl = (fwd_blk & (ri >= ci)) | (bwd_blk & (ri <= ci))
    strict = (fwd_blk & (ri > ci)) | (bwd_blk & (ri < ci))
    lane_row = lax.broadcasted_iota(I32, (1, c2), 1)
    lane_blk = lax.broadcasted_iota(I32, (c2, c2), 1)

    def chunk_group_step(grp, carry):
        chunks = [grp * DN_CHUNK_UNROLL + j for j in range(DN_CHUNK_UNROLL)]
        r0s = [pl.multiple_of(c * c64, c64) for c in chunks]
        b0s = [pl.multiple_of(c * c2, c2) for c in chunks]
        gb2s, gam2s, gam2_ts, q2s, k2s, v2s, k2_ts, lhs_gs = [], [], [], [], [], [], [], []
        for r0 in r0s:
            gb2 = jnp.concatenate([gbf[pl.ds(r0, c64), :], gbb[pl.ds(r0, c64), :]], axis=0)
            gam2 = _chunk_cumsum(gb2, c64)
            gb2s.append(gb2)
            gam2s.append(gam2)
            gam2_ts.append(gam2.T)
            q2, k2, v2 = [], [], []
            for hh in range(nh):
                hs = slice(hh * LANES, (hh + 1) * LANES)
                q = qn[pl.ds(r0, c64), hs]
                k = kn[pl.ds(r0, c64), hs]
                v = vn[pl.ds(r0, c64), hs]
                q2.append(jnp.concatenate([q, q], axis=0))
                k2.append(jnp.concatenate([k, k], axis=0))
                v2.append(jnp.concatenate([v, v], axis=0))
            q2s.append(q2)
            k2s.append(k2)
            v2s.append(v2)
            k2_ts.append([x.T for x in k2])
            lhs_gs.append([jnp.concatenate([k2[hh], q2[hh]], axis=0) for hh in range(nh)])
        g2s = [_pair_dot(lhs_g, k2_t) for lhs_g, k2_t in zip(lhs_gs, k2_ts)]
        a_pairs, decays, e_gs, gam_rs, gam_lasts, beta_cs = [], [], [], [], [], []
        for j in range(DN_CHUNK_UNROLL):
            a_pair, decay, e_g, gam_r, gam_last, beta_c = [], [], [], [], [], []
            for hh in range(nh):
                gcol = 2 * nh + hh
                beta_c.append(gb2s[j][:, hh:hh + 1])
                gam_c = gam2s[j][:, gcol:gcol + 1]
                gam_r.append(gam2_ts[j][gcol:gcol + 1, :])
                gam_last.append((gam_c[c64 - 1:c64, :], gam_c[c64:c64 + 1, :]))
                decay.append(jnp.exp(jnp.where(incl, gam_c - gam_r[hh], -jnp.inf)))
                a_pair.append(jnp.where(strict, beta_c[hh] * g2s[j][hh][:c2] * decay[hh], 0.0))
                e_g.append(jnp.exp(gam_c))
            a_pairs.append(a_pair)
            decays.append(decay)
            e_gs.append(e_g)
            gam_rs.append(gam_r)
            gam_lasts.append(gam_last)
            beta_cs.append(beta_c)
        t_invs = _pair_neumann_inverse(a_pairs, c64)
        rhss = [[jnp.concatenate([beta_cs[j][hh] * v2s[j][hh], (beta_cs[j][hh] * e_gs[j][hh]) * k2s[j][hh]], axis=1)
                 for hh in range(nh)] for j in range(DN_CHUNK_UNROLL)]
        sols = [_pair_dot(t_inv, rhs) for t_inv, rhs in zip(t_invs, rhss)]
        for j, c in enumerate(chunks):
            b0 = b0s[j]
            for hh in range(nh):
                gl_f, gl_b = gam_lasts[j][hh]
                gl_row = jnp.where(lane_row < c64, gl_f, gl_b)
                us[hh, pl.ds(b0, c2), :] = sols[j][hh][:, :LANES].astype(BF16)
                ws[hh, pl.ds(b0, c2), :] = sols[j][hh][:, LANES:].astype(BF16)
                qds[hh, pl.ds(b0, c2), :] = (q2s[j][hh] * e_gs[j][hh]).astype(BF16)
                qks[hh, pl.ds(b0, c2), :] = (g2s[j][hh][c2:] * decays[j][hh]).astype(BF16)
                kdts[hh, pl.ds(b0, c2), :] = (k2_ts[j][hh] * jnp.exp(gl_row - gam_rs[j][hh])).astype(BF16)
                cds[hh, pl.ds(pl.multiple_of(c * 16, 16), 8), :] = jnp.broadcast_to(jnp.exp(gl_f), (8, LANES))
                cds[hh, pl.ds(pl.multiple_of(c * 16 + 8, 8), 8), :] = jnp.broadcast_to(jnp.exp(gl_b), (8, LANES))
        return carry

    assert n_chunks % DN_CHUNK_UNROLL == 0
    lax.fori_loop(0, n_chunks // DN_CHUNK_UNROLL, chunk_group_step, 0)

    s_s[...] = jnp.zeros(s_s.shape, F32)

    def rec_step(i, carry):
        cs = (i, n_chunks - 1 - i)
        r0s = [pl.multiple_of(c * c64, c64) for c in cs]
        b0s = [pl.multiple_of(c * c2, c2) for c in cs]
        rows = [pl.ds(pl.multiple_of(b0s[d] + d * c64, c64), c64) for d in range(2)]
        keeps = (lane_blk < c64, lane_blk >= c64)
        states = [[s_s[hh * 2 + d] for hh in range(nh)] for d in range(2)]
        wq_s = [_pair_dot([jnp.concatenate([ws[hh, rows[d], :], qds[hh, rows[d], :]], axis=0) for hh in range(nh)],
                          states[d]) for d in range(2)]
        vbs = [[(us[hh, rows[d], :].astype(F32) - wq_s[d][hh][:c64]).astype(BF16) for hh in range(nh)]
               for d in range(2)]
        lhs2 = [[jnp.concatenate([qks[hh, rows[d], :],
                                  jnp.where(keeps[d], kdts[hh, pl.ds(b0s[d], c2), :], jnp.zeros((), BF16))], axis=0)
                 for hh in range(nh)] for d in range(2)]
        r2 = [_pair_dot(lhs2[d], [jnp.concatenate([x, x], axis=0) for x in vbs[d]]) for d in range(2)]
        for d in range(2):
            for hh in range(nh):
                cd = cds[hh, pl.ds(pl.multiple_of(cs[d] * 16 + d * 8, 8), 8), :][0:1, :]
                s_s[hh * 2 + d] = states[d][hh] * cd + r2[d][hh][c64:]
                (of_s if d == 0 else ob_s)[pl.ds(r0s[d], c64), hh * LANES:(hh + 1) * LANES] = (
                    wq_s[d][hh][c64:] + r2[d][hh][:c64])
        return carry

    lax.fori_loop(0, n_chunks, rec_step, 0)

    gain = on_ref[...]

    def emit(o, z):
        outs = []
        for hh in range(nh):
            oh = o[:, hh * LANES:(hh + 1) * LANES]
            outs.append(oh * lax.rsqrt(jnp.mean(oh * oh, axis=-1, keepdims=True) + NORM_EPS) * gain)
        on = jnp.concatenate(outs, axis=1)
        zf = z.astype(F32)
        return (on * (zf * _sigmoid(zf))).astype(BF16)

    n0 = c64 - DN_PAD
    o0 = of_s[DN_PAD:c64, :] + ob_s[DN_PAD:c64, :]
    y_ref[0, 0:n0, :] = emit(o0, z_ref[0, 0:n0, :])

    def out_step(c, carry):
        r0 = pl.multiple_of(c * c64, c64)
        l0 = pl.multiple_of(c * c64 - DN_PAD, 16)
        o = of_s[pl.ds(r0, c64), :] + ob_s[pl.ds(r0, c64), :]
        y_ref[0, pl.ds(l0, c64), :] = emit(o, z_ref[0, pl.ds(l0, c64), :])
        return carry

    lax.fori_loop(1, n_chunks, out_step, 0)


def _deltanet(proj3, ba3, conv_w, gpar, out_norm):
    b, l, _ = proj3.shape
    nh = DN_NH
    wd = nh * DN_HEAD_DIM
    n_rows = l + DN_PAD
    n_chunks = n_rows // DN_CHUNK
    ng = DN_NG
    blk = lambda col0: pl.BlockSpec((1, l, wd), lambda i, g, col0=col0: (i, 0, col0 // nh + g))
    cblk = lambda part: pl.BlockSpec((DN_CONV, wd), lambda i, g, part=part: (0, part * ng + g))
    return pl.pallas_call(
        functools.partial(_dn_body, seq_len=l),
        grid=(b, ng),
        in_specs=[
            blk(COL_DQ), blk(COL_DK), blk(COL_DV), blk(COL_DZ),
            pl.BlockSpec((1, l, LANES), lambda i, g: (i, 0, g)),
            cblk(0), cblk(1), cblk(2),
            pl.BlockSpec((1, 2, LANES), lambda i, g: (g, 0, 0)),
            pl.BlockSpec((1, LANES), lambda i, g: (0, 0)),
        ],
        out_specs=pl.BlockSpec((1, l, wd), lambda i, g: (i, 0, g)),
        out_shape=jax.ShapeDtypeStruct((b, l, DN_WIDTH), BF16),
        scratch_shapes=[
            pltpu.VMEM((n_rows + 16, wd), F32), pltpu.VMEM((n_rows + 16, wd), F32),
            pltpu.VMEM((n_rows + 16, wd), F32),
            pltpu.VMEM((n_rows, wd), F32), pltpu.VMEM((n_rows, wd), F32), pltpu.VMEM((n_rows, wd), F32),
            pltpu.VMEM((n_rows, LANES), F32), pltpu.VMEM((n_rows, LANES), F32),
            pltpu.VMEM((nh, 2 * n_rows, LANES), BF16), pltpu.VMEM((nh, 2 * n_rows, LANES), BF16),
            pltpu.VMEM((nh, 2 * n_rows, LANES), BF16),
            pltpu.VMEM((nh, 2 * n_rows, LANES), BF16),
            pltpu.VMEM((nh, 2 * n_rows, LANES), BF16),
            pltpu.VMEM((nh, n_chunks * 16, LANES), F32),
            pltpu.VMEM((n_rows, wd), F32), pltpu.VMEM((n_rows, wd), F32),
            pltpu.VMEM((DN_NCH, LANES, LANES), F32),
        ],
        compiler_params=pltpu.CompilerParams(
            dimension_semantics=("arbitrary", "arbitrary"), vmem_limit_bytes=VMEM_LIMIT),
        name="deltanet",
    )(proj3, proj3, proj3, proj3, ba3, conv_w, conv_w, conv_w, gpar, out_norm)


def _na_body(q_ref, k_ref, v_ref, tb_ref, o_ref, *, rows):
    scale = NA_HEAD_DIM ** -0.5
    wr = NA_WIN_R
    nk = wr * GRID_W
    qm = q_ref[0, 0:N_META, :]
    km = k_ref[0, 0:N_META, :]
    vm = v_ref[0, 0:N_META, :]

    s = _dot_nt(qm, km) * scale
    p = jnp.exp(s - jnp.max(s, axis=-1, keepdims=True))
    o = _dot(p.astype(BF16), vm) / jnp.sum(p, axis=-1, keepdims=True)
    o_ref[0, 0:N_META, :] = o.astype(BF16)

    def row_step(r, carry):
        rs = jnp.clip(r - wr // 2, 0, rows - wr)
        i0 = rs - r + NA_WIN_R - 1
        q = q_ref[0, pl.ds(pl.multiple_of(N_META + r * GRID_W, 16), GRID_W), :]
        kw = k_ref[0, pl.ds(pl.multiple_of(N_META + rs * GRID_W, 16), nk), :]
        vw = v_ref[0, pl.ds(pl.multiple_of(N_META + rs * GRID_W, 16), nk), :]
        s_win = _dot_nt(q, kw) * scale + tb_ref[0, i0]
        s_meta = _dot_nt(q, km) * scale
        m = jnp.maximum(jnp.max(s_win, axis=-1, keepdims=True), jnp.max(s_meta, axis=-1, keepdims=True))
        p_win = jnp.exp(s_win - m)
        p_meta = jnp.exp(s_meta - m)
        den = jnp.sum(p_win, axis=-1, keepdims=True) + jnp.sum(p_meta, axis=-1, keepdims=True)
        o = (_dot(p_win.astype(BF16), vw) + _dot(p_meta.astype(BF16), vm)) / den
        o_ref[0, pl.ds(pl.multiple_of(N_META + r * GRID_W, 16), GRID_W), :] = o.astype(BF16)
        return carry

    lax.fori_loop(0, rows, row_step, 0)


def _na_bias_table(rpb):
    wr, wc = NA_WIN_R, NA_WIN_C
    qc = np.arange(GRID_W)[:, None]
    kc = np.arange(GRID_W)[None, :]
    q_start = np.clip(qc - wc // 2, 0, GRID_W - wc)
    valid = (kc - q_start >= 0) & (kc - q_start < wc)
    dc = np.clip(kc - qc + wc - 1, 0, 2 * wc - 2)
    i0 = np.arange(wr)[:, None]
    jj = np.arange(wr)[None, :]
    dr = i0 + jj
    tb = rpb.astype(F32)[:, dr][:, :, :, dc]
    tb = jnp.where(valid[None, None, None], tb, -jnp.inf)
    tb = tb.transpose(0, 1, 3, 2, 4)
    return tb.reshape(NA_HEADS, wr, GRID_W, wr * GRID_W)


def _natten(proj3, tb):
    b, l, _ = proj3.shape
    rows = (l - N_META) // GRID_W
    assert rows >= NA_WIN_R
    blk = lambda col0: pl.BlockSpec((1, l, LANES), lambda h, i, col0=col0: (i, 0, col0 + h))
    return pl.pallas_call(
        functools.partial(_na_body, rows=rows),
        grid=(NA_HEADS, b),
        in_specs=[
            blk(COL_NQ), blk(COL_NK), blk(COL_NV),
            pl.BlockSpec((1, NA_WIN_R, GRID_W, NA_WIN_R * GRID_W), lambda h, i: (h, 0, 0, 0)),
        ],
        out_specs=pl.BlockSpec((1, l, LANES), lambda h, i: (i, 0, h)),
        out_shape=jax.ShapeDtypeStruct((b, l, NA_WIDTH), BF16),
        compiler_params=pltpu.CompilerParams(
            dimension_semantics=("arbitrary", "arbitrary"), vmem_limit_bytes=VMEM_LIMIT),
        name="natten",
    )(proj3, proj3, proj3, tb)


def _mix_body(ydn_ref, yna_ref, ga_ref, gb_ref, h_ref, wdn_ref, wna_ref, wo_ref, gn_ref, wr_ref, br_ref,
              h2_ref, u2_ref, gate_ref, idx_ref):
    a = _dot(ydn_ref[...], wdn_ref[...])
    b = _dot(yna_ref[...], wna_ref[...])
    mix = _sigmoid(ga_ref[...].astype(F32)) * a + _sigmoid(gb_ref[...].astype(F32)) * b
    h2 = h_ref[...] + _dot(mix.astype(BF16), wo_ref[...])
    h2_ref[...] = h2
    u2 = h2 * lax.rsqrt(jnp.mean(h2 * h2, axis=-1, keepdims=True) + NORM_EPS) * gn_ref[...]
    _store_token_tiles(u2_ref, _pack_halves(u2), u2.shape[0])
    logits = _dot(u2, wr_ref[...], precision=lax.Precision.HIGHEST) + br_ref[...]
    lane = lax.broadcasted_iota(I32, logits.shape, 1)
    logits = jnp.where(lane < N_EXPERTS, logits, -jnp.inf)
    vals, idxs = [], []
    for _ in range(TOP_K):
        m = jnp.max(logits, axis=-1, keepdims=True)
        sel = jnp.min(jnp.where(logits == m, lane, LANES), axis=-1, keepdims=True)
        vals.append(m)
        idxs.append(sel)
        logits = jnp.where(lane == sel, -jnp.inf, logits)
    es = [jnp.exp(v - vals[0]) for v in vals]
    den = es[0] + es[1] + es[2] + es[3]
    gates = jnp.zeros(logits.shape, F32)
    idx = jnp.zeros(logits.shape, I32)
    for k in range(TOP_K):
        gates = jnp.where(lane == k, es[k] / den, gates)
        idx = jnp.where(lane == k, idxs[k], idx)
    gate_ref[...] = gates
    idx_ref[...] = idx


def _mix(y_dn, y_na, proj, h, w_dn, w_na, w_o, gain, w_r, b_r, tm):
    t, d = h.shape
    row = lambda w: pl.BlockSpec((tm, w), lambda i: (i, 0))
    full = lambda a: pl.BlockSpec(a.shape, lambda i: (0,) * a.ndim)
    return pl.pallas_call(
        _mix_body,
        grid=(t // tm,),
        in_specs=[
            row(DN_WIDTH), row(NA_WIDTH),
            pl.BlockSpec((tm, d), lambda i: (i, COL_GA * LANES // d)),
            pl.BlockSpec((tm, d), lambda i: (i, COL_GB * LANES // d)),
            row(d), full(w_dn), full(w_na), full(w_o), full(gain), full(w_r), full(b_r),
        ],
        out_specs=[row(d), pl.BlockSpec((tm * SUBLANES, LANES), lambda i: (i, 0)), row(LANES), row(LANES)],
        out_shape=[
            jax.ShapeDtypeStruct((t, d), F32), jax.ShapeDtypeStruct((t * SUBLANES, LANES), U32),
            jax.ShapeDtypeStruct((t, LANES), F32), jax.ShapeDtypeStruct((t, LANES), I32),
        ],
        compiler_params=pltpu.CompilerParams(
            dimension_semantics=("arbitrary",), vmem_limit_bytes=VMEM_LIMIT),
        name="mix",
    )(y_dn, y_na, proj, proj, h, w_dn, w_na, w_o, gain, w_r, b_r)


def _moe_body(blk_exp, n_valid, idx_hbm, u_hbm, wg_ref, wl_ref, bg_ref, bl_ref, wo_ref, bo_ref, y_hbm,
              idx_s, xbuf, xs, acc, stage, isem, gsem, ssem, *, n_blocks):
    i = pl.program_id(0)
    f = pl.program_id(1)
    slot = i % 2
    nxt = (i + 1) % 2
    tm = MOE_TM

    def idx_copy(blk, sl):
        return pltpu.make_async_copy(idx_hbm.at[pl.ds(pl.multiple_of(blk * (2 * tm), 2 * tm), 2 * tm)],
                                     idx_s.at[pl.ds(pl.multiple_of(sl * (2 * tm), 2 * tm), 2 * tm)], isem.at[sl])

    def tile(ref, row0):
        return ref.at[pl.ds(pl.multiple_of(row0, SUBLANES), SUBLANES), :]

    def gather_rows(sl):
        def body(r, carry):
            src = idx_s[sl * (2 * tm) + r]
            pltpu.make_async_copy(tile(u_hbm, src), tile(xbuf.at[sl], r * SUBLANES), gsem.at[sl]).start()
            return carry
        lax.fori_loop(0, tm, body, 0, unroll=DMA_ISSUE_UNROLL)

    def gather_wait(sl):
        pltpu.make_async_copy(xbuf.at[sl], xbuf.at[sl], gsem.at[sl]).wait()

    def scatter_rows(sl):
        def body(r, carry):
            dst = idx_s[sl * (2 * tm) + tm + r]
            pltpu.make_async_copy(tile(stage, r * SUBLANES), tile(y_hbm, dst), ssem).start()
            return carry
        lax.fori_loop(0, tm, body, 0, unroll=DMA_ISSUE_UNROLL)

    def scatter_wait():
        pltpu.make_async_copy(stage, stage, ssem).wait()

    live = n_valid[i] > 0

    @pl.when((i == 0) & (f == 0))
    def _():
        stage[...] = jnp.zeros(stage.shape, U32)
        dump = pltpu.make_async_copy(stage, y_hbm.at[pl.ds(y_hbm.shape[0] - tm * SUBLANES, tm * SUBLANES), :], ssem)
        dump.start()
        dump.wait()
        idx_copy(0, 0).start()
        idx_copy(0, 0).wait()

        @pl.when(live)
        def _():
            gather_rows(0)

    @pl.when((f == 0) & live)
    def _():
        gather_wait(slot)
        for s in range(SUBLANES):
            lo, hi = _unpack_halves(xbuf[slot, pl.ds(s, tm, stride=SUBLANES), :])
            xs[:, s * LANES:(s + 1) * LANES] = lo.astype(BF16)
            xs[:, HALF + s * LANES:HALF + (s + 1) * LANES] = hi.astype(BF16)

    @pl.when((f == 0) & (i + 1 < n_blocks))
    def _():
        idx_copy(i + 1, nxt).start()

    @pl.when((f == 1) & (i + 1 < n_blocks))
    def _():
        idx_copy(i + 1, nxt).wait()

        @pl.when(n_valid[jnp.minimum(i + 1, n_blocks - 1)] > 0)
        def _():
            gather_rows(nxt)

    @pl.when(live)
    def _():
        x = xs[...]
        hg = _dot(x, wg_ref[0]) + bg_ref[0]
        hl = _dot(x, wl_ref[0]) + bl_ref[0]
        glu = jnp.minimum(hg, SWIGLU_LIMIT)
        lin = jnp.clip(hl, -SWIGLU_LIMIT, SWIGLU_LIMIT)
        act = glu * _sigmoid(SWIGLU_ALPHA * glu) * (lin + 1.0)
        part = _dot(act.astype(BF16), wo_ref[0])

        @pl.when(f == 0)
        def _():
            acc[...] = part

        @pl.when(f > 0)
        def _():
            acc[...] += part

    @pl.when(f == MOE_NF - 1)
    def _():
        @pl.when((i > 0) & (n_valid[jnp.maximum(i - 1, 0)] > 0))
        def _():
            scatter_wait()

        @pl.when(live)
        def _():
            _store_token_tiles(stage, _pack_halves(acc[...] + bo_ref[0]), tm)
            scatter_rows(slot)

        @pl.when((i == n_blocks - 1) & live)
        def _():
            scatter_wait()


def _moe(blk_exp, n_valid, idx_blocks, u2p, w_in, b_in, w_out, b_out, n_rows_out):
    n_blocks = blk_exp.shape[0]
    d = D_MODEL
    nf = MOE_NF

    def wmap(off):
        def index_map(i, f, be, nv):
            fe = jnp.where(nv[i] > 0, f, nf - 1)
            return (be[i], 0, fe + off)
        return index_map

    def womap(i, f, be, nv):
        return (be[i], jnp.where(nv[i] > 0, f, nf - 1), 0)

    grid_spec = pltpu.PrefetchScalarGridSpec(
        num_scalar_prefetch=2,
        grid=(n_blocks, nf),
        in_specs=[
            pl.BlockSpec(memory_space=pl.ANY),
            pl.BlockSpec(memory_space=pl.ANY),
            pl.BlockSpec((1, d, MOE_TF), wmap(0)),
            pl.BlockSpec((1, d, MOE_TF), wmap(nf)),
            pl.BlockSpec((1, 1, MOE_TF), wmap(0)),
            pl.BlockSpec((1, 1, MOE_TF), wmap(nf)),
            pl.BlockSpec((1, MOE_TF, d), womap),
            pl.BlockSpec((1, 1, d), lambda i, f, be, nv: (be[i], 0, 0)),
        ],
        out_specs=pl.BlockSpec(memory_space=pl.ANY),
        scratch_shapes=[
            pltpu.SMEM((2 * 2 * MOE_TM,), I32),
            pltpu.VMEM((2, MOE_TM * SUBLANES, LANES), U32),
            pltpu.VMEM((MOE_TM, d), BF16),
            pltpu.VMEM((MOE_TM, d), F32),
            pltpu.VMEM((MOE_TM * SUBLANES, LANES), U32),
            pltpu.SemaphoreType.DMA((2,)),
            pltpu.SemaphoreType.DMA((2,)),
            pltpu.SemaphoreType.DMA,
        ],
    )
    return pl.pallas_call(
        functools.partial(_moe_body, n_blocks=n_blocks),
        grid_spec=grid_spec,
        out_shape=jax.ShapeDtypeStruct((n_rows_out * SUBLANES, LANES), U32),
        compiler_params=pltpu.CompilerParams(
            dimension_semantics=("arbitrary", "arbitrary"), vmem_limit_bytes=VMEM_LIMIT),
        name="moe",
    )(blk_exp, n_valid, idx_blocks, u2p, w_in, w_in, b_in, b_in, w_out, b_out)


def _routing_metadata(top_idx, n_blocks):
    tm = MOE_TM
    e = top_idx[:, :TOP_K].reshape(-1)
    n_asg = e.shape[0]
    order = jnp.argsort(e, stable=True).astype(I32)
    counts = jnp.sum((e[:, None] == jnp.arange(N_EXPERTS, dtype=I32)[None, :]).astype(I32), axis=0)
    cstart = jnp.cumsum(counts) - counts
    nblk = (counts + tm - 1) // tm
    bend = jnp.cumsum(nblk)
    bstart = bend - nblk
    blk = jnp.arange(n_blocks, dtype=I32)
    blk_exp = jnp.minimum(jnp.searchsorted(bend, blk, side="right"), N_EXPERTS - 1).astype(I32)
    first = (blk - bstart[blk_exp]) * tm
    n_valid = jnp.where(blk < bend[-1], jnp.clip(counts[blk_exp] - first, 0, tm), 0).astype(I32)
    r = jnp.arange(tm, dtype=I32)[None, :]
    valid = r < n_valid[:, None]
    asg = order[jnp.where(valid, (cstart[blk_exp] + first)[:, None] + r, 0)]
    src_row = (asg // TOP_K) * SUBLANES
    dst_row = jnp.where(valid, asg, n_asg + r) * SUBLANES
    idx_blocks = jnp.concatenate([src_row, dst_row], axis=1).reshape(-1)
    return blk_exp, n_valid, idx_blocks


def _final_body(h_ref, y_ref, gate_ref, gn_ref, o_ref):
    h = h_ref[...]
    gates = gate_ref[...]
    lo_acc = jnp.zeros((h.shape[0], HALF), F32)
    hi_acc = jnp.zeros((h.shape[0], HALF), F32)
    for k in range(TOP_K):
        lo, hi = _unpack_halves(_load_token_tiles(y_ref, h.shape[0], k * SUBLANES, TOP_K * SUBLANES))
        g = gates[:, k:k + 1]
        lo_acc = lo_acc + g * lo
        hi_acc = hi_acc + g * hi
    y = h + jnp.concatenate([lo_acc, hi_acc], axis=1)
    o_ref[...] = y * lax.rsqrt(jnp.mean(y * y, axis=-1, keepdims=True) + NORM_EPS) * gn_ref[...]


def _final(h2, y4, gates, gain, tm):
    t, d = h2.shape
    return pl.pallas_call(
        _final_body,
        grid=(t // tm,),
        in_specs=[
            pl.BlockSpec((tm, d), lambda i: (i, 0)),
            pl.BlockSpec((tm * TOP_K * SUBLANES, LANES), lambda i: (i, 0)),
            pl.BlockSpec((tm, LANES), lambda i: (i, 0)),
            pl.BlockSpec((1, d), lambda i: (0, 0)),
        ],
        out_specs=pl.BlockSpec((tm, d), lambda i: (i, 0)),
        out_shape=jax.ShapeDtypeStruct((t, d), F32),
        compiler_params=pltpu.CompilerParams(
            dimension_semantics=("arbitrary",), vmem_limit_bytes=VMEM_LIMIT),
        name="final",
    )(h2, y4, gates, gain)


def _split_w_in(w_in):
    n_dn = 4 * DN_WIDTH
    n_na = n_dn + 4 * DN_HEADS
    w_main = jnp.concatenate(
        [w_in[:, :n_dn], w_in[:, n_na + 3 * NA_WIDTH:], w_in[:, n_na:n_na + 3 * NA_WIDTH]], axis=1).astype(BF16)
    wb = w_in[:, n_dn:n_dn + 2 * DN_HEADS]
    wa = w_in[:, n_dn + 2 * DN_HEADS:n_dn + 4 * DN_HEADS]
    return w_main, _group_lanes(wb, wa).astype(BF16)


def _group_lanes(b_part, a_part):
    nh, ng = DN_NH, DN_NG
    lead = b_part.shape[:-1]
    bp = b_part.reshape(lead + (2, ng, nh))
    ap = a_part.reshape(lead + (2, ng, nh))
    perm = tuple(range(len(lead))) + (len(lead) + 1, len(lead), len(lead) + 2)
    bp = bp.transpose(perm).reshape(lead + (ng, 2 * nh))
    ap = ap.transpose(perm).reshape(lead + (ng, 2 * nh))
    both = jnp.concatenate([bp, ap], axis=-1)
    pad = [(0, 0)] * (both.ndim - 1) + [(0, LANES - 4 * nh)]
    return jnp.pad(both, pad).reshape(lead + (ng * LANES,))


def kernel(x, meta_tokens, norm_mix, w_in, dn_conv, dn_a_log, dn_dt_bias, dn_out_norm, na_rpb,
           w_branch_dn, w_branch_na, w_out, norm_ffn, w_router, b_router, w_exp_in, b_exp_in,
           w_exp_out, b_exp_out, norm_final):
    bsz, seq, d = x.shape
    l = seq + N_META
    t = bsz * l
    meta = jnp.broadcast_to(meta_tokens[None].astype(x.dtype), (bsz, N_META, d))
    h = jnp.concatenate([meta, x], axis=1).reshape(t, d)
    depth = w_in.shape[0]
    assert depth == 1, "the final kernel fuses the last residual add with the final RMSNorm"
    tm_proj = 768 if t % 768 == 0 else 8 * (l // 8)
    tm_mix = 384 if t % 384 == 0 else 8 * (l // 8)
    n_blocks = t * TOP_K // MOE_TM + N_EXPERTS
    for layer in range(depth):
        w_main, w_ba = _split_w_in(w_in[layer])
        proj, ba = _inproj(h, norm_mix[layer][None, :], w_main, w_ba, tm_proj, 1024)
        proj3 = proj.reshape(bsz, l, N_MAIN_COLS)
        ba3 = ba.reshape(bsz, l, DN_NG * LANES)
        gpar = jnp.stack([_group_lanes(jnp.zeros_like(dn_a_log[layer]).reshape(-1), dn_a_log[layer].reshape(-1)),
                          _group_lanes(jnp.zeros_like(dn_dt_bias[layer]).reshape(-1),
                                       dn_dt_bias[layer].reshape(-1))], axis=0)
        gpar = gpar.reshape(2, DN_NG, LANES).transpose(1, 0, 2)
        y_dn = _deltanet(proj3, ba3, dn_conv[layer], gpar, dn_out_norm[layer][None, :])
        y_na = _natten(proj3, _na_bias_table(na_rpb[layer]))
        w_r = jnp.pad(w_router[layer], ((0, 0), (0, LANES - N_EXPERTS)))
        b_r = jnp.pad(b_router[layer], (0, LANES - N_EXPERTS))[None, :]
        h2, u2p, gates, top_idx = _mix(
            y_dn.reshape(t, DN_WIDTH), y_na.reshape(t, NA_WIDTH), proj, h,
            w_branch_dn[layer].astype(BF16), w_branch_na[layer].astype(BF16), w_out[layer].astype(BF16),
            norm_ffn[layer][None, :], w_r, b_r, tm_mix)
        blk_exp, n_valid, idx_blocks = _routing_metadata(top_idx, n_blocks)
        y4 = _moe(blk_exp, n_valid, idx_blocks, u2p,
                  w_exp_in[layer].astype(BF16), b_exp_in[layer][:, None, :],
                  w_exp_out[layer].astype(BF16), b_exp_out[layer][:, None, :], t * TOP_K + MOE_TM)
        h = _final(h2, y4, gates, norm_final[None, :], tm_mix)
    return h.reshape(bsz, l, d)[:, N_META:]
```

```python
import functools

import jax
import jax.numpy as jnp
import numpy as np
from jax import lax
from jax.experimental import pallas as pl
from jax.experimental.pallas import tpu as pltpu

F32 = jnp.float32
BF16 = jnp.bfloat16
I32 = jnp.int32
U32 = jnp.uint32

D_MODEL = 2048
N_META = 16
GRID_W = 64
DN_HEADS = 8
DN_HEAD_DIM = 128
DN_WIDTH = 1024
DN_CONV = 5
DN_CHUNK = 64
NA_HEADS = 8
NA_HEAD_DIM = 128
NA_WIDTH = 1024
NA_WIN_R = 8
NA_WIN_C = 16
N_EXPERTS = 32
TOP_K = 4
D_EXPERT = 2048
SWIGLU_ALPHA = 1.702
SWIGLU_LIMIT = 7.0
NORM_EPS = 1e-6

LANES = 128
VMEM_LIMIT = 56 * 1024 * 1024

COL_DQ, COL_DK, COL_DV, COL_DZ = 0, 8, 16, 24
COL_GA, COL_GB = 32, 48
COL_NQ, COL_NK, COL_NV = 64, 72, 80
N_MAIN_COLS = 88 * LANES

DN_NH = 2
DN_NG = DN_HEADS // DN_NH
DN_NCH = 2 * DN_NH
DN_PAD = (-N_META) % DN_CHUNK
DN_CHUNK_UNROLL = 3

DMA_ISSUE_UNROLL = 8

NA_ROW_GROUP = 4
NA_KEYS = 640

MOE_TM = 1024
MOE_TF = 512
MOE_NF = D_EXPERT // MOE_TF
HALF = D_MODEL // 2
SUBLANES = 8
assert HALF == SUBLANES * LANES


def _store_token_tiles(ref, packed, n_tok, base=0, group=SUBLANES):
    for s in range(SUBLANES):
        ref[pl.ds(base + s, n_tok, stride=group), :] = packed[:, s * LANES:(s + 1) * LANES]


def _load_token_tiles(ref, n_tok, base=0, group=SUBLANES):
    return jnp.concatenate([ref[pl.ds(base + s, n_tok, stride=group), :] for s in range(SUBLANES)], axis=1)


def _dot(a, b, **kw):
    return jnp.dot(a, b, preferred_element_type=F32, **kw)


def _dot_nt(a, b, **kw):
    return lax.dot_general(a, b, (((1,), (1,)), ((), ())), preferred_element_type=F32, **kw)


def _dot_tn(a, b, **kw):
    return lax.dot_general(a, b, (((0,), (0,)), ((), ())), preferred_element_type=F32, **kw)


def _sigmoid(x):
    return 1.0 / (1.0 + jnp.exp(-x))


def _pack_halves(x):
    n = x.shape[1] // 2
    lo = pltpu.bitcast(x[:, :n].astype(BF16).astype(F32), U32)
    hi = pltpu.bitcast(x[:, n:].astype(BF16).astype(F32), U32)
    return (hi & jnp.uint32(0xFFFF0000)) | (lo >> 16)


def _unpack_halves(p):
    lo = pltpu.bitcast(p << 16, F32)
    hi = pltpu.bitcast(p & jnp.uint32(0xFFFF0000), F32)
    return lo, hi


def _inproj_body(h_ref, g_ref, w_ref, wba_ref, o_ref, oba_ref, u_ref):
    @pl.when(pl.program_id(1) == 0)
    def _():
        x = h_ref[...]
        y = x * lax.rsqrt(jnp.mean(x * x, axis=-1, keepdims=True) + NORM_EPS) * g_ref[...]
        ub = y.astype(BF16)
        u_ref[...] = ub
        oba_ref[...] = _dot(ub, wba_ref[...])

    o_ref[...] = _dot(u_ref[...], w_ref[...]).astype(BF16)


def _inproj(h, gain, w_main, w_ba, tm, tn):
    t, d = h.shape
    n = w_main.shape[1]
    nba = w_ba.shape[1]
    return pl.pallas_call(
        _inproj_body,
        grid=(t // tm, n // tn),
        in_specs=[
            pl.BlockSpec((tm, d), lambda i, j: (i, 0)),
            pl.BlockSpec((1, d), lambda i, j: (0, 0)),
            pl.BlockSpec((d, tn), lambda i, j: (0, j)),
            pl.BlockSpec((d, nba), lambda i, j: (0, 0)),
        ],
        out_specs=[
            pl.BlockSpec((tm, tn), lambda i, j: (i, j)),
            pl.BlockSpec((tm, nba), lambda i, j: (i, 0)),
        ],
        out_shape=[jax.ShapeDtypeStruct((t, n), BF16), jax.ShapeDtypeStruct((t, nba), F32)],
        scratch_shapes=[pltpu.VMEM((tm, d), BF16)],
        compiler_params=pltpu.CompilerParams(
            dimension_semantics=("arbitrary", "arbitrary"), vmem_limit_bytes=VMEM_LIMIT),
        name="inproj",
    )(h, gain, w_main, w_ba)


def _pair_dot(lhs, rhs):
    w = rhs[0].shape[1]
    r0, r1 = rhs[0].astype(BF16), rhs[1].astype(BF16)
    zero = jnp.zeros(r0.shape, BF16)
    bd = jnp.concatenate([jnp.concatenate([r0, zero], axis=1), jnp.concatenate([zero, r1], axis=1)], axis=0)
    x = _dot(jnp.concatenate([lhs[0].astype(BF16), lhs[1].astype(BF16)], axis=1), bd)
    return x[:, :w], x[:, w:]


def _pair_neumann_inverse(a_pairs, nilpotency):
    c = a_pairs[0][0].shape[0]
    eye = (lax.broadcasted_iota(I32, (c, c), 0) == lax.broadcasted_iota(I32, (c, c), 1)).astype(F32)
    ns = [[-a for a in pair] for pair in a_pairs]
    ps = [[eye + x for x in n] for n in ns]
    ns = [list(_pair_dot(n, n)) for n in ns]
    steps = int(np.log2(nilpotency)) - 1
    for step in range(steps):
        if step < steps - 1:
            pns = [_pair_dot([jnp.concatenate([p[i], n[i]], axis=0) for i in range(2)], n) for p, n in zip(ps, ns)]
            ps = [[p[i] + pn[i][:c] for i in range(2)] for p, pn in zip(ps, pns)]
            ns = [[pn[i][c:] for i in range(2)] for pn in pns]
        else:
            pns = [_pair_dot(p, n) for p, n in zip(ps, ns)]
            ps = [[p[i] + pn[i] for i in range(2)] for p, pn in zip(ps, pns)]
    return ps


def _chunk_cumsum(x, half):
    rows = x.shape[0]
    r = lax.broadcasted_iota(I32, x.shape, 0)
    top = r < half
    s = 1
    while s < half:
        down = pltpu.roll(x, s, axis=0)
        up = pltpu.roll(x, rows - s, axis=0)
        x = x + jnp.where(top, jnp.where(r >= s, down, 0.0), jnp.where(r < rows - s, up, 0.0))
        s *= 2
    return x


def _dn_body(q_ref, k_ref, v_ref, z_ref, ba_ref, cq_ref, ck_ref, cv_ref, gp_ref, on_ref, y_ref,
             xq, xk, xv, qn, kn, vn, gbf, gbb, us, ws, qds, kdts, qks, cds, of_s, ob_s, s_s, *, seq_len):
    nh = DN_NH
    wd = nh * DN_HEAD_DIM
    c64 = DN_CHUNK
    n_chunks = (seq_len + DN_PAD) // c64
    n_rows = n_chunks * c64
    lead = 8 + DN_PAD
    copy_rows = 48
    n_copy = seq_len // copy_rows

    for x in (xq, xk, xv):
        x[0:lead, :] = jnp.zeros((lead, wd), F32)
        x[lead + seq_len:lead + seq_len + 8, :] = jnp.zeros((8, wd), F32)
    gbf[0:DN_PAD, :] = jnp.zeros((DN_PAD, LANES), F32)
    gbb[0:DN_PAD, :] = jnp.zeros((DN_PAD, LANES), F32)

    lane = lax.broadcasted_iota(I32, (copy_rows, LANES), 1)
    neg_a = -jnp.exp(gp_ref[0, 0:1, :])
    dt_b = gp_ref[0, 1:2, :]

    def copy_step(j, carry):
        src = pl.ds(pl.multiple_of(j * copy_rows, 16), copy_rows)
        dst = pl.ds(pl.multiple_of(lead + j * copy_rows, 8), copy_rows)
        xq[dst, :] = q_ref[0, src, :].astype(F32)
        xk[dst, :] = k_ref[0, src, :].astype(F32)
        xv[dst, :] = v_ref[0, src, :].astype(F32)
        ba = ba_ref[0, src, :]
        beta = _sigmoid(ba)
        sp_in = ba + dt_b
        softplus = jnp.maximum(sp_in, 0.0) + jnp.log(1.0 + jnp.exp(-jnp.abs(sp_in)))
        val = jnp.where(lane < 2 * nh, beta, jnp.where(lane < 4 * nh, neg_a * softplus, 0.0))
        dst_g = pl.ds(pl.multiple_of(DN_PAD + j * copy_rows, 8), copy_rows)
        gbf[dst_g, :] = val
        gbb[dst_g, :] = pltpu.roll(val, LANES - nh, axis=1)
        return carry

    lax.fori_loop(0, n_copy, copy_step, 0)

    row_iota = lax.broadcasted_iota(I32, (c64, wd), 0)

    def conv_step(c, carry):
        r0 = pl.multiple_of(c * c64, c64)
        live = (row_iota + r0) >= DN_PAD
        for x, cw, dst, kind in ((xq, cq_ref, qn, "q"), (xk, ck_ref, kn, "k"), (xv, cv_ref, vn, "v")):
            win = x[pl.ds(r0, c64 + 16), :]
            acc = jnp.zeros((c64, wd), F32)
            for j in range(DN_CONV):
                off = 8 - DN_CONV // 2 + j
                acc = acc + win[off:off + c64, :] * cw[j:j + 1, :]
            y = jnp.where(live, acc * _sigmoid(acc), 0.0)
            if kind == "v":
                dst[pl.ds(r0, c64), :] = y
            else:
                scale = DN_HEAD_DIM ** -0.5 if kind == "q" else 1.0
                for hh in range(nh):
                    yh = y[:, hh * LANES:(hh + 1) * LANES]
                    yn = yh * lax.rsqrt(jnp.sum(yh * yh, axis=-1, keepdims=True) + 1e-6)
                    dst[pl.ds(r0, c64), hh * LANES:(hh + 1) * LANES] = yn * scale
        return carry

    lax.fori_loop(0, n_chunks, conv_step, 0)

    c2 = 2 * c64
    ri = lax.broadcasted_iota(I32, (c2, c2), 0)
    ci = lax.broadcasted_iota(I32, (c2, c2), 1)
    fwd_blk = (ri < c64) & (ci < c64)
    bwd_blk = (ri >= c64) & (ci >= c64)
    incl = (fwd_blk & (ri >= ci)) | (bwd_blk & (ri <= ci))
    strict = (fwd_blk & (ri > ci)) | (bwd_blk & (ri < ci))
    lane_row = lax.broadcasted_iota(I32, (1, c2), 1)
    lane_blk = lax.broadcasted_iota(I32, (c2, c2), 1)

    def chunk_group_step(grp, carry):
        chunks = [grp * DN_CHUNK_UNROLL + j for j in range(DN_CHUNK_UNROLL)]
        r0s = [pl.multiple_of(c * c64, c64) for c in chunks]
        b0s = [pl.multiple_of(c * c2, c2) for c in chunks]
        gb2s, gam2s, gam2_ts, q2s, k2s, v2s, k2_ts, lhs_gs = [], [], [], [], [], [], [], []
        for r0 in r0s:
            gb2 = jnp.concatenate([gbf[pl.ds(r0, c64), :], gbb[pl.ds(r0, c64), :]], axis=0)
            gam2 = _chunk_cumsum(gb2, c64)
            gb2s.append(gb2)
            gam2s.append(gam2)
            gam2_ts.append(gam2.T)
            q2, k2, v2 = [], [], []
            for hh in range(nh):
                hs = slice(hh * LANES, (hh + 1) * LANES)
                q = qn[pl.ds(r0, c64), hs]
                k = kn[pl.ds(r0, c64), hs]
                v = vn[pl.ds(r0, c64), hs]
                q2.append(jnp.concatenate([q, q], axis=0))
                k2.append(jnp.concatenate([k, k], axis=0))
                v2.append(jnp.concatenate([v, v], axis=0))
            q2s.append(q2)
            k2s.append(k2)
            v2s.append(v2)
            k2_ts.append([x.T for x in k2])
            lhs_gs.append([jnp.concatenate([k2[hh], q2[hh]], axis=0) for hh in range(nh)])
        g2s = [_pair_dot(lhs_g, k2_t) for lhs_g, k2_t in zip(lhs_gs, k2_ts)]
        a_pairs, decays, e_gs, gam_rs, gam_lasts, beta_cs = [], [], [], [], [], []
        for j in range(DN_CHUNK_UNROLL):
            a_pair, decay, e_g, gam_r, gam_last, beta_c = [], [], [], [], [], []
            for hh in range(nh):
                gcol = 2 * nh + hh
                beta_c.append(gb2s[j][:, hh:hh + 1])
                gam_c = gam2s[j][:, gcol:gcol + 1]
                gam_r.append(gam2_ts[j][gcol:gcol + 1, :])
                gam_last.append((gam_c[c64 - 1:c64, :], gam_c[c64:c64 + 1, :]))
                decay.append(jnp.exp(jnp.where(incl, gam_c - gam_r[hh], -jnp.inf)))
                a_pair.append(jnp.where(strict, beta_c[hh] * g2s[j][hh][:c2] * decay[hh], 0.0))
                e_g.append(jnp.exp(gam_c))
            a_pairs.append(a_pair)
            decays.append(decay)
            e_gs.append(e_g)
            gam_rs.append(gam_r)
            gam_lasts.append(gam_last)
            beta_cs.append(beta_c)
        t_invs = _pair_neumann_inverse(a_pairs, c64)
        rhss = [[jnp.concatenate([beta_cs[j][hh] * v2s[j][hh], (beta_cs[j][hh] * e_gs[j][hh]) * k2s[j][hh]], axis=1)
                 for hh in range(nh)] for j in range(DN_CHUNK_UNROLL)]
        sols = [_pair_dot(t_inv, rhs) for t_inv, rhs in zip(t_invs, rhss)]
        for j, c in enumerate(chunks):
            b0 = b0s[j]
            for hh in range(nh):
                gl_f, gl_b = gam_lasts[j][hh]
                gl_row = jnp.where(lane_row < c64, gl_f, gl_b)
                us[hh, pl.ds(b0, c2), :] = sols[j][hh][:, :LANES].astype(BF16)
                ws[hh, pl.ds(b0, c2), :] = sols[j][hh][:, LANES:].astype(BF16)
                qds[hh, pl.ds(b0, c2), :] = (q2s[j][hh] * e_gs[j][hh]).astype(BF16)
                qks[hh, pl.ds(b0, c2), :] = (g2s[j][hh][c2:] * decays[j][hh]).astype(BF16)
                kdts[hh, pl.ds(b0, c2), :] = (k2_ts[j][hh] * jnp.exp(gl_row - gam_rs[j][hh])).astype(BF16)
                cds[hh, pl.ds(pl.multiple_of(c * 16, 16), 8), :] = jnp.broadcast_to(jnp.exp(gl_f), (8, LANES))
                cds[hh, pl.ds(pl.multiple_of(c * 16 + 8, 8), 8), :] = jnp.broadcast_to(jnp.exp(gl_b), (8, LANES))
        return carry

    assert n_chunks % DN_CHUNK_UNROLL == 0
    lax.fori_loop(0, n_chunks // DN_CHUNK_UNROLL, chunk_group_step, 0)

    s_s[...] = jnp.zeros(s_s.shape, F32)

    def rec_step(i, carry):
        cs = (i, n_chunks - 1 - i)
        r0s = [pl.multiple_of(c * c64, c64) for c in cs]
        b0s = [pl.multiple_of(c * c2, c2) for c in cs]
        rows = [pl.ds(pl.multiple_of(b0s[d] + d * c64, c64), c64) for d in range(2)]
        keeps = (lane_blk < c64, lane_blk >= c64)
        states = [[s_s[hh * 2 + d] for hh in range(nh)] for d in range(2)]
        wq_s = [_pair_dot([jnp.concatenate([ws[hh, rows[d], :], qds[hh, rows[d], :]], axis=0) for hh in range(nh)],
                          states[d]) for d in range(2)]
        vbs = [[(us[hh, rows[d], :].astype(F32) - wq_s[d][hh][:c64]).astype(BF16) for hh in range(nh)]
               for d in range(2)]
        lhs2 = [[jnp.concatenate([qks[hh, rows[d], :],
                                  jnp.where(keeps[d], kdts[hh, pl.ds(b0s[d], c2), :], jnp.zeros((), BF16))], axis=0)
                 for hh in range(nh)] for d in range(2)]
        r2 = [_pair_dot(lhs2[d], [jnp.concatenate([x, x], axis=0) for x in vbs[d]]) for d in range(2)]
        for d in range(2):
            for hh in range(nh):
                cd = cds[hh, pl.ds(pl.multiple_of(cs[d] * 16 + d * 8, 8), 8), :][0:1, :]
                s_s[hh * 2 + d] = states[d][hh] * cd + r2[d][hh][c64:]
                (of_s if d == 0 else ob_s)[pl.ds(r0s[d], c64), hh * LANES:(hh + 1) * LANES] = (
                    wq_s[d][hh][c64:] + r2[d][hh][:c64])
        return carry

    lax.fori_loop(0, n_chunks, rec_step, 0)

    gain = on_ref[...]

    def emit(o, z):
        outs = []
        for hh in range(nh):
            oh = o[:, hh * LANES:(hh + 1) * LANES]
            outs.append(oh * lax.rsqrt(jnp.mean(oh * oh, axis=-1, keepdims=True) + NORM_EPS) * gain)
        on = jnp.concatenate(outs, axis=1)
        zf = z.astype(F32)
        return (on * (zf * _sigmoid(zf))).astype(BF16)

    n0 = c64 - DN_PAD
    o0 = of_s[DN_PAD:c64, :] + ob_s[DN_PAD:c64, :]
    y_ref[0, 0:n0, :] = emit(o0, z_ref[0, 0:n0, :])

    def out_step(c, carry):
        r0 = pl.multiple_of(c * c64, c64)
        l0 = pl.multiple_of(c * c64 - DN_PAD, 16)
        o = of_s[pl.ds(r0, c64), :] + ob_s[pl.ds(r0, c64), :]
        y_ref[0, pl.ds(l0, c64), :] = emit(o, z_ref[0, pl.ds(l0, c64), :])
        return carry

    lax.fori_loop(1, n_chunks, out_step, 0)


def _deltanet(proj3, ba3, conv_w, gpar, out_norm):
    b, l, _ = proj3.shape
    nh = DN_NH
    wd = nh * DN_HEAD_DIM
    n_rows = l + DN_PAD
    n_chunks = n_rows // DN_CHUNK
    ng = DN_NG
    blk = lambda col0: pl.BlockSpec((1, l, wd), lambda i, g, col0=col0: (i, 0, col0 // nh + g))
    cblk = lambda part: pl.BlockSpec((DN_CONV, wd), lambda i, g, part=part: (0, part * ng + g))
    return pl.pallas_call(
        functools.partial(_dn_body, seq_len=l),
        grid=(b, ng),
        in_specs=[
            blk(COL_DQ), blk(COL_DK), blk(COL_DV), blk(COL_DZ),
            pl.BlockSpec((1, l, LANES), lambda i, g: (i, 0, g)),
            cblk(0), cblk(1), cblk(2),
            pl.BlockSpec((1, 2, LANES), lambda i, g: (g, 0, 0)),
            pl.BlockSpec((1, LANES), lambda i, g: (0, 0)),
        ],
        out_specs=pl.BlockSpec((1, l, wd), lambda i, g: (i, 0, g)),
        out_shape=jax.ShapeDtypeStruct((b, l, DN_WIDTH), BF16),
        scratch_shapes=[
            pltpu.VMEM((n_rows + 16, wd), F32), pltpu.VMEM((n_rows + 16, wd), F32),
            pltpu.VMEM((n_rows + 16, wd), F32),
            pltpu.VMEM((n_rows, wd), F32), pltpu.VMEM((n_rows, wd), F32), pltpu.VMEM((n_rows, wd), F32),
            pltpu.VMEM((n_rows, LANES), F32), pltpu.VMEM((n_rows, LANES), F32),
            pltpu.VMEM((nh, 2 * n_rows, LANES), BF16), pltpu.VMEM((nh, 2 * n_rows, LANES), BF16),
            pltpu.VMEM((nh, 2 * n_rows, LANES), BF16),
            pltpu.VMEM((nh, 2 * n_rows, LANES), BF16),
            pltpu.VMEM((nh, 2 * n_rows, LANES), BF16),
            pltpu.VMEM((nh, n_chunks * 16, LANES), F32),
            pltpu.VMEM((n_rows, wd), F32), pltpu.VMEM((n_rows, wd), F32),
            pltpu.VMEM((DN_NCH, LANES, LANES), F32),
        ],
        compiler_params=pltpu.CompilerParams(
            dimension_semantics=("arbitrary", "arbitrary"), vmem_limit_bytes=VMEM_LIMIT),
        name="deltanet",
    )(proj3, proj3, proj3, proj3, ba3, conv_w, conv_w, conv_w, gpar, out_norm)


def _na_body(q_ref, k_ref, v_ref, tb_ref, o_ref, *, rows):
    scale = NA_HEAD_DIM ** -0.5
    wr = NA_WIN_R
    nk = wr * GRID_W
    qm = q_ref[0, 0:N_META, :]
    km = k_ref[0, 0:N_META, :]
    vm = v_ref[0, 0:N_META, :]

    s = _dot_nt(qm, km) * scale
    p = jnp.exp(s - jnp.max(s, axis=-1, keepdims=True))
    o = _dot(p.astype(BF16), vm) / jnp.sum(p, axis=-1, keepdims=True)
    o_ref[0, 0:N_META, :] = o.astype(BF16)

    pad = jnp.zeros((NA_KEYS - nk - N_META, LANES), BF16)

    def row_group_step(grp, carry):
        qs, ks, vs, biases, dsts = [], [], [], [], []
        for j in range(NA_ROW_GROUP):
            r = grp * NA_ROW_GROUP + j
            rs = jnp.clip(r - wr // 2, 0, rows - wr)
            dsts.append(pl.ds(pl.multiple_of(N_META + r * GRID_W, 16), GRID_W))
            win = pl.ds(pl.multiple_of(N_META + rs * GRID_W, 16), nk)
            qs.append(q_ref[0, dsts[j], :])
            ks.append(jnp.concatenate([k_ref[0, win, :], km, pad], axis=0))
            vs.append(jnp.concatenate([v_ref[0, win, :], vm, pad], axis=0))
            biases.append(tb_ref[0, rs - r + NA_WIN_R - 1])
        ss = [_dot_nt(qs[j], ks[j]) * scale + biases[j] for j in range(NA_ROW_GROUP)]
        ps = [jnp.exp(s - jnp.max(s, axis=-1, keepdims=True)) for s in ss]
        os_ = [_dot(ps[j].astype(BF16), vs[j]) / jnp.sum(ps[j], axis=-1, keepdims=True) for j in range(NA_ROW_GROUP)]
        for j in range(NA_ROW_GROUP):
            o_ref[0, dsts[j], :] = os_[j].astype(BF16)
        return carry

    assert rows % NA_ROW_GROUP == 0
    lax.fori_loop(0, rows // NA_ROW_GROUP, row_group_step, 0)


def _na_bias_table(rpb):
    wr, wc = NA_WIN_R, NA_WIN_C
    qc = np.arange(GRID_W)[:, None]
    kc = np.arange(GRID_W)[None, :]
    q_start = np.clip(qc - wc // 2, 0, GRID_W - wc)
    valid = (kc - q_start >= 0) & (kc - q_start < wc)
    dc = np.clip(kc - qc + wc - 1, 0, 2 * wc - 2)
    i0 = np.arange(wr)[:, None]
    jj = np.arange(wr)[None, :]
    dr = i0 + jj
    tb = rpb.astype(F32)[:, dr][:, :, :, dc]
    tb = jnp.where(valid[None, None, None], tb, -jnp.inf)
    tb = tb.transpose(0, 1, 3, 2, 4)
    tb = tb.reshape(NA_HEADS, wr, GRID_W, wr * GRID_W)
    meta = jnp.zeros(tb.shape[:3] + (N_META,), F32)
    fill = jnp.full(tb.shape[:3] + (NA_KEYS - wr * GRID_W - N_META,), -jnp.inf, F32)
    return jnp.concatenate([tb, meta, fill], axis=-1)


def _natten(proj3, tb):
    b, l, _ = proj3.shape
    rows = (l - N_META) // GRID_W
    assert rows >= NA_WIN_R
    blk = lambda col0: pl.BlockSpec((1, l, LANES), lambda h, i, col0=col0: (i, 0, col0 + h))
    return pl.pallas_call(
        functools.partial(_na_body, rows=rows),
        grid=(NA_HEADS, b),
        in_specs=[
            blk(COL_NQ), blk(COL_NK), blk(COL_NV),
            pl.BlockSpec((1, NA_WIN_R, GRID_W, NA_KEYS), lambda h, i: (h, 0, 0, 0)),
        ],
        out_specs=pl.BlockSpec((1, l, LANES), lambda h, i: (i, 0, h)),
        out_shape=jax.ShapeDtypeStruct((b, l, NA_WIDTH), BF16),
        compiler_params=pltpu.CompilerParams(
            dimension_semantics=("arbitrary", "arbitrary"), vmem_limit_bytes=VMEM_LIMIT),
        name="natten",
    )(proj3, proj3, proj3, tb)


def _mix_body(ydn_ref, yna_ref, ga_ref, gb_ref, h_ref, wdn_ref, wna_ref, wo_ref, gn_ref, wr_ref, br_ref,
              h2_ref, u2_ref, gate_ref, idx_ref):
    a = _dot(ydn_ref[...], wdn_ref[...])
    b = _dot(yna_ref[...], wna_ref[...])
    mix = _sigmoid(ga_ref[...].astype(F32)) * a + _sigmoid(gb_ref[...].astype(F32)) * b
    h2 = h_ref[...] + _dot(mix.astype(BF16), wo_ref[...])
    h2_ref[...] = h2
    u2 = h2 * lax.rsqrt(jnp.mean(h2 * h2, axis=-1, keepdims=True) + NORM_EPS) * gn_ref[...]
    _store_token_tiles(u2_ref, _pack_halves(u2), u2.shape[0])
    u_hi = u2.astype(BF16)
    u_lo = (u2 - u_hi.astype(F32)).astype(BF16)
    w_r = wr_ref[...]
    w_hi = w_r.astype(BF16)
    w_lo = (w_r - w_hi.astype(F32)).astype(BF16)
    logits = _dot(u_hi, w_hi) + (_dot(u_hi, w_lo) + _dot(u_lo, w_hi)) + br_ref[...]
    lane = lax.broadcasted_iota(I32, logits.shape, 1)
    logits = jnp.where(lane < N_EXPERTS, logits, -jnp.inf)
    vals, idxs = [], []
    for _ in range(TOP_K):
        m = jnp.max(logits, axis=-1, keepdims=True)
        sel = jnp.min(jnp.where(logits == m, lane, LANES), axis=-1, keepdims=True)
        vals.append(m)
        idxs.append(sel)
        logits = jnp.where(lane == sel, -jnp.inf, logits)
    es = [jnp.exp(v - vals[0]) for v in vals]
    den = es[0] + es[1] + es[2] + es[3]
    gates = jnp.zeros(logits.shape, F32)
    idx = jnp.zeros(logits.shape, I32)
    for k in range(TOP_K):
        gates = jnp.where(lane == k, es[k] / den, gates)
        idx = jnp.where(lane == k, idxs[k], idx)
    gate_ref[...] = gates
    idx_ref[...] = idx


def _mix(y_dn, y_na, proj, h, w_dn, w_na, w_o, gain, w_r, b_r, tm):
    t, d = h.shape
    row = lambda w: pl.BlockSpec((tm, w), lambda i: (i, 0))
    full = lambda a: pl.BlockSpec(a.shape, lambda i: (0,) * a.ndim)
    return pl.pallas_call(
        _mix_body,
        grid=(t // tm,),
        in_specs=[
            row(DN_WIDTH), row(NA_WIDTH),
            pl.BlockSpec((tm, d), lambda i: (i, COL_GA * LANES // d)),
            pl.BlockSpec((tm, d), lambda i: (i, COL_GB * LANES // d)),
            row(d), full(w_dn), full(w_na), full(w_o), full(gain), full(w_r), full(b_r),
        ],
        out_specs=[row(d), pl.BlockSpec((tm * SUBLANES, LANES), lambda i: (i, 0)), row(LANES), row(LANES)],
        out_shape=[
            jax.ShapeDtypeStruct((t, d), F32), jax.ShapeDtypeStruct((t * SUBLANES, LANES), U32),
            jax.ShapeDtypeStruct((t, LANES), F32), jax.ShapeDtypeStruct((t, LANES), I32),
        ],
        compiler_params=pltpu.CompilerParams(
            dimension_semantics=("arbitrary",), vmem_limit_bytes=VMEM_LIMIT),
        name="mix",
    )(y_dn, y_na, proj, proj, h, w_dn, w_na, w_o, gain, w_r, b_r)


def _moe_body(blk_exp, n_valid, idx_hbm, u_hbm, wg_ref, wl_ref, bg_ref, bl_ref, wo_ref, bo_ref, y_hbm,
              idx_s, xbuf, xs, acc, stage, isem, gsem, ssem, *, n_blocks):
    i = pl.program_id(0)
    f = pl.program_id(1)
    slot = i % 2
    nxt = (i + 1) % 2
    tm = MOE_TM

    def idx_copy(blk, sl):
        return pltpu.make_async_copy(idx_hbm.at[pl.ds(pl.multiple_of(blk * (2 * tm), 2 * tm), 2 * tm)],
                                     idx_s.at[pl.ds(pl.multiple_of(sl * (2 * tm), 2 * tm), 2 * tm)], isem.at[sl])

    def tile(ref, row0):
        return ref.at[pl.ds(pl.multiple_of(row0, SUBLANES), SUBLANES), :]

    def gather_rows(sl):
        def body(r, carry):
            src = idx_s[sl * (2 * tm) + r]
            pltpu.make_async_copy(tile(u_hbm, src), tile(xbuf.at[sl], r * SUBLANES), gsem.at[sl]).start()
            return carry
        lax.fori_loop(0, tm, body, 0, unroll=DMA_ISSUE_UNROLL)

    def gather_wait(sl):
        pltpu.make_async_copy(xbuf.at[sl], xbuf.at[sl], gsem.at[sl]).wait()

    def scatter_rows(sl):
        def body(r, carry):
            dst = idx_s[sl * (2 * tm) + tm + r]
            pltpu.make_async_copy(tile(stage, r * SUBLANES), tile(y_hbm, dst), ssem).start()
            return carry
        lax.fori_loop(0, tm, body, 0, unroll=DMA_ISSUE_UNROLL)

    def scatter_wait():
        pltpu.make_async_copy(stage, stage, ssem).wait()

    live = n_valid[i] > 0

    @pl.when((i == 0) & (f == 0))
    def _():
        stage[...] = jnp.zeros(stage.shape, U32)
        dump = pltpu.make_async_copy(stage, y_hbm.at[pl.ds(y_hbm.shape[0] - tm * SUBLANES, tm * SUBLANES), :], ssem)
        dump.start()
        dump.wait()
        idx_copy(0, 0).start()
        idx_copy(0, 0).wait()

        @pl.when(live)
        def _():
            gather_rows(0)

    @pl.when((f == 0) & live)
    def _():
        gather_wait(slot)
        for s in range(SUBLANES):
            lo, hi = _unpack_halves(xbuf[slot, pl.ds(s, tm, stride=SUBLANES), :])
            xs[:, s * LANES:(s + 1) * LANES] = lo.astype(BF16)
            xs[:, HALF + s * LANES:HALF + (s + 1) * LANES] = hi.astype(BF16)

    @pl.when((f == 0) & (i + 1 < n_blocks))
    def _():
        idx_copy(i + 1, nxt).start()

    @pl.when((f == 1) & (i + 1 < n_blocks))
    def _():
        idx_copy(i + 1, nxt).wait()

        @pl.when(n_valid[jnp.minimum(i + 1, n_blocks - 1)] > 0)
        def _():
            gather_rows(nxt)

    @pl.when(live)
    def _():
        x = xs[...]
        hg = _dot(x, wg_ref[0, 0]) + bg_ref[0]
        hl = _dot(x, wl_ref[0, 0]) + bl_ref[0]
        glu = jnp.minimum(hg, SWIGLU_LIMIT)
        lin = jnp.clip(hl, -SWIGLU_LIMIT, SWIGLU_LIMIT)
        act = glu * _sigmoid(SWIGLU_ALPHA * glu) * (lin + 1.0)
        part = _dot(act.astype(BF16), wo_ref[0])

        @pl.when(f == 0)
        def _():
            acc[...] = part

        @pl.when(f > 0)
        def _():
            acc[...] += part

    @pl.when(f == MOE_NF - 1)
    def _():
        @pl.when((i > 0) & (n_valid[jnp.maximum(i - 1, 0)] > 0))
        def _():
            scatter_wait()

        @pl.when(live)
        def _():
            _store_token_tiles(stage, _pack_halves(acc[...] + bo_ref[0]), tm)
            scatter_rows(slot)

        @pl.when((i == n_blocks - 1) & live)
        def _():
            scatter_wait()


def _moe(blk_exp, n_valid, idx_blocks, u2p, w_in, b_in, w_out, b_out, n_rows_out):
    n_blocks = blk_exp.shape[0]
    d = D_MODEL
    nf = MOE_NF

    def wmap(off):
        def index_map(i, f, be, nv):
            fe = jnp.where(nv[i] > 0, f, nf - 1)
            return (be[i], fe + off, 0, 0)
        return index_map

    def bmap(off):
        def index_map(i, f, be, nv):
            fe = jnp.where(nv[i] > 0, f, nf - 1)
            return (be[i], 0, fe + off)
        return index_map

    def womap(i, f, be, nv):
        return (be[i], jnp.where(nv[i] > 0, f, nf - 1), 0)

    grid_spec = pltpu.PrefetchScalarGridSpec(
        num_scalar_prefetch=2,
        grid=(n_blocks, nf),
        in_specs=[
            pl.BlockSpec(memory_space=pl.ANY),
            pl.BlockSpec(memory_space=pl.ANY),
            pl.BlockSpec((1, 1, d, MOE_TF), wmap(0)),
            pl.BlockSpec((1, 1, d, MOE_TF), wmap(nf)),
            pl.BlockSpec((1, 1, MOE_TF), bmap(0)),
            pl.BlockSpec((1, 1, MOE_TF), bmap(nf)),
            pl.BlockSpec((1, MOE_TF, d), womap),
            pl.BlockSpec((1, 1, d), lambda i, f, be, nv: (be[i], 0, 0)),
        ],
        out_specs=pl.BlockSpec(memory_space=pl.ANY),
        scratch_shapes=[
            pltpu.SMEM((2 * 2 * MOE_TM,), I32),
            pltpu.VMEM((2, MOE_TM * SUBLANES, LANES), U32),
            pltpu.VMEM((MOE_TM, d), BF16),
            pltpu.VMEM((MOE_TM, d), F32),
            pltpu.VMEM((MOE_TM * SUBLANES, LANES), U32),
            pltpu.SemaphoreType.DMA((2,)),
            pltpu.SemaphoreType.DMA((2,)),
            pltpu.SemaphoreType.DMA,
        ],
    )
    return pl.pallas_call(
        functools.partial(_moe_body, n_blocks=n_blocks),
        grid_spec=grid_spec,
        out_shape=jax.ShapeDtypeStruct((n_rows_out * SUBLANES, LANES), U32),
        compiler_params=pltpu.CompilerParams(
            dimension_semantics=("arbitrary", "arbitrary"), vmem_limit_bytes=VMEM_LIMIT),
        name="moe",
    )(blk_exp, n_valid, idx_blocks, u2p, w_in, w_in, b_in, b_in, w_out, b_out)


def _routing_metadata(top_idx, n_blocks):
    tm = MOE_TM
    e = top_idx[:, :TOP_K].reshape(-1)
    n_asg = e.shape[0]
    order = jnp.argsort(e, stable=True).astype(I32)
    counts = jnp.sum((e[:, None] == jnp.arange(N_EXPERTS, dtype=I32)[None, :]).astype(I32), axis=0)
    cstart = jnp.cumsum(counts) - counts
    nblk = (counts + tm - 1) // tm
    bend = jnp.cumsum(nblk)
    bstart = bend - nblk
    blk = jnp.arange(n_blocks, dtype=I32)
    blk_exp = jnp.minimum(jnp.sum((bend[None, :] <= blk[:, None]).astype(I32), axis=1), N_EXPERTS - 1)
    first = (blk - bstart[blk_exp]) * tm
    n_valid = jnp.where(blk < bend[-1], jnp.clip(counts[blk_exp] - first, 0, tm), 0).astype(I32)
    r = jnp.arange(tm, dtype=I32)[None, :]
    valid = r < n_valid[:, None]
    asg = order[jnp.where(valid, (cstart[blk_exp] + first)[:, None] + r, 0)]
    src_row = (asg // TOP_K) * SUBLANES
    dst_row = jnp.where(valid, asg, n_asg + r) * SUBLANES
    idx_blocks = jnp.concatenate([src_row, dst_row], axis=1).reshape(-1)
    return blk_exp, n_valid, idx_blocks


def _final_body(h_ref, y_ref, gate_ref, gn_ref, o_ref):
    h = h_ref[...]
    gates = gate_ref[...]
    lo_acc = jnp.zeros((h.shape[0], HALF), F32)
    hi_acc = jnp.zeros((h.shape[0], HALF), F32)
    for k in range(TOP_K):
        lo, hi = _unpack_halves(_load_token_tiles(y_ref, h.shape[0], k * SUBLANES, TOP_K * SUBLANES))
        g = gates[:, k:k + 1]
        lo_acc = lo_acc + g * lo
        hi_acc = hi_acc + g * hi
    y = h + jnp.concatenate([lo_acc, hi_acc], axis=1)
    o_ref[...] = y * lax.rsqrt(jnp.mean(y * y, axis=-1, keepdims=True) + NORM_EPS) * gn_ref[...]


def _final(h2, y4, gates, gain, tm):
    t, d = h2.shape
    return pl.pallas_call(
        _final_body,
        grid=(t // tm,),
        in_specs=[
            pl.BlockSpec((tm, d), lambda i: (i, 0)),
            pl.BlockSpec((tm * TOP_K * SUBLANES, LANES), lambda i: (i, 0)),
            pl.BlockSpec((tm, LANES), lambda i: (i, 0)),
            pl.BlockSpec((1, d), lambda i: (0, 0)),
        ],
        out_specs=pl.BlockSpec((tm, d), lambda i: (i, 0)),
        out_shape=jax.ShapeDtypeStruct((t, d), F32),
        compiler_params=pltpu.CompilerParams(
            dimension_semantics=("arbitrary",), vmem_limit_bytes=VMEM_LIMIT),
        name="final",
    )(h2, y4, gates, gain)


def _expert_in_tiles(w):
    e, d, f2 = w.shape
    return w.reshape(e, d, f2 // MOE_TF, MOE_TF).transpose(0, 2, 1, 3).astype(BF16)


def _split_w_in(w_in):
    n_dn = 4 * DN_WIDTH
    n_na = n_dn + 4 * DN_HEADS
    w_main = jnp.concatenate(
        [w_in[:, :n_dn], w_in[:, n_na + 3 * NA_WIDTH:], w_in[:, n_na:n_na + 3 * NA_WIDTH]], axis=1).astype(BF16)
    wb = w_in[:, n_dn:n_dn + 2 * DN_HEADS]
    wa = w_in[:, n_dn + 2 * DN_HEADS:n_dn + 4 * DN_HEADS]
    return w_main, _group_lanes(wb, wa).astype(BF16)


def _group_lanes(b_part, a_part):
    nh, ng = DN_NH, DN_NG
    lead = b_part.shape[:-1]
    bp = b_part.reshape(lead + (2, ng, nh))
    ap = a_part.reshape(lead + (2, ng, nh))
    perm = tuple(range(len(lead))) + (len(lead) + 1, len(lead), len(lead) + 2)
    bp = bp.transpose(perm).reshape(lead + (ng, 2 * nh))
    ap = ap.transpose(perm).reshape(lead + (ng, 2 * nh))
    both = jnp.concatenate([bp, ap], axis=-1)
    pad = [(0, 0)] * (both.ndim - 1) + [(0, LANES - 4 * nh)]
    return jnp.pad(both, pad).reshape(lead + (ng * LANES,))


def kernel(x, meta_tokens, norm_mix, w_in, dn_conv, dn_a_log, dn_dt_bias, dn_out_norm, na_rpb,
           w_branch_dn, w_branch_na, w_out, norm_ffn, w_router, b_router, w_exp_in, b_exp_in,
           w_exp_out, b_exp_out, norm_final):
    bsz, seq, d = x.shape
    l = seq + N_META
    t = bsz * l
    meta = jnp.broadcast_to(meta_tokens[None].astype(x.dtype), (bsz, N_META, d))
    h = jnp.concatenate([meta, x], axis=1).reshape(t, d)
    depth = w_in.shape[0]
    assert depth == 1, "the final kernel fuses the last residual add with the final RMSNorm"
    tm_proj = 768 if t % 768 == 0 else 8 * (l // 8)
    tm_mix = 384 if t % 384 == 0 else 8 * (l // 8)
    n_blocks = t * TOP_K // MOE_TM + N_EXPERTS
    for layer in range(depth):
        w_main, w_ba = _split_w_in(w_in[layer])
        proj, ba = _inproj(h, norm_mix[layer][None, :], w_main, w_ba, tm_proj, 1024)
        proj3 = proj.reshape(bsz, l, N_MAIN_COLS)
        ba3 = ba.reshape(bsz, l, DN_NG * LANES)
        gpar = jnp.stack([_group_lanes(jnp.zeros_like(dn_a_log[layer]).reshape(-1), dn_a_log[layer].reshape(-1)),
                          _group_lanes(jnp.zeros_like(dn_dt_bias[layer]).reshape(-1),
                                       dn_dt_bias[layer].reshape(-1))], axis=0)
        gpar = gpar.reshape(2, DN_NG, LANES).transpose(1, 0, 2)
        y_dn = _deltanet(proj3, ba3, dn_conv[layer], gpar, dn_out_norm[layer][None, :])
        y_na = _natten(proj3, _na_bias_table(na_rpb[layer]))
        w_r = jnp.pad(w_router[layer], ((0, 0), (0, LANES - N_EXPERTS)))
        b_r = jnp.pad(b_router[layer], (0, LANES - N_EXPERTS))[None, :]
        h2, u2p, gates, top_idx = _mix(
            y_dn.reshape(t, DN_WIDTH), y_na.reshape(t, NA_WIDTH), proj, h,
            w_branch_dn[layer].astype(BF16), w_branch_na[layer].astype(BF16), w_out[layer].astype(BF16),
            norm_ffn[layer][None, :], w_r, b_r, tm_mix)
        blk_exp, n_valid, idx_blocks = _routing_metadata(top_idx, n_blocks)
        y4 = _moe(blk_exp, n_valid, idx_blocks, u2p,
                  _expert_in_tiles(w_exp_in[layer]), b_exp_in[layer][:, None, :],
                  w_exp_out[layer].astype(BF16), b_exp_out[layer][:, None, :], t * TOP_K + MOE_TM)
        h = _final(h2, y4, gates, norm_final[None, :], tm_mix)
    return h.reshape(bsz, l, d)[:, N_META:]
```

```python
import functools

import jax
import jax.numpy as jnp
import numpy as np
from jax import lax
from jax.experimental import pallas as pl
from jax.experimental.pallas import tpu as pltpu

F32 = jnp.float32
BF16 = jnp.bfloat16
I32 = jnp.int32
U32 = jnp.uint32

D_MODEL = 2048
N_META = 16
GRID_W = 64
DN_HEADS = 8
DN_HEAD_DIM = 128
DN_WIDTH = 1024
DN_CONV = 5
DN_CHUNK = 64
NA_HEADS = 8
NA_HEAD_DIM = 128
NA_WIDTH = 1024
NA_WIN_R = 8
NA_WIN_C = 16
N_EXPERTS = 32
TOP_K = 4
D_EXPERT = 2048
SWIGLU_ALPHA = 1.702
SWIGLU_LIMIT = 7.0
NORM_EPS = 1e-6

LANES = 128
NORM_ROWS = 16
VMEM_LIMIT = 56 * 1024 * 1024

COL_DQ, COL_DK, COL_DV, COL_DZ = 0, 8, 16, 24
COL_GA, COL_GB = 32, 48
COL_NQ, COL_NK, COL_NV = 64, 72, 80
N_MAIN_COLS = 88 * LANES

DN_NH = 2
DN_NG = DN_HEADS // DN_NH
DN_NCH = 2 * DN_NH
DN_PAD = (-N_META) % DN_CHUNK
DN_CHUNK_UNROLL = 3

DMA_ISSUE_UNROLL = 8

NA_ROW_GROUP = 4
NA_KEYS = 640

MOE_TM = 512
MOE_TF = 512
MOE_SUB = 256
MOE_NF = D_EXPERT // MOE_TF
HALF = D_MODEL // 2
SUBLANES = 8
assert HALF == SUBLANES * LANES


def _store_token_tiles(ref, packed, n_tok, base=0, group=SUBLANES):
    for s in range(SUBLANES):
        ref[pl.ds(base + s, n_tok, stride=group), :] = packed[:, s * LANES:(s + 1) * LANES]


def _load_token_tiles(ref, n_tok, base=0, group=SUBLANES):
    return jnp.concatenate([ref[pl.ds(base + s, n_tok, stride=group), :] for s in range(SUBLANES)], axis=1)


def _dot(a, b, **kw):
    return jnp.dot(a, b, preferred_element_type=F32, **kw)


def _dot_nt(a, b, **kw):
    return lax.dot_general(a, b, (((1,), (1,)), ((), ())), preferred_element_type=F32, **kw)


def _dot_tn(a, b, **kw):
    return lax.dot_general(a, b, (((0,), (0,)), ((), ())), preferred_element_type=F32, **kw)


def _sigmoid(x):
    return 1.0 / (1.0 + jnp.exp(-x))


def _pack_halves(x):
    n = x.shape[1] // 2
    lo = pltpu.bitcast(x[:, :n].astype(BF16).astype(F32), U32)
    hi = pltpu.bitcast(x[:, n:].astype(BF16).astype(F32), U32)
    return (hi & jnp.uint32(0xFFFF0000)) | (lo >> 16)


def _unpack_halves(p):
    lo = pltpu.bitcast(p << 16, F32)
    hi = pltpu.bitcast(p & jnp.uint32(0xFFFF0000), F32)
    return lo, hi


def _inproj_body(h_ref, g_ref, w_ref, wba_ref, o_ref, oba_ref, u_ref):
    @pl.when(pl.program_id(1) == 0)
    def _():
        gain = g_ref[...]

        def norm_rows(c, carry):
            rows = pl.ds(pl.multiple_of(c * NORM_ROWS, NORM_ROWS), NORM_ROWS)
            x = h_ref[rows, :]
            y = x * lax.rsqrt(jnp.mean(x * x, axis=-1, keepdims=True) + NORM_EPS) * gain
            u_ref[rows, :] = y.astype(BF16)
            return carry

        lax.fori_loop(0, h_ref.shape[0] // NORM_ROWS, norm_rows, 0)
        oba_ref[...] = _dot(u_ref[...], wba_ref[...])

    o_ref[...] = _dot(u_ref[...], w_ref[...]).astype(BF16)


def _inproj(h, gain, w_main, w_ba, tm, tn):
    t, d = h.shape
    n = w_main.shape[1]
    nba = w_ba.shape[1]
    return pl.pallas_call(
        _inproj_body,
        grid=(t // tm, n // tn),
        in_specs=[
            pl.BlockSpec((tm, d), lambda i, j: (i, 0), pipeline_mode=pl.Buffered(1)),
            pl.BlockSpec((1, d), lambda i, j: (0, 0)),
            pl.BlockSpec((d, tn), lambda i, j: (0, j)),
            pl.BlockSpec((d, nba), lambda i, j: (0, 0)),
        ],
        out_specs=[
            pl.BlockSpec((tm, tn), lambda i, j: (i, j)),
            pl.BlockSpec((tm, nba), lambda i, j: (i, 0)),
        ],
        out_shape=[jax.ShapeDtypeStruct((t, n), BF16), jax.ShapeDtypeStruct((t, nba), F32)],
        scratch_shapes=[pltpu.VMEM((tm, d), BF16)],
        compiler_params=pltpu.CompilerParams(
            dimension_semantics=("arbitrary", "arbitrary"), vmem_limit_bytes=VMEM_LIMIT),
        name="inproj",
    )(h, gain, w_main, w_ba)


def _pair_dot(lhs, rhs):
    w = rhs[0].shape[1]
    r0, r1 = rhs[0].astype(BF16), rhs[1].astype(BF16)
    zero = jnp.zeros(r0.shape, BF16)
    bd = jnp.concatenate([jnp.concatenate([r0, zero], axis=1), jnp.concatenate([zero, r1], axis=1)], axis=0)
    x = _dot(jnp.concatenate([lhs[0].astype(BF16), lhs[1].astype(BF16)], axis=1), bd)
    return x[:, :w], x[:, w:]


def _pair_neumann_inverse(a_pairs, nilpotency):
    c = a_pairs[0][0].shape[0]
    eye = (lax.broadcasted_iota(I32, (c, c), 0) == lax.broadcasted_iota(I32, (c, c), 1)).astype(F32)
    ns = [[-a for a in pair] for pair in a_pairs]
    ps = [[eye + x for x in n] for n in ns]
    ns = [list(_pair_dot(n, n)) for n in ns]
    steps = int(np.log2(nilpotency)) - 1
    for step in range(steps):
        if step < steps - 1:
            pns = [_pair_dot([jnp.concatenate([p[i], n[i]], axis=0) for i in range(2)], n) for p, n in zip(ps, ns)]
            ps = [[p[i] + pn[i][:c] for i in range(2)] for p, pn in zip(ps, pns)]
            ns = [[pn[i][c:] for i in range(2)] for pn in pns]
        else:
            pns = [_pair_dot(p, n) for p, n in zip(ps, ns)]
            ps = [[p[i] + pn[i] for i in range(2)] for p, pn in zip(ps, pns)]
    return ps


def _chunk_cumsum(x, half):
    rows = x.shape[0]
    r = lax.broadcasted_iota(I32, x.shape, 0)
    top = r < half
    s = 1
    while s < half:
        down = pltpu.roll(x, s, axis=0)
        up = pltpu.roll(x, rows - s, axis=0)
        x = x + jnp.where(top, jnp.where(r >= s, down, 0.0), jnp.where(r < rows - s, up, 0.0))
        s *= 2
    return x


def _dn_body(q_ref, k_ref, v_ref, z_ref, ba_ref, cq_ref, ck_ref, cv_ref, gp_ref, on_ref, y_ref,
             xq, xk, xv, qn, kn, vn, gbf, gbb, us, ws, qds, kdts, qks, cds, of_s, ob_s, s_s, *, seq_len):
    nh = DN_NH
    wd = nh * DN_HEAD_DIM
    c64 = DN_CHUNK
    n_chunks = (seq_len + DN_PAD) // c64
    n_rows = n_chunks * c64
    lead = 8 + DN_PAD

    for x in (xq, xk, xv):
        x[0:lead, :] = jnp.zeros((lead, wd), F32)
        x[lead + seq_len:lead + seq_len + 8, :] = jnp.zeros((8, wd), F32)
    gbf[0:DN_PAD, :] = jnp.zeros((DN_PAD, LANES), F32)
    gbb[0:DN_PAD, :] = jnp.zeros((DN_PAD, LANES), F32)

    neg_a = -jnp.exp(gp_ref[0, 0:1, :])
    dt_b = gp_ref[0, 1:2, :]

    def copy_rows(src0, pos0, n):
        src = pl.ds(src0, n)
        dst = pl.ds(pl.multiple_of(8 + pos0, 8), n)
        xq[dst, :] = q_ref[0, src, :].astype(F32)
        xk[dst, :] = k_ref[0, src, :].astype(F32)
        xv[dst, :] = v_ref[0, src, :].astype(F32)
        ba = ba_ref[0, src, :]
        lane = lax.broadcasted_iota(I32, ba.shape, 1)
        beta = _sigmoid(ba)
        sp_in = ba + dt_b
        softplus = jnp.maximum(sp_in, 0.0) + jnp.log(1.0 + jnp.exp(-jnp.abs(sp_in)))
        val = jnp.where(lane < 2 * nh, beta, jnp.where(lane < 4 * nh, neg_a * softplus, 0.0))
        dst_g = pl.ds(pl.multiple_of(pos0, 8), n)
        gbf[dst_g, :] = val
        gbb[dst_g, :] = pltpu.roll(val, LANES - nh, axis=1)

    n_grid = seq_len - N_META
    copy_rows(n_grid, DN_PAD, N_META)

    def copy_step(j, carry):
        copy_rows(pl.multiple_of(j * c64, c64), DN_PAD + N_META + j * c64, c64)
        return carry

    lax.fori_loop(0, n_grid // c64, copy_step, 0)

    row_iota = lax.broadcasted_iota(I32, (c64, wd), 0)

    def conv_step(c, carry):
        r0 = pl.multiple_of(c * c64, c64)
        live = (row_iota + r0) >= DN_PAD
        for x, cw, dst, kind in ((xq, cq_ref, qn, "q"), (xk, ck_ref, kn, "k"), (xv, cv_ref, vn, "v")):
            win = x[pl.ds(r0, c64 + 16), :]
            acc = jnp.zeros((c64, wd), F32)
            for j in range(DN_CONV):
                off = 8 - DN_CONV // 2 + j
                acc = acc + win[off:off + c64, :] * cw[j:j + 1, :]
            y = jnp.where(live, acc * _sigmoid(acc), 0.0)
            if kind == "v":
                dst[pl.ds(r0, c64), :] = y
            else:
                scale = DN_HEAD_DIM ** -0.5 if kind == "q" else 1.0
                for hh in range(nh):
                    yh = y[:, hh * LANES:(hh + 1) * LANES]
                    yn = yh * lax.rsqrt(jnp.sum(yh * yh, axis=-1, keepdims=True) + 1e-6)
                    dst[pl.ds(r0, c64), hh * LANES:(hh + 1) * LANES] = yn * scale
        return carry

    lax.fori_loop(0, n_chunks, conv_step, 0)

    c2 = 2 * c64
    ri = lax.broadcasted_iota(I32, (c2, c2), 0)
    ci = lax.broadcasted_iota(I32, (c2, c2), 1)
    fwd_blk = (ri < c64) & (ci < c64)
    bwd_blk = (ri >= c64) & (ci >= c64)
    incl = (fwd_blk & (ri >= ci)) | (bwd_blk & (ri <= ci))
    strict = (fwd_blk & (ri > ci)) | (bwd_blk & (ri < ci))
    lane_row = lax.broadcasted_iota(I32, (1, c2), 1)
    lane_blk = lax.broadcasted_iota(I32, (c2, c2), 1)

    def chunk_group_step(grp, carry):
        chunks = [grp * DN_CHUNK_UNROLL + j for j in range(DN_CHUNK_UNROLL)]
        r0s = [pl.multiple_of(c * c64, c64) for c in chunks]
        b0s = [pl.multiple_of(c * c2, c2) for c in chunks]
        gb2s, gam2s, gam2_ts, q2s, k2s, v2s, k2_ts, lhs_gs = [], [], [], [], [], [], [], []
        for r0 in r0s:
            gb2 = jnp.concatenate([gbf[pl.ds(r0, c64), :], gbb[pl.ds(r0, c64), :]], axis=0)
            gam2 = _chunk_cumsum(gb2, c64)
            gb2s.append(gb2)
            gam2s.append(gam2)
            gam2_ts.append(gam2.T)
            q2, k2, v2 = [], [], []
            for hh in range(nh):
                hs = slice(hh * LANES, (hh + 1) * LANES)
                q = qn[pl.ds(r0, c64), hs]
                k = kn[pl.ds(r0, c64), hs]
                v = vn[pl.ds(r0, c64), hs]
                q2.append(jnp.concatenate([q, q], axis=0))
                k2.append(jnp.concatenate([k, k], axis=0))
                v2.append(jnp.concatenate([v, v], axis=0))
            q2s.append(q2)
            k2s.append(k2)
            v2s.append(v2)
            k2_ts.append([x.T for x in k2])
            lhs_gs.append([jnp.concatenate([k2[hh], q2[hh]], axis=0) for hh in range(nh)])
        g2s = [_pair_dot(lhs_g, k2_t) for lhs_g, k2_t in zip(lhs_gs, k2_ts)]
        a_pairs, decays, e_gs, gam_rs, gam_lasts, beta_cs = [], [], [], [], [], []
        for j in range(DN_CHUNK_UNROLL):
            a_pair, decay, e_g, gam_r, gam_last, beta_c = [], [], [], [], [], []
            for hh in range(nh):
                gcol = 2 * nh + hh
                beta_c.append(gb2s[j][:, hh:hh + 1])
                gam_c = gam2s[j][:, gcol:gcol + 1]
                gam_r.append(gam2_ts[j][gcol:gcol + 1, :])
                gam_last.append((gam_c[c64 - 1:c64, :], gam_c[c64:c64 + 1, :]))
                decay.append(jnp.exp(jnp.where(incl, gam_c - gam_r[hh], -jnp.inf)))
                a_pair.append(jnp.where(strict, beta_c[hh] * g2s[j][hh][:c2] * decay[hh], 0.0))
                e_g.append(jnp.exp(gam_c))
            a_pairs.append(a_pair)
            decays.append(decay)
            e_gs.append(e_g)
            gam_rs.append(gam_r)
            gam_lasts.append(gam_last)
            beta_cs.append(beta_c)
        t_invs = _pair_neumann_inverse(a_pairs, c64)
        rhss = [[jnp.concatenate([beta_cs[j][hh] * v2s[j][hh], (beta_cs[j][hh] * e_gs[j][hh]) * k2s[j][hh]], axis=1)
                 for hh in range(nh)] for j in range(DN_CHUNK_UNROLL)]
        sols = [_pair_dot(t_inv, rhs) for t_inv, rhs in zip(t_invs, rhss)]
        for j, c in enumerate(chunks):
            b0 = b0s[j]
            for hh in range(nh):
                gl_f, gl_b = gam_lasts[j][hh]
                gl_row = jnp.where(lane_row < c64, gl_f, gl_b)
                us[hh, pl.ds(b0, c2), :] = sols[j][hh][:, :LANES].astype(BF16)
                ws[hh, pl.ds(b0, c2), :] = sols[j][hh][:, LANES:].astype(BF16)
                qds[hh, pl.ds(b0, c2), :] = (q2s[j][hh] * e_gs[j][hh]).astype(BF16)
                qks[hh, pl.ds(b0, c2), :] = (g2s[j][hh][c2:] * decays[j][hh]).astype(BF16)
                kdts[hh, pl.ds(b0, c2), :] = (k2_ts[j][hh] * jnp.exp(gl_row - gam_rs[j][hh])).astype(BF16)
                cds[hh, pl.ds(pl.multiple_of(c * 16, 16), 8), :] = jnp.broadcast_to(jnp.exp(gl_f), (8, LANES))
                cds[hh, pl.ds(pl.multiple_of(c * 16 + 8, 8), 8), :] = jnp.broadcast_to(jnp.exp(gl_b), (8, LANES))
        return carry

    assert n_chunks % DN_CHUNK_UNROLL == 0
    lax.fori_loop(0, n_chunks // DN_CHUNK_UNROLL, chunk_group_step, 0)

    s_s[...] = jnp.zeros(s_s.shape, F32)

    def rec_step(i, carry):
        cs = (i, n_chunks - 1 - i)
        r0s = [pl.multiple_of(c * c64, c64) for c in cs]
        b0s = [pl.multiple_of(c * c2, c2) for c in cs]
        rows = [pl.ds(pl.multiple_of(b0s[d] + d * c64, c64), c64) for d in range(2)]
        keeps = (lane_blk < c64, lane_blk >= c64)
        states = [[s_s[hh * 2 + d] for hh in range(nh)] for d in range(2)]
        wq_s = [_pair_dot([jnp.concatenate([ws[hh, rows[d], :], qds[hh, rows[d], :]], axis=0) for hh in range(nh)],
                          states[d]) for d in range(2)]
        vbs = [[(us[hh, rows[d], :].astype(F32) - wq_s[d][hh][:c64]).astype(BF16) for hh in range(nh)]
               for d in range(2)]
        lhs2 = [[jnp.concatenate([qks[hh, rows[d], :],
                                  jnp.where(keeps[d], kdts[hh, pl.ds(b0s[d], c2), :], jnp.zeros((), BF16))], axis=0)
                 for hh in range(nh)] for d in range(2)]
        r2 = [_pair_dot(lhs2[d], [jnp.concatenate([x, x], axis=0) for x in vbs[d]]) for d in range(2)]
        for d in range(2):
            for hh in range(nh):
                cd = cds[hh, pl.ds(pl.multiple_of(cs[d] * 16 + d * 8, 8), 8), :][0:1, :]
                s_s[hh * 2 + d] = states[d][hh] * cd + r2[d][hh][c64:]
                (of_s if d == 0 else ob_s)[pl.ds(r0s[d], c64), hh * LANES:(hh + 1) * LANES] = (
                    wq_s[d][hh][c64:] + r2[d][hh][:c64])
        return carry

    lax.fori_loop(0, n_chunks, rec_step, 0)

    gain = on_ref[...]

    def emit(o, z):
        outs = []
        for hh in range(nh):
            oh = o[:, hh * LANES:(hh + 1) * LANES]
            outs.append(oh * lax.rsqrt(jnp.mean(oh * oh, axis=-1, keepdims=True) + NORM_EPS) * gain)
        on = jnp.concatenate(outs, axis=1)
        zf = z.astype(F32)
        return (on * (zf * _sigmoid(zf))).astype(BF16)

    o0 = of_s[DN_PAD:c64, :] + ob_s[DN_PAD:c64, :]
    y_ref[0, n_grid:seq_len, :] = emit(o0, z_ref[0, n_grid:seq_len, :])

    def out_step(c, carry):
        r0 = pl.multiple_of(c * c64, c64)
        l0 = pl.multiple_of((c - 1) * c64, c64)
        o = of_s[pl.ds(r0, c64), :] + ob_s[pl.ds(r0, c64), :]
        y_ref[0, pl.ds(l0, c64), :] = emit(o, z_ref[0, pl.ds(l0, c64), :])
        return carry

    lax.fori_loop(1, n_chunks, out_step, 0)


def _deltanet(proj3, ba3, conv_w, gpar, out_norm):
    b, l, _ = proj3.shape
    nh = DN_NH
    wd = nh * DN_HEAD_DIM
    n_rows = l + DN_PAD
    n_chunks = n_rows // DN_CHUNK
    ng = DN_NG
    blk = lambda col0: pl.BlockSpec((1, l, wd), lambda i, g, col0=col0: (i, 0, col0 // nh + g))
    cblk = lambda part: pl.BlockSpec((DN_CONV, wd), lambda i, g, part=part: (0, part * ng + g))
    return pl.pallas_call(
        functools.partial(_dn_body, seq_len=l),
        grid=(b, ng),
        in_specs=[
            blk(COL_DQ), blk(COL_DK), blk(COL_DV), blk(COL_DZ),
            pl.BlockSpec((1, l, LANES), lambda i, g: (i, 0, g)),
            cblk(0), cblk(1), cblk(2),
            pl.BlockSpec((1, 2, LANES), lambda i, g: (g, 0, 0)),
            pl.BlockSpec((1, LANES), lambda i, g: (0, 0)),
        ],
        out_specs=pl.BlockSpec((1, l, wd), lambda i, g: (i, 0, g)),
        out_shape=jax.ShapeDtypeStruct((b, l, DN_WIDTH), BF16),
        scratch_shapes=[
            pltpu.VMEM((n_rows + 16, wd), F32), pltpu.VMEM((n_rows + 16, wd), F32),
            pltpu.VMEM((n_rows + 16, wd), F32),
            pltpu.VMEM((n_rows, wd), F32), pltpu.VMEM((n_rows, wd), F32), pltpu.VMEM((n_rows, wd), F32),
            pltpu.VMEM((n_rows, LANES), F32), pltpu.VMEM((n_rows, LANES), F32),
            pltpu.VMEM((nh, 2 * n_rows, LANES), BF16), pltpu.VMEM((nh, 2 * n_rows, LANES), BF16),
            pltpu.VMEM((nh, 2 * n_rows, LANES), BF16),
            pltpu.VMEM((nh, 2 * n_rows, LANES), BF16),
            pltpu.VMEM((nh, 2 * n_rows, LANES), BF16),
            pltpu.VMEM((nh, n_chunks * 16, LANES), F32),
            pltpu.VMEM((n_rows, wd), F32), pltpu.VMEM((n_rows, wd), F32),
            pltpu.VMEM((DN_NCH, LANES, LANES), F32),
        ],
        compiler_params=pltpu.CompilerParams(
            dimension_semantics=("arbitrary", "arbitrary"), vmem_limit_bytes=VMEM_LIMIT),
        name="deltanet",
    )(proj3, proj3, proj3, proj3, ba3, conv_w, conv_w, conv_w, gpar, out_norm)


def _na_body(q_ref, k_ref, v_ref, tb_ref, o_ref, *, rows):
    scale = NA_HEAD_DIM ** -0.5
    wr = NA_WIN_R
    nk = wr * GRID_W
    n_grid = rows * GRID_W
    qm = q_ref[0, n_grid:n_grid + N_META, :]
    km = k_ref[0, n_grid:n_grid + N_META, :]
    vm = v_ref[0, n_grid:n_grid + N_META, :]

    s = _dot_nt(qm, km) * scale
    p = jnp.exp(s - jnp.max(s, axis=-1, keepdims=True))
    o = _dot(p.astype(BF16), vm) / jnp.sum(p, axis=-1, keepdims=True)
    o_ref[0, n_grid:n_grid + N_META, :] = o.astype(BF16)

    pad = jnp.zeros((NA_KEYS - nk - N_META, LANES), BF16)

    def row_group_step(grp, carry):
        qs, ks, vs, biases, dsts = [], [], [], [], []
        for j in range(NA_ROW_GROUP):
            r = grp * NA_ROW_GROUP + j
            rs = jnp.clip(r - wr // 2, 0, rows - wr)
            dsts.append(pl.ds(pl.multiple_of(r * GRID_W, GRID_W), GRID_W))
            win = pl.ds(pl.multiple_of(rs * GRID_W, GRID_W), nk)
            qs.append(q_ref[0, dsts[j], :])
            ks.append(jnp.concatenate([k_ref[0, win, :], km, pad], axis=0))
            vs.append(jnp.concatenate([v_ref[0, win, :], vm, pad], axis=0))
            biases.append(tb_ref[0, rs - r + NA_WIN_R - 1])
        ss = [_dot_nt(qs[j], ks[j]) * scale + biases[j] for j in range(NA_ROW_GROUP)]
        ps = [jnp.exp(s - jnp.max(s, axis=-1, keepdims=True)) for s in ss]
        os_ = [_dot(ps[j].astype(BF16), vs[j]) / jnp.sum(ps[j], axis=-1, keepdims=True) for j in range(NA_ROW_GROUP)]
        for j in range(NA_ROW_GROUP):
            o_ref[0, dsts[j], :] = os_[j].astype(BF16)
        return carry

    assert rows % NA_ROW_GROUP == 0
    lax.fori_loop(0, rows // NA_ROW_GROUP, row_group_step, 0)


def _na_bias_table(rpb):
    wr, wc = NA_WIN_R, NA_WIN_C
    qc = np.arange(GRID_W)[:, None]
    kc = np.arange(GRID_W)[None, :]
    q_start = np.clip(qc - wc // 2, 0, GRID_W - wc)
    valid = (kc - q_start >= 0) & (kc - q_start < wc)
    dc = np.clip(kc - qc + wc - 1, 0, 2 * wc - 2)
    i0 = np.arange(wr)[:, None]
    jj = np.arange(wr)[None, :]
    dr = i0 + jj
    tb = rpb.astype(F32)[:, dr][:, :, :, dc]
    tb = jnp.where(valid[None, None, None], tb, -jnp.inf)
    tb = tb.transpose(0, 1, 3, 2, 4)
    tb = tb.reshape(NA_HEADS, wr, GRID_W, wr * GRID_W)
    meta = jnp.zeros(tb.shape[:3] + (N_META,), F32)
    fill = jnp.full(tb.shape[:3] + (NA_KEYS - wr * GRID_W - N_META,), -jnp.inf, F32)
    return jnp.concatenate([tb, meta, fill], axis=-1)


def _natten(proj3, tb):
    b, l, _ = proj3.shape
    rows = (l - N_META) // GRID_W
    assert rows >= NA_WIN_R
    blk = lambda col0: pl.BlockSpec((1, l, LANES), lambda h, i, col0=col0: (i, 0, col0 + h))
    return pl.pallas_call(
        functools.partial(_na_body, rows=rows),
        grid=(NA_HEADS, b),
        in_specs=[
            blk(COL_NQ), blk(COL_NK), blk(COL_NV),
            pl.BlockSpec((1, NA_WIN_R, GRID_W, NA_KEYS), lambda h, i: (h, 0, 0, 0)),
        ],
        out_specs=pl.BlockSpec((1, l, LANES), lambda h, i: (i, 0, h)),
        out_shape=jax.ShapeDtypeStruct((b, l, NA_WIDTH), BF16),
        compiler_params=pltpu.CompilerParams(
            dimension_semantics=("arbitrary", "arbitrary"), vmem_limit_bytes=VMEM_LIMIT),
        name="natten",
    )(proj3, proj3, proj3, tb)


def _mix_body(ydn_ref, yna_ref, ga_ref, gb_ref, h_ref, wdn_ref, wna_ref, wo_ref, gn_ref, wr_ref, br_ref,
              h2_ref, u2_ref, gate_ref, idx_ref):
    a = _dot(ydn_ref[...], wdn_ref[...])
    b = _dot(yna_ref[...], wna_ref[...])
    mix = _sigmoid(ga_ref[...].astype(F32)) * a + _sigmoid(gb_ref[...].astype(F32)) * b
    h2 = h_ref[...] + _dot(mix.astype(BF16), wo_ref[...])
    h2_ref[...] = h2
    u2 = h2 * lax.rsqrt(jnp.mean(h2 * h2, axis=-1, keepdims=True) + NORM_EPS) * gn_ref[...]
    _store_token_tiles(u2_ref, _pack_halves(u2), u2.shape[0])
    u_hi = u2.astype(BF16)
    u_lo = (u2 - u_hi.astype(F32)).astype(BF16)
    w_r = wr_ref[...]
    w_hi = w_r.astype(BF16)
    w_lo = (w_r - w_hi.astype(F32)).astype(BF16)
    logits = _dot(u_hi, w_hi) + (_dot(u_hi, w_lo) + _dot(u_lo, w_hi)) + br_ref[...]
    lane = lax.broadcasted_iota(I32, logits.shape, 1)
    logits = jnp.where(lane < N_EXPERTS, logits, -jnp.inf)
    vals, idxs = [], []
    for _ in range(TOP_K):
        m = jnp.max(logits, axis=-1, keepdims=True)
        sel = jnp.min(jnp.where(logits == m, lane, LANES), axis=-1, keepdims=True)
        vals.append(m)
        idxs.append(sel)
        logits = jnp.where(lane == sel, -jnp.inf, logits)
    es = [jnp.exp(v - vals[0]) for v in vals]
    den = es[0] + es[1] + es[2] + es[3]
    gates = jnp.zeros(logits.shape, F32)
    idx = jnp.zeros(logits.shape, I32)
    for k in range(TOP_K):
        gates = jnp.where(lane == k, es[k] / den, gates)
        idx = jnp.where(lane == k, idxs[k], idx)
    gate_ref[...] = gates
    idx_ref[...] = idx


def _mix(y_dn, y_na, proj, h, w_dn, w_na, w_o, gain, w_r, b_r, tm):
    t, d = h.shape
    row = lambda w: pl.BlockSpec((tm, w), lambda i: (i, 0))
    full = lambda a: pl.BlockSpec(a.shape, lambda i: (0,) * a.ndim)
    return pl.pallas_call(
        _mix_body,
        grid=(t // tm,),
        in_specs=[
            row(DN_WIDTH), row(NA_WIDTH),
            pl.BlockSpec((tm, d), lambda i: (i, COL_GA * LANES // d)),
            pl.BlockSpec((tm, d), lambda i: (i, COL_GB * LANES // d)),
            row(d), full(w_dn), full(w_na), full(w_o), full(gain), full(w_r), full(b_r),
        ],
        out_specs=[row(d), pl.BlockSpec((tm * SUBLANES, LANES), lambda i: (i, 0)), row(LANES), row(LANES)],
        out_shape=[
            jax.ShapeDtypeStruct((t, d), F32), jax.ShapeDtypeStruct((t * SUBLANES, LANES), U32),
            jax.ShapeDtypeStruct((t, LANES), F32), jax.ShapeDtypeStruct((t, LANES), I32),
        ],
        compiler_params=pltpu.CompilerParams(
            dimension_semantics=("arbitrary",), vmem_limit_bytes=VMEM_LIMIT),
        name="mix",
    )(y_dn, y_na, proj, proj, h, w_dn, w_na, w_o, gain, w_r, b_r)


def _moe_body(blk_exp, n_valid, idx_hbm, u_hbm, wg_ref, wl_ref, bg_ref, bl_ref, wo_ref, bo_ref, y_hbm,
              idx_s, xbuf, xs, acc, stage, isem, gsem, ssem, *, n_blocks, n_asg):
    i = pl.program_id(0)
    f = pl.program_id(1)
    slot = i % 2
    nxt = (i + 1) % 2
    tm = MOE_TM

    def idx_copy(blk, sl):
        return pltpu.make_async_copy(idx_hbm.at[pl.ds(pl.multiple_of(blk * (2 * tm), 2 * tm), 2 * tm)],
                                     idx_s.at[pl.ds(pl.multiple_of(sl * (2 * tm), 2 * tm), 2 * tm)], isem.at[sl])

    def tile(ref, row0):
        return ref.at[pl.ds(pl.multiple_of(row0, SUBLANES), SUBLANES), :]

    def gather_rows(sl):
        def body(r, carry):
            src = idx_s[sl * (2 * tm) + r]
            pltpu.make_async_copy(tile(u_hbm, src), tile(xbuf.at[sl], r * SUBLANES), gsem.at[sl]).start()
            return carry
        lax.fori_loop(0, tm, body, 0, unroll=DMA_ISSUE_UNROLL)

    def gather_wait(sl):
        pltpu.make_async_copy(xbuf.at[sl], xbuf.at[sl], gsem.at[sl]).wait()

    def scatter_rows(sl):
        def body(r, carry):
            dst = idx_s[sl * (2 * tm) + tm + r]
            pltpu.make_async_copy(tile(stage, r * SUBLANES), tile(y_hbm, dst), ssem).start()
            return carry
        lax.fori_loop(0, tm, body, 0, unroll=DMA_ISSUE_UNROLL)

    def scatter_wait():
        pltpu.make_async_copy(stage, stage, ssem).wait()

    live = n_valid[i] > 0

    @pl.when((i == 0) & (f == 0))
    def _():
        stage[...] = jnp.zeros(stage.shape, U32)
        n_stage = tm * SUBLANES
        tail0, tail1 = n_asg * SUBLANES, y_hbm.shape[0]
        fills = [pltpu.make_async_copy(stage.at[pl.ds(0, min(n_stage, tail1 - s0)), :],
                                       y_hbm.at[pl.ds(s0, min(n_stage, tail1 - s0)), :], ssem)
                 for s0 in range(tail0, tail1, n_stage)]
        for fill in fills:
            fill.start()
        for fill in fills:
            fill.wait()
        idx_copy(0, 0).start()
        idx_copy(0, 0).wait()

        @pl.when(live)
        def _():
            gather_rows(0)

    @pl.when((f == 0) & live)
    def _():
        gather_wait(slot)
        for s in range(SUBLANES):
            lo, hi = _unpack_halves(xbuf[slot, pl.ds(s, tm, stride=SUBLANES), :])
            xs[:, s * LANES:(s + 1) * LANES] = lo.astype(BF16)
            xs[:, HALF + s * LANES:HALF + (s + 1) * LANES] = hi.astype(BF16)

    @pl.when((f == 0) & (i + 1 < n_blocks))
    def _():
        idx_copy(i + 1, nxt).start()

    @pl.when((f == 1) & (i + 1 < n_blocks))
    def _():
        idx_copy(i + 1, nxt).wait()

        @pl.when(n_valid[jnp.minimum(i + 1, n_blocks - 1)] > 0)
        def _():
            gather_rows(nxt)

    @pl.when(live)
    def _():
        x = xs[...]
        n_sub = MOE_TF // MOE_SUB
        pre = []
        for s in range(n_sub):
            cols = slice(s * MOE_SUB, (s + 1) * MOE_SUB)
            pre.append((_dot(x, wg_ref[0, :, cols]) + bg_ref[0][:, cols],
                        _dot(x, wl_ref[0, :, cols]) + bl_ref[0][:, cols]))
        part = None
        for s in range(n_sub):
            hg, hl = pre[s]
            glu = jnp.minimum(hg, SWIGLU_LIMIT)
            lin = jnp.clip(hl, -SWIGLU_LIMIT, SWIGLU_LIMIT)
            act = glu * _sigmoid(SWIGLU_ALPHA * glu) * (lin + 1.0)
            out = _dot(act.astype(BF16), wo_ref[0, s * MOE_SUB:(s + 1) * MOE_SUB, :])
            part = out if part is None else part + out

        @pl.when(f == 0)
        def _():
            acc[...] = part

        @pl.when(f > 0)
        def _():
            acc[...] += part

    @pl.when(f == MOE_NF - 1)
    def _():
        @pl.when((i > 0) & (n_valid[jnp.maximum(i - 1, 0)] > 0))
        def _():
            scatter_wait()

        @pl.when(live)
        def _():
            _store_token_tiles(stage, _pack_halves(acc[...] + bo_ref[0]), tm)
            scatter_rows(slot)

        @pl.when((i == n_blocks - 1) & live)
        def _():
            scatter_wait()


def _moe(blk_exp, n_valid, idx_blocks, u2p, w_in, b_in, w_out, b_out, n_asg, n_rows_out):
    n_blocks = blk_exp.shape[0]
    d = D_MODEL
    nf = MOE_NF

    def wmap(off):
        def index_map(i, f, be, nv):
            fe = jnp.where(nv[i] > 0, f, nf - 1)
            return (be[i], 0, fe + off)
        return index_map

    def womap(i, f, be, nv):
        return (be[i], jnp.where(nv[i] > 0, f, nf - 1), 0)

    grid_spec = pltpu.PrefetchScalarGridSpec(
        num_scalar_prefetch=2,
        grid=(n_blocks, nf),
        in_specs=[
            pl.BlockSpec(memory_space=pl.ANY),
            pl.BlockSpec(memory_space=pl.ANY),
            pl.BlockSpec((1, d, MOE_TF), wmap(0)),
            pl.BlockSpec((1, d, MOE_TF), wmap(nf)),
            pl.BlockSpec((1, 1, MOE_TF), wmap(0)),
            pl.BlockSpec((1, 1, MOE_TF), wmap(nf)),
            pl.BlockSpec((1, MOE_TF, d), womap),
            pl.BlockSpec((1, 1, d), lambda i, f, be, nv: (be[i], 0, 0)),
        ],
        out_specs=pl.BlockSpec(memory_space=pl.ANY),
        scratch_shapes=[
            pltpu.SMEM((2 * 2 * MOE_TM,), I32),
            pltpu.VMEM((2, MOE_TM * SUBLANES, LANES), U32),
            pltpu.VMEM((MOE_TM, d), BF16),
            pltpu.VMEM((MOE_TM, d), F32),
            pltpu.VMEM((MOE_TM * SUBLANES, LANES), U32),
            pltpu.SemaphoreType.DMA((2,)),
            pltpu.SemaphoreType.DMA((2,)),
            pltpu.SemaphoreType.DMA,
        ],
    )
    return pl.pallas_call(
        functools.partial(_moe_body, n_blocks=n_blocks, n_asg=n_asg),
        grid_spec=grid_spec,
        out_shape=jax.ShapeDtypeStruct((n_rows_out * SUBLANES, LANES), U32),
        compiler_params=pltpu.CompilerParams(
            dimension_semantics=("arbitrary", "arbitrary"), vmem_limit_bytes=VMEM_LIMIT),
        name="moe",
    )(blk_exp, n_valid, idx_blocks, u2p, w_in, w_in, b_in, b_in, w_out, b_out)


def _routing_metadata(top_idx, n_blocks):
    tm = MOE_TM
    e = top_idx[:, :TOP_K].reshape(-1)
    n_asg = e.shape[0]
    order = jnp.argsort(e, stable=True).astype(I32)
    counts = jnp.sum((e[:, None] == jnp.arange(N_EXPERTS, dtype=I32)[None, :]).astype(I32), axis=0)
    cstart = jnp.cumsum(counts) - counts
    nblk = (counts + tm - 1) // tm
    bend = jnp.cumsum(nblk)
    bstart = bend - nblk
    blk = jnp.arange(n_blocks, dtype=I32)
    blk_exp = jnp.minimum(jnp.sum((bend[None, :] <= blk[:, None]).astype(I32), axis=1), N_EXPERTS - 1)
    first = (blk - bstart[blk_exp]) * tm
    n_valid = jnp.where(blk < bend[-1], jnp.clip(counts[blk_exp] - first, 0, tm), 0).astype(I32)
    r = jnp.arange(tm, dtype=I32)[None, :]
    valid = r < n_valid[:, None]
    asg = order[jnp.where(valid, (cstart[blk_exp] + first)[:, None] + r, 0)]
    src_row = (asg // TOP_K) * SUBLANES
    dst_row = jnp.where(valid, asg, n_asg + r) * SUBLANES
    idx_blocks = jnp.concatenate([src_row, dst_row], axis=1).reshape(-1)
    return blk_exp, n_valid, idx_blocks


def _final_body(h_ref, y_ref, gate_ref, gn_ref, o_ref):
    h = h_ref[0]
    gates = gate_ref[0]
    y_ref = y_ref.at[0]
    lo_acc = jnp.zeros((h.shape[0], HALF), F32)
    hi_acc = jnp.zeros((h.shape[0], HALF), F32)
    for k in range(TOP_K):
        lo, hi = _unpack_halves(_load_token_tiles(y_ref, h.shape[0], k * SUBLANES, TOP_K * SUBLANES))
        g = gates[:, k:k + 1]
        lo_acc = lo_acc + g * lo
        hi_acc = hi_acc + g * hi
    y = h + jnp.concatenate([lo_acc, hi_acc], axis=1)
    o_ref[0] = y * lax.rsqrt(jnp.mean(y * y, axis=-1, keepdims=True) + NORM_EPS) * gn_ref[...]


def _final(h2, y4, gates, gain, n_out, tm):
    b, _, d = h2.shape
    return pl.pallas_call(
        _final_body,
        grid=(b, n_out // tm),
        in_specs=[
            pl.BlockSpec((1, tm, d), lambda i, j: (i, j, 0)),
            pl.BlockSpec((1, tm * TOP_K * SUBLANES, LANES), lambda i, j: (i, j, 0)),
            pl.BlockSpec((1, tm, LANES), lambda i, j: (i, j, 0)),
            pl.BlockSpec((1, d), lambda i, j: (0, 0)),
        ],
        out_specs=pl.BlockSpec((1, tm, d), lambda i, j: (i, j, 0)),
        out_shape=jax.ShapeDtypeStruct((b, n_out, d), F32),
        compiler_params=pltpu.CompilerParams(
            dimension_semantics=("arbitrary", "arbitrary"), vmem_limit_bytes=VMEM_LIMIT),
        name="final",
    )(h2, y4, gates, gain)


def _split_w_in(w_in):
    n_dn = 4 * DN_WIDTH
    n_na = n_dn + 4 * DN_HEADS
    w_main = jnp.concatenate(
        [w_in[:, :n_dn], w_in[:, n_na + 3 * NA_WIDTH:], w_in[:, n_na:n_na + 3 * NA_WIDTH]], axis=1).astype(BF16)
    wb = w_in[:, n_dn:n_dn + 2 * DN_HEADS]
    wa = w_in[:, n_dn + 2 * DN_HEADS:n_dn + 4 * DN_HEADS]
    return w_main, _group_lanes(wb, wa).astype(BF16)


def _group_lanes(b_part, a_part):
    nh, ng = DN_NH, DN_NG
    lead = b_part.shape[:-1]
    bp = b_part.reshape(lead + (2, ng, nh))
    ap = a_part.reshape(lead + (2, ng, nh))
    perm = tuple(range(len(lead))) + (len(lead) + 1, len(lead), len(lead) + 2)
    bp = bp.transpose(perm).reshape(lead + (ng, 2 * nh))
    ap = ap.transpose(perm).reshape(lead + (ng, 2 * nh))
    both = jnp.concatenate([bp, ap], axis=-1)
    pad = [(0, 0)] * (both.ndim - 1) + [(0, LANES - 4 * nh)]
    return jnp.pad(both, pad).reshape(lead + (ng * LANES,))


def kernel(x, meta_tokens, norm_mix, w_in, dn_conv, dn_a_log, dn_dt_bias, dn_out_norm, na_rpb,
           w_branch_dn, w_branch_na, w_out, norm_ffn, w_router, b_router, w_exp_in, b_exp_in,
           w_exp_out, b_exp_out, norm_final):
    bsz, seq, d = x.shape
    l = seq + N_META
    t = bsz * l
    meta = jnp.broadcast_to(meta_tokens[None].astype(x.dtype), (bsz, N_META, d))
    h = jnp.concatenate([x, meta], axis=1).reshape(t, d)
    depth = w_in.shape[0]
    assert depth == 1, "the final kernel fuses the last residual add with the final RMSNorm"
    tm_proj = 1376 if t % 1376 == 0 else 16 * (l // 16)
    tm_mix = 384 if t % 384 == 0 else 8 * (l // 8)
    tm_final = 512 if seq % 512 == 0 else seq
    n_blocks = t * TOP_K // MOE_TM + N_EXPERTS
    for layer in range(depth):
        w_main, w_ba = _split_w_in(w_in[layer])
        proj, ba = _inproj(h, norm_mix[layer][None, :], w_main, w_ba, tm_proj, 1024)
        proj3 = proj.reshape(bsz, l, N_MAIN_COLS)
        ba3 = ba.reshape(bsz, l, DN_NG * LANES)
        gpar = jnp.stack([_group_lanes(jnp.zeros_like(dn_a_log[layer]).reshape(-1), dn_a_log[layer].reshape(-1)),
                          _group_lanes(jnp.zeros_like(dn_dt_bias[layer]).reshape(-1),
                                       dn_dt_bias[layer].reshape(-1))], axis=0)
        gpar = gpar.reshape(2, DN_NG, LANES).transpose(1, 0, 2)
        y_dn = _deltanet(proj3, ba3, dn_conv[layer], gpar, dn_out_norm[layer][None, :])
        y_na = _natten(proj3, _na_bias_table(na_rpb[layer]))
        w_r = jnp.pad(w_router[layer], ((0, 0), (0, LANES - N_EXPERTS)))
        b_r = jnp.pad(b_router[layer], (0, LANES - N_EXPERTS))[None, :]
        h2, u2p, gates, top_idx = _mix(
            y_dn.reshape(t, DN_WIDTH), y_na.reshape(t, NA_WIDTH), proj, h,
            w_branch_dn[layer].astype(BF16), w_branch_na[layer].astype(BF16), w_out[layer].astype(BF16),
            norm_ffn[layer][None, :], w_r, b_r, tm_mix)
        blk_exp, n_valid, idx_blocks = _routing_metadata(top_idx, n_blocks)
        y4 = _moe(blk_exp, n_valid, idx_blocks, u2p,
                  w_exp_in[layer].astype(BF16), b_exp_in[layer][:, None, :],
                  w_exp_out[layer].astype(BF16), b_exp_out[layer][:, None, :], t * TOP_K, (bsz + 1) * l * TOP_K)
        out = _final(h2.reshape(bsz, l, d), y4.reshape(bsz + 1, l * TOP_K * SUBLANES, LANES),
                     gates.reshape(bsz, l, LANES), norm_final[None, :], seq, tm_final)
    return out
```

```python
import functools

import jax
import jax.numpy as jnp
import numpy as np
from jax import lax
from jax.experimental import pallas as pl
from jax.experimental.pallas import tpu as pltpu

F32 = jnp.float32
BF16 = jnp.bfloat16
I32 = jnp.int32
U32 = jnp.uint32

D_MODEL = 2048
N_META = 16
GRID_W = 64
DN_HEADS = 8
DN_HEAD_DIM = 128
DN_WIDTH = 1024
DN_CONV = 5
DN_CHUNK = 64
NA_HEADS = 8
NA_HEAD_DIM = 128
NA_WIDTH = 1024
NA_WIN_R = 8
NA_WIN_C = 16
N_EXPERTS = 32
TOP_K = 4
D_EXPERT = 2048
SWIGLU_ALPHA = 1.702
SWIGLU_LIMIT = 7.0
NORM_EPS = 1e-6

LANES = 128
NORM_ROWS = 16
VMEM_LIMIT = 56 * 1024 * 1024

COL_DQ, COL_DK, COL_DV, COL_DZ = 0, 8, 16, 24
COL_GA, COL_GB = 32, 48
COL_NQ, COL_NK, COL_NV = 64, 72, 80
N_MAIN_COLS = 88 * LANES

DN_NH = 2
DN_NG = DN_HEADS // DN_NH
DN_NCH = 2 * DN_NH
DN_PAD = (-N_META) % DN_CHUNK
DN_CHUNK_UNROLL = 3

DMA_ISSUE_UNROLL = 8

NA_ROW_GROUP = 4
NA_KEYS = 640

MOE_TM = 512
MOE_TF = 1024
MOE_SUB = 256
MOE_NF = D_EXPERT // MOE_TF
HALF = D_MODEL // 2
SUBLANES = 8
assert HALF == SUBLANES * LANES


def _store_token_tiles(ref, packed, n_tok, base=0, group=SUBLANES):
    for s in range(SUBLANES):
        ref[pl.ds(base + s, n_tok, stride=group), :] = packed[:, s * LANES:(s + 1) * LANES]


def _load_token_tiles(ref, n_tok, base=0, group=SUBLANES):
    return jnp.concatenate([ref[pl.ds(base + s, n_tok, stride=group), :] for s in range(SUBLANES)], axis=1)


def _dot(a, b, **kw):
    return jnp.dot(a, b, preferred_element_type=F32, **kw)


def _dot_nt(a, b, **kw):
    return lax.dot_general(a, b, (((1,), (1,)), ((), ())), preferred_element_type=F32, **kw)


def _dot_tn(a, b, **kw):
    return lax.dot_general(a, b, (((0,), (0,)), ((), ())), preferred_element_type=F32, **kw)


def _sigmoid(x):
    return 1.0 / (1.0 + jnp.exp(-x))


def _pack_halves(x):
    n = x.shape[1] // 2
    lo = pltpu.bitcast(x[:, :n].astype(BF16).astype(F32), U32)
    hi = pltpu.bitcast(x[:, n:].astype(BF16).astype(F32), U32)
    return (hi & jnp.uint32(0xFFFF0000)) | (lo >> 16)


def _unpack_halves(p):
    lo = pltpu.bitcast(p << 16, F32)
    hi = pltpu.bitcast(p & jnp.uint32(0xFFFF0000), F32)
    return lo, hi


def _inproj_body(h_ref, g_ref, w_ref, wba_ref, o_ref, oba_ref, u_ref):
    @pl.when(pl.program_id(1) == 0)
    def _():
        gain = g_ref[...]

        def norm_rows(c, carry):
            rows = pl.ds(pl.multiple_of(c * NORM_ROWS, NORM_ROWS), NORM_ROWS)
            x = h_ref[rows, :]
            y = x * lax.rsqrt(jnp.mean(x * x, axis=-1, keepdims=True) + NORM_EPS) * gain
            u_ref[rows, :] = y.astype(BF16)
            return carry

        lax.fori_loop(0, h_ref.shape[0] // NORM_ROWS, norm_rows, 0)
        oba_ref[...] = _dot(u_ref[...], wba_ref[...])

    o_ref[...] = _dot(u_ref[...], w_ref[...]).astype(BF16)


def _inproj(h, gain, w_main, w_ba, tm, tn):
    t, d = h.shape
    n = w_main.shape[1]
    nba = w_ba.shape[1]
    return pl.pallas_call(
        _inproj_body,
        grid=(t // tm, n // tn),
        in_specs=[
            pl.BlockSpec((tm, d), lambda i, j: (i, 0)),
            pl.BlockSpec((1, d), lambda i, j: (0, 0)),
            pl.BlockSpec((d, tn), lambda i, j: (0, j)),
            pl.BlockSpec((d, nba), lambda i, j: (0, 0)),
        ],
        out_specs=[
            pl.BlockSpec((tm, tn), lambda i, j: (i, j)),
            pl.BlockSpec((tm, nba), lambda i, j: (i, 0)),
        ],
        out_shape=[jax.ShapeDtypeStruct((t, n), BF16), jax.ShapeDtypeStruct((t, nba), F32)],
        scratch_shapes=[pltpu.VMEM((tm, d), BF16)],
        compiler_params=pltpu.CompilerParams(
            dimension_semantics=("arbitrary", "arbitrary"), vmem_limit_bytes=VMEM_LIMIT),
        name="inproj",
    )(h, gain, w_main, w_ba)


def _pair_dot(lhs, rhs):
    w = rhs[0].shape[1]
    r0, r1 = rhs[0].astype(BF16), rhs[1].astype(BF16)
    zero = jnp.zeros(r0.shape, BF16)
    bd = jnp.concatenate([jnp.concatenate([r0, zero], axis=1), jnp.concatenate([zero, r1], axis=1)], axis=0)
    x = _dot(jnp.concatenate([lhs[0].astype(BF16), lhs[1].astype(BF16)], axis=1), bd)
    return x[:, :w], x[:, w:]


def _pair_neumann_inverse(a_pairs, nilpotency):
    c = a_pairs[0][0].shape[0]
    eye = (lax.broadcasted_iota(I32, (c, c), 0) == lax.broadcasted_iota(I32, (c, c), 1)).astype(F32)
    ns = [[-a for a in pair] for pair in a_pairs]
    ps = [[eye + x for x in n] for n in ns]
    ns = [list(_pair_dot(n, n)) for n in ns]
    steps = int(np.log2(nilpotency)) - 1
    for step in range(steps):
        if step < steps - 1:
            pns = [_pair_dot([jnp.concatenate([p[i], n[i]], axis=0) for i in range(2)], n) for p, n in zip(ps, ns)]
            ps = [[p[i] + pn[i][:c] for i in range(2)] for p, pn in zip(ps, pns)]
            ns = [[pn[i][c:] for i in range(2)] for pn in pns]
        else:
            pns = [_pair_dot(p, n) for p, n in zip(ps, ns)]
            ps = [[p[i] + pn[i] for i in range(2)] for p, pn in zip(ps, pns)]
    return ps


def _chunk_cumsum(x, half):
    rows = x.shape[0]
    r = lax.broadcasted_iota(I32, x.shape, 0)
    top = r < half
    s = 1
    while s < half:
        down = pltpu.roll(x, s, axis=0)
        up = pltpu.roll(x, rows - s, axis=0)
        x = x + jnp.where(top, jnp.where(r >= s, down, 0.0), jnp.where(r < rows - s, up, 0.0))
        s *= 2
    return x


def _dn_body(q_ref, k_ref, v_ref, z_ref, ba_ref, cq_ref, ck_ref, cv_ref, gp_ref, on_ref, y_ref,
             xq, xk, xv, qn, kn, vn, gbf, gbb, us, ws, qds, kdts, qks, cds, of_s, ob_s, s_s, *, seq_len):
    nh = DN_NH
    wd = nh * DN_HEAD_DIM
    c64 = DN_CHUNK
    n_chunks = (seq_len + DN_PAD) // c64
    n_rows = n_chunks * c64
    lead = 8 + DN_PAD

    for x in (xq, xk, xv):
        x[0:lead, :] = jnp.zeros((lead, wd), F32)
        x[lead + seq_len:lead + seq_len + 8, :] = jnp.zeros((8, wd), F32)
    gbf[0:DN_PAD, :] = jnp.zeros((DN_PAD, LANES), F32)
    gbb[0:DN_PAD, :] = jnp.zeros((DN_PAD, LANES), F32)

    neg_a = -jnp.exp(gp_ref[0, 0:1, :])
    dt_b = gp_ref[0, 1:2, :]

    def copy_rows(src0, pos0, n):
        src = pl.ds(src0, n)
        dst = pl.ds(pl.multiple_of(8 + pos0, 8), n)
        xq[dst, :] = q_ref[0, src, :].astype(F32)
        xk[dst, :] = k_ref[0, src, :].astype(F32)
        xv[dst, :] = v_ref[0, src, :].astype(F32)
        ba = ba_ref[0, src, :]
        lane = lax.broadcasted_iota(I32, ba.shape, 1)
        beta = _sigmoid(ba)
        sp_in = ba + dt_b
        softplus = jnp.maximum(sp_in, 0.0) + jnp.log(1.0 + jnp.exp(-jnp.abs(sp_in)))
        val = jnp.where(lane < 2 * nh, beta, jnp.where(lane < 4 * nh, neg_a * softplus, 0.0))
        dst_g = pl.ds(pl.multiple_of(pos0, 8), n)
        gbf[dst_g, :] = val
        gbb[dst_g, :] = pltpu.roll(val, LANES - nh, axis=1)

    n_grid = seq_len - N_META
    copy_rows(n_grid, DN_PAD, N_META)

    def copy_step(j, carry):
        copy_rows(pl.multiple_of(j * c64, c64), DN_PAD + N_META + j * c64, c64)
        return carry

    lax.fori_loop(0, n_grid // c64, copy_step, 0)

    row_iota = lax.broadcasted_iota(I32, (c64, wd), 0)

    def conv_step(c, carry):
        r0 = pl.multiple_of(c * c64, c64)
        live = (row_iota + r0) >= DN_PAD
        for x, cw, dst, kind in ((xq, cq_ref, qn, "q"), (xk, ck_ref, kn, "k"), (xv, cv_ref, vn, "v")):
            win = x[pl.ds(r0, c64 + 16), :]
            acc = jnp.zeros((c64, wd), F32)
            for j in range(DN_CONV):
                off = 8 - DN_CONV // 2 + j
                acc = acc + win[off:off + c64, :] * cw[j:j + 1, :]
            y = jnp.where(live, acc * _sigmoid(acc), 0.0)
            if kind == "v":
                dst[pl.ds(r0, c64), :] = y
            else:
                scale = DN_HEAD_DIM ** -0.5 if kind == "q" else 1.0
                for hh in range(nh):
                    yh = y[:, hh * LANES:(hh + 1) * LANES]
                    yn = yh * lax.rsqrt(jnp.sum(yh * yh, axis=-1, keepdims=True) + 1e-6)
                    dst[pl.ds(r0, c64), hh * LANES:(hh + 1) * LANES] = yn * scale
        return carry

    lax.fori_loop(0, n_chunks, conv_step, 0)

    c2 = 2 * c64
    ri = lax.broadcasted_iota(I32, (c2, c2), 0)
    ci = lax.broadcasted_iota(I32, (c2, c2), 1)
    fwd_blk = (ri < c64) & (ci < c64)
    bwd_blk = (ri >= c64) & (ci >= c64)
    incl = (fwd_blk & (ri >= ci)) | (bwd_blk & (ri <= ci))
    strict = (fwd_blk & (ri > ci)) | (bwd_blk & (ri < ci))
    lane_row = lax.broadcasted_iota(I32, (1, c2), 1)
    lane_blk = lax.broadcasted_iota(I32, (c2, c2), 1)

    def chunk_group_step(grp, carry):
        chunks = [grp * DN_CHUNK_UNROLL + j for j in range(DN_CHUNK_UNROLL)]
        r0s = [pl.multiple_of(c * c64, c64) for c in chunks]
        b0s = [pl.multiple_of(c * c2, c2) for c in chunks]
        gb2s, gam2s, gam2_ts, q2s, k2s, v2s, k2_ts, lhs_gs = [], [], [], [], [], [], [], []
        for r0 in r0s:
            gb2 = jnp.concatenate([gbf[pl.ds(r0, c64), :], gbb[pl.ds(r0, c64), :]], axis=0)
            gam2 = _chunk_cumsum(gb2, c64)
            gb2s.append(gb2)
            gam2s.append(gam2)
            gam2_ts.append(gam2.T)
            q2, k2, v2 = [], [], []
            for hh in range(nh):
                hs = slice(hh * LANES, (hh + 1) * LANES)
                q = qn[pl.ds(r0, c64), hs]
                k = kn[pl.ds(r0, c64), hs]
                v = vn[pl.ds(r0, c64), hs]
                q2.append(jnp.concatenate([q, q], axis=0))
                k2.append(jnp.concatenate([k, k], axis=0))
                v2.append(jnp.concatenate([v, v], axis=0))
            q2s.append(q2)
            k2s.append(k2)
            v2s.append(v2)
            k2_ts.append([x.T for x in k2])
            lhs_gs.append([jnp.concatenate([k2[hh], q2[hh]], axis=0) for hh in range(nh)])
        g2s = [_pair_dot(lhs_g, k2_t) for lhs_g, k2_t in zip(lhs_gs, k2_ts)]
        a_pairs, decays, e_gs, gam_rs, gam_lasts, beta_cs = [], [], [], [], [], []
        for j in range(DN_CHUNK_UNROLL):
            a_pair, decay, e_g, gam_r, gam_last, beta_c = [], [], [], [], [], []
            for hh in range(nh):
                gcol = 2 * nh + hh
                beta_c.append(gb2s[j][:, hh:hh + 1])
                gam_c = gam2s[j][:, gcol:gcol + 1]
                gam_r.append(gam2_ts[j][gcol:gcol + 1, :])
                gam_last.append((gam_c[c64 - 1:c64, :], gam_c[c64:c64 + 1, :]))
                decay.append(jnp.exp(jnp.where(incl, gam_c - gam_r[hh], -jnp.inf)))
                a_pair.append(jnp.where(strict, beta_c[hh] * g2s[j][hh][:c2] * decay[hh], 0.0))
                e_g.append(jnp.exp(gam_c))
            a_pairs.append(a_pair)
            decays.append(decay)
            e_gs.append(e_g)
            gam_rs.append(gam_r)
            gam_lasts.append(gam_last)
            beta_cs.append(beta_c)
        t_invs = _pair_neumann_inverse(a_pairs, c64)
        rhss = [[jnp.concatenate([beta_cs[j][hh] * v2s[j][hh], (beta_cs[j][hh] * e_gs[j][hh]) * k2s[j][hh]], axis=1)
                 for hh in range(nh)] for j in range(DN_CHUNK_UNROLL)]
        sols = [_pair_dot(t_inv, rhs) for t_inv, rhs in zip(t_invs, rhss)]
        for j, c in enumerate(chunks):
            b0 = b0s[j]
            for hh in range(nh):
                gl_f, gl_b = gam_lasts[j][hh]
                gl_row = jnp.where(lane_row < c64, gl_f, gl_b)
                us[hh, pl.ds(b0, c2), :] = sols[j][hh][:, :LANES].astype(BF16)
                ws[hh, pl.ds(b0, c2), :] = sols[j][hh][:, LANES:].astype(BF16)
                qds[hh, pl.ds(b0, c2), :] = (q2s[j][hh] * e_gs[j][hh]).astype(BF16)
                qks[hh, pl.ds(b0, c2), :] = (g2s[j][hh][c2:] * decays[j][hh]).astype(BF16)
                kdts[hh, pl.ds(b0, c2), :] = (k2_ts[j][hh] * jnp.exp(gl_row - gam_rs[j][hh])).astype(BF16)
                cds[hh, pl.ds(pl.multiple_of(c * 16, 16), 8), :] = jnp.broadcast_to(jnp.exp(gl_f), (8, LANES))
                cds[hh, pl.ds(pl.multiple_of(c * 16 + 8, 8), 8), :] = jnp.broadcast_to(jnp.exp(gl_b), (8, LANES))
        return carry

    assert n_chunks % DN_CHUNK_UNROLL == 0
    lax.fori_loop(0, n_chunks // DN_CHUNK_UNROLL, chunk_group_step, 0)

    s_s[...] = jnp.zeros(s_s.shape, F32)

    def rec_step(i, carry):
        cs = (i, n_chunks - 1 - i)
        r0s = [pl.multiple_of(c * c64, c64) for c in cs]
        b0s = [pl.multiple_of(c * c2, c2) for c in cs]
        rows = [pl.ds(pl.multiple_of(b0s[d] + d * c64, c64), c64) for d in range(2)]
        keeps = (lane_blk < c64, lane_blk >= c64)
        states = [[s_s[hh * 2 + d] for hh in range(nh)] for d in range(2)]
        wq_s = [_pair_dot([jnp.concatenate([ws[hh, rows[d], :], qds[hh, rows[d], :]], axis=0) for hh in range(nh)],
                          states[d]) for d in range(2)]
        vbs = [[(us[hh, rows[d], :].astype(F32) - wq_s[d][hh][:c64]).astype(BF16) for hh in range(nh)]
               for d in range(2)]
        lhs2 = [[jnp.concatenate([qks[hh, rows[d], :],
                                  jnp.where(keeps[d], kdts[hh, pl.ds(b0s[d], c2), :], jnp.zeros((), BF16))], axis=0)
                 for hh in range(nh)] for d in range(2)]
        r2 = [_pair_dot(lhs2[d], [jnp.concatenate([x, x], axis=0) for x in vbs[d]]) for d in range(2)]
        for d in range(2):
            for hh in range(nh):
                cd = cds[hh, pl.ds(pl.multiple_of(cs[d] * 16 + d * 8, 8), 8), :][0:1, :]
                s_s[hh * 2 + d] = states[d][hh] * cd + r2[d][hh][c64:]
                (of_s if d == 0 else ob_s)[pl.ds(r0s[d], c64), hh * LANES:(hh + 1) * LANES] = (
                    wq_s[d][hh][c64:] + r2[d][hh][:c64])
        return carry

    lax.fori_loop(0, n_chunks, rec_step, 0)

    gain = on_ref[...]

    def emit(o, z):
        outs = []
        for hh in range(nh):
            oh = o[:, hh * LANES:(hh + 1) * LANES]
            outs.append(oh * lax.rsqrt(jnp.mean(oh * oh, axis=-1, keepdims=True) + NORM_EPS) * gain)
        on = jnp.concatenate(outs, axis=1)
        zf = z.astype(F32)
        return (on * (zf * _sigmoid(zf))).astype(BF16)

    o0 = of_s[DN_PAD:c64, :] + ob_s[DN_PAD:c64, :]
    y_ref[0, n_grid:seq_len, :] = emit(o0, z_ref[0, n_grid:seq_len, :])

    def out_step(c, carry):
        r0 = pl.multiple_of(c * c64, c64)
        l0 = pl.multiple_of((c - 1) * c64, c64)
        o = of_s[pl.ds(r0, c64), :] + ob_s[pl.ds(r0, c64), :]
        y_ref[0, pl.ds(l0, c64), :] = emit(o, z_ref[0, pl.ds(l0, c64), :])
        return carry

    lax.fori_loop(1, n_chunks, out_step, 0)


def _deltanet(proj3, ba3, conv_w, gpar, out_norm):
    b, l, _ = proj3.shape
    nh = DN_NH
    wd = nh * DN_HEAD_DIM
    n_rows = l + DN_PAD
    n_chunks = n_rows // DN_CHUNK
    ng = DN_NG
    blk = lambda col0: pl.BlockSpec((1, l, wd), lambda i, g, col0=col0: (i, 0, col0 // nh + g))
    cblk = lambda part: pl.BlockSpec((DN_CONV, wd), lambda i, g, part=part: (0, part * ng + g))
    return pl.pallas_call(
        functools.partial(_dn_body, seq_len=l),
        grid=(b, ng),
        in_specs=[
            blk(COL_DQ), blk(COL_DK), blk(COL_DV), blk(COL_DZ),
            pl.BlockSpec((1, l, LANES), lambda i, g: (i, 0, g)),
            cblk(0), cblk(1), cblk(2),
            pl.BlockSpec((1, 2, LANES), lambda i, g: (g, 0, 0)),
            pl.BlockSpec((1, LANES), lambda i, g: (0, 0)),
        ],
        out_specs=pl.BlockSpec((1, l, wd), lambda i, g: (i, 0, g)),
        out_shape=jax.ShapeDtypeStruct((b, l, DN_WIDTH), BF16),
        scratch_shapes=[
            pltpu.VMEM((n_rows + 16, wd), F32), pltpu.VMEM((n_rows + 16, wd), F32),
            pltpu.VMEM((n_rows + 16, wd), F32),
            pltpu.VMEM((n_rows, wd), F32), pltpu.VMEM((n_rows, wd), F32), pltpu.VMEM((n_rows, wd), F32),
            pltpu.VMEM((n_rows, LANES), F32), pltpu.VMEM((n_rows, LANES), F32),
            pltpu.VMEM((nh, 2 * n_rows, LANES), BF16), pltpu.VMEM((nh, 2 * n_rows, LANES), BF16),
            pltpu.VMEM((nh, 2 * n_rows, LANES), BF16),
            pltpu.VMEM((nh, 2 * n_rows, LANES), BF16),
            pltpu.VMEM((nh, 2 * n_rows, LANES), BF16),
            pltpu.VMEM((nh, n_chunks * 16, LANES), F32),
            pltpu.VMEM((n_rows, wd), F32), pltpu.VMEM((n_rows, wd), F32),
            pltpu.VMEM((DN_NCH, LANES, LANES), F32),
        ],
        compiler_params=pltpu.CompilerParams(
            dimension_semantics=("arbitrary", "arbitrary"), vmem_limit_bytes=VMEM_LIMIT),
        name="deltanet",
    )(proj3, proj3, proj3, proj3, ba3, conv_w, conv_w, conv_w, gpar, out_norm)


def _na_body(q_ref, k_ref, v_ref, tb_ref, o_ref, *, rows):
    scale = NA_HEAD_DIM ** -0.5
    wr = NA_WIN_R
    nk = wr * GRID_W
    n_grid = rows * GRID_W
    qm = q_ref[0, n_grid:n_grid + N_META, :]
    km = k_ref[0, n_grid:n_grid + N_META, :]
    vm = v_ref[0, n_grid:n_grid + N_META, :]

    s = _dot_nt(qm, km) * scale
    p = jnp.exp(s - jnp.max(s, axis=-1, keepdims=True))
    o = _dot(p.astype(BF16), vm) / jnp.sum(p, axis=-1, keepdims=True)
    o_ref[0, n_grid:n_grid + N_META, :] = o.astype(BF16)

    pad = jnp.zeros((NA_KEYS - nk - N_META, LANES), BF16)

    def row_group_step(grp, carry):
        qs, ks, vs, biases, dsts = [], [], [], [], []
        for j in range(NA_ROW_GROUP):
            r = grp * NA_ROW_GROUP + j
            rs = jnp.clip(r - wr // 2, 0, rows - wr)
            dsts.append(pl.ds(pl.multiple_of(r * GRID_W, GRID_W), GRID_W))
            win = pl.ds(pl.multiple_of(rs * GRID_W, GRID_W), nk)
            qs.append(q_ref[0, dsts[j], :])
            ks.append(jnp.concatenate([k_ref[0, win, :], km, pad], axis=0))
            vs.append(jnp.concatenate([v_ref[0, win, :], vm, pad], axis=0))
            biases.append(tb_ref[0, rs - r + NA_WIN_R - 1])
        ss = [_dot_nt(qs[j], ks[j]) * scale + biases[j] for j in range(NA_ROW_GROUP)]
        ps = [jnp.exp(s - jnp.max(s, axis=-1, keepdims=True)) for s in ss]
        os_ = [_dot(ps[j].astype(BF16), vs[j]) / jnp.sum(ps[j], axis=-1, keepdims=True) for j in range(NA_ROW_GROUP)]
        for j in range(NA_ROW_GROUP):
            o_ref[0, dsts[j], :] = os_[j].astype(BF16)
        return carry

    assert rows % NA_ROW_GROUP == 0
    lax.fori_loop(0, rows // NA_ROW_GROUP, row_group_step, 0)


def _na_bias_table(rpb):
    wr, wc = NA_WIN_R, NA_WIN_C
    qc = np.arange(GRID_W)[:, None]
    kc = np.arange(GRID_W)[None, :]
    q_start = np.clip(qc - wc // 2, 0, GRID_W - wc)
    valid = (kc - q_start >= 0) & (kc - q_start < wc)
    dc = np.clip(kc - qc + wc - 1, 0, 2 * wc - 2)
    i0 = np.arange(wr)[:, None]
    jj = np.arange(wr)[None, :]
    dr = i0 + jj
    tb = rpb.astype(F32)[:, dr][:, :, :, dc]
    tb = jnp.where(valid[None, None, None], tb, -jnp.inf)
    tb = tb.transpose(0, 1, 3, 2, 4)
    tb = tb.reshape(NA_HEADS, wr, GRID_W, wr * GRID_W)
    meta = jnp.zeros(tb.shape[:3] + (N_META,), F32)
    fill = jnp.full(tb.shape[:3] + (NA_KEYS - wr * GRID_W - N_META,), -jnp.inf, F32)
    return jnp.concatenate([tb, meta, fill], axis=-1)


def _natten(proj3, tb):
    b, l, _ = proj3.shape
    rows = (l - N_META) // GRID_W
    assert rows >= NA_WIN_R
    blk = lambda col0: pl.BlockSpec((1, l, LANES), lambda h, i, col0=col0: (i, 0, col0 + h))
    return pl.pallas_call(
        functools.partial(_na_body, rows=rows),
        grid=(NA_HEADS, b),
        in_specs=[
            blk(COL_NQ), blk(COL_NK), blk(COL_NV),
            pl.BlockSpec((1, NA_WIN_R, GRID_W, NA_KEYS), lambda h, i: (h, 0, 0, 0)),
        ],
        out_specs=pl.BlockSpec((1, l, LANES), lambda h, i: (i, 0, h)),
        out_shape=jax.ShapeDtypeStruct((b, l, NA_WIDTH), BF16),
        compiler_params=pltpu.CompilerParams(
            dimension_semantics=("arbitrary", "arbitrary"), vmem_limit_bytes=VMEM_LIMIT),
        name="natten",
    )(proj3, proj3, proj3, tb)


def _mix_body(ydn_ref, yna_ref, ga_ref, gb_ref, h_ref, wdn_ref, wna_ref, wo_ref, gn_ref, wr_ref, br_ref,
              h2_ref, u2_ref, gate_ref, idx_ref):
    a = _dot(ydn_ref[...], wdn_ref[...])
    b = _dot(yna_ref[...], wna_ref[...])
    mix = _sigmoid(ga_ref[...].astype(F32)) * a + _sigmoid(gb_ref[...].astype(F32)) * b
    h2 = h_ref[...] + _dot(mix.astype(BF16), wo_ref[...])
    h2_ref[...] = h2
    u2 = h2 * lax.rsqrt(jnp.mean(h2 * h2, axis=-1, keepdims=True) + NORM_EPS) * gn_ref[...]
    _store_token_tiles(u2_ref, _pack_halves(u2), u2.shape[0])
    u_hi = u2.astype(BF16)
    u_lo = (u2 - u_hi.astype(F32)).astype(BF16)
    w_r = wr_ref[...]
    w_hi = w_r.astype(BF16)
    w_lo = (w_r - w_hi.astype(F32)).astype(BF16)
    logits = _dot(u_hi, w_hi) + (_dot(u_hi, w_lo) + _dot(u_lo, w_hi)) + br_ref[...]
    lane = lax.broadcasted_iota(I32, logits.shape, 1)
    logits = jnp.where(lane < N_EXPERTS, logits, -jnp.inf)
    vals, idxs = [], []
    for _ in range(TOP_K):
        m = jnp.max(logits, axis=-1, keepdims=True)
        sel = jnp.min(jnp.where(logits == m, lane, LANES), axis=-1, keepdims=True)
        vals.append(m)
        idxs.append(sel)
        logits = jnp.where(lane == sel, -jnp.inf, logits)
    es = [jnp.exp(v - vals[0]) for v in vals]
    den = es[0] + es[1] + es[2] + es[3]
    gates = jnp.zeros(logits.shape, F32)
    idx = jnp.zeros(logits.shape, I32)
    for k in range(TOP_K):
        gates = jnp.where(lane == k, es[k] / den, gates)
        idx = jnp.where(lane == k, idxs[k], idx)
    gate_ref[...] = gates
    idx_ref[...] = idx


def _mix(y_dn, y_na, proj, h, w_dn, w_na, w_o, gain, w_r, b_r, tm):
    t, d = h.shape
    row = lambda w: pl.BlockSpec((tm, w), lambda i: (i, 0))
    full = lambda a: pl.BlockSpec(a.shape, lambda i: (0,) * a.ndim)
    return pl.pallas_call(
        _mix_body,
        grid=(t // tm,),
        in_specs=[
            row(DN_WIDTH), row(NA_WIDTH),
            pl.BlockSpec((tm, d), lambda i: (i, COL_GA * LANES // d)),
            pl.BlockSpec((tm, d), lambda i: (i, COL_GB * LANES // d)),
            row(d), full(w_dn), full(w_na), full(w_o), full(gain), full(w_r), full(b_r),
        ],
        out_specs=[row(d), pl.BlockSpec((tm * SUBLANES, LANES), lambda i: (i, 0)), row(LANES), row(LANES)],
        out_shape=[
            jax.ShapeDtypeStruct((t, d), F32), jax.ShapeDtypeStruct((t * SUBLANES, LANES), U32),
            jax.ShapeDtypeStruct((t, LANES), F32), jax.ShapeDtypeStruct((t, LANES), I32),
        ],
        compiler_params=pltpu.CompilerParams(
            dimension_semantics=("arbitrary",), vmem_limit_bytes=VMEM_LIMIT),
        name="mix",
    )(y_dn, y_na, proj, proj, h, w_dn, w_na, w_o, gain, w_r, b_r)


def _moe_body(blk_exp, n_valid, idx_hbm, u_hbm, wg_ref, wl_ref, bg_ref, bl_ref, wo_ref, bo_ref, y_hbm,
              idx_s, xbuf, xs, acc, stage, isem, gsem, ssem, *, n_blocks, n_asg):
    i = pl.program_id(0)
    f = pl.program_id(1)
    slot = i % 2
    nxt = (i + 1) % 2
    tm = MOE_TM

    def idx_copy(blk, sl):
        return pltpu.make_async_copy(idx_hbm.at[pl.ds(pl.multiple_of(blk * (2 * tm), 2 * tm), 2 * tm)],
                                     idx_s.at[pl.ds(pl.multiple_of(sl * (2 * tm), 2 * tm), 2 * tm)], isem.at[sl])

    def tile(ref, row0):
        return ref.at[pl.ds(pl.multiple_of(row0, SUBLANES), SUBLANES), :]

    def gather_rows(sl):
        def body(r, carry):
            src = idx_s[sl * (2 * tm) + r]
            pltpu.make_async_copy(tile(u_hbm, src), tile(xbuf.at[sl], r * SUBLANES), gsem.at[sl]).start()
            return carry
        lax.fori_loop(0, tm, body, 0, unroll=DMA_ISSUE_UNROLL)

    def gather_wait(sl):
        pltpu.make_async_copy(xbuf.at[sl], xbuf.at[sl], gsem.at[sl]).wait()

    def scatter_rows(sl):
        def body(r, carry):
            dst = idx_s[sl * (2 * tm) + tm + r]
            pltpu.make_async_copy(tile(stage, r * SUBLANES), tile(y_hbm, dst), ssem).start()
            return carry
        lax.fori_loop(0, tm, body, 0, unroll=DMA_ISSUE_UNROLL)

    def scatter_wait():
        pltpu.make_async_copy(stage, stage, ssem).wait()

    live = n_valid[i] > 0

    @pl.when((i == 0) & (f == 0))
    def _():
        stage[...] = jnp.zeros(stage.shape, U32)
        n_stage = tm * SUBLANES
        tail0, tail1 = n_asg * SUBLANES, y_hbm.shape[0]
        fills = [pltpu.make_async_copy(stage.at[pl.ds(0, min(n_stage, tail1 - s0)), :],
                                       y_hbm.at[pl.ds(s0, min(n_stage, tail1 - s0)), :], ssem)
                 for s0 in range(tail0, tail1, n_stage)]
        for fill in fills:
            fill.start()
        for fill in fills:
            fill.wait()
        idx_copy(0, 0).start()
        idx_copy(0, 0).wait()

        @pl.when(live)
        def _():
            gather_rows(0)

    @pl.when((f == 0) & live)
    def _():
        gather_wait(slot)
        for s in range(SUBLANES):
            lo, hi = _unpack_halves(xbuf[slot, pl.ds(s, tm, stride=SUBLANES), :])
            xs[:, s * LANES:(s + 1) * LANES] = lo.astype(BF16)
            xs[:, HALF + s * LANES:HALF + (s + 1) * LANES] = hi.astype(BF16)

    @pl.when((f == 0) & (i + 1 < n_blocks))
    def _():
        idx_copy(i + 1, nxt).start()

    @pl.when((f == 1) & (i + 1 < n_blocks))
    def _():
        idx_copy(i + 1, nxt).wait()

        @pl.when(n_valid[jnp.minimum(i + 1, n_blocks - 1)] > 0)
        def _():
            gather_rows(nxt)

    @pl.when(live)
    def _():
        x = xs[...]
        n_sub = MOE_TF // MOE_SUB
        pre = []
        for s in range(n_sub):
            cols = slice(s * MOE_SUB, (s + 1) * MOE_SUB)
            pre.append((_dot(x, wg_ref[0, :, cols]) + bg_ref[0][:, cols],
                        _dot(x, wl_ref[0, :, cols]) + bl_ref[0][:, cols]))
        part = None
        for s in range(n_sub):
            hg, hl = pre[s]
            glu = jnp.minimum(hg, SWIGLU_LIMIT)
            lin = jnp.clip(hl, -SWIGLU_LIMIT, SWIGLU_LIMIT)
            act = glu * _sigmoid(SWIGLU_ALPHA * glu) * (lin + 1.0)
            out = _dot(act.astype(BF16), wo_ref[0, s * MOE_SUB:(s + 1) * MOE_SUB, :])
            part = out if part is None else part + out

        @pl.when(f == 0)
        def _():
            acc[...] = part

        @pl.when(f > 0)
        def _():
            acc[...] += part

    @pl.when(f == MOE_NF - 1)
    def _():
        @pl.when((i > 0) & (n_valid[jnp.maximum(i - 1, 0)] > 0))
        def _():
            scatter_wait()

        @pl.when(live)
        def _():
            _store_token_tiles(stage, _pack_halves(acc[...] + bo_ref[0]), tm)
            scatter_rows(slot)

        @pl.when((i == n_blocks - 1) & live)
        def _():
            scatter_wait()


def _moe(blk_exp, n_valid, idx_blocks, u2p, w_in, b_in, w_out, b_out, n_asg, n_rows_out):
    n_blocks = blk_exp.shape[0]
    d = D_MODEL
    nf = MOE_NF

    def wmap(off):
        def index_map(i, f, be, nv):
            fe = jnp.where(nv[i] > 0, f, nf - 1)
            return (be[i], 0, fe + off)
        return index_map

    def womap(i, f, be, nv):
        return (be[i], jnp.where(nv[i] > 0, f, nf - 1), 0)

    grid_spec = pltpu.PrefetchScalarGridSpec(
        num_scalar_prefetch=2,
        grid=(n_blocks, nf),
        in_specs=[
            pl.BlockSpec(memory_space=pl.ANY),
            pl.BlockSpec(memory_space=pl.ANY),
            pl.BlockSpec((1, d, MOE_TF), wmap(0)),
            pl.BlockSpec((1, d, MOE_TF), wmap(nf)),
            pl.BlockSpec((1, 1, MOE_TF), wmap(0)),
            pl.BlockSpec((1, 1, MOE_TF), wmap(nf)),
            pl.BlockSpec((1, MOE_TF, d), womap),
            pl.BlockSpec((1, 1, d), lambda i, f, be, nv: (be[i], 0, 0)),
        ],
        out_specs=pl.BlockSpec(memory_space=pl.ANY),
        scratch_shapes=[
            pltpu.SMEM((2 * 2 * MOE_TM,), I32),
            pltpu.VMEM((2, MOE_TM * SUBLANES, LANES), U32),
            pltpu.VMEM((MOE_TM, d), BF16),
            pltpu.VMEM((MOE_TM, d), F32),
            pltpu.VMEM((MOE_TM * SUBLANES, LANES), U32),
            pltpu.SemaphoreType.DMA((2,)),
            pltpu.SemaphoreType.DMA((2,)),
            pltpu.SemaphoreType.DMA,
        ],
    )
    return pl.pallas_call(
        functools.partial(_moe_body, n_blocks=n_blocks, n_asg=n_asg),
        grid_spec=grid_spec,
        out_shape=jax.ShapeDtypeStruct((n_rows_out * SUBLANES, LANES), U32),
        compiler_params=pltpu.CompilerParams(
            dimension_semantics=("arbitrary", "arbitrary"), vmem_limit_bytes=VMEM_LIMIT),
        name="moe",
    )(blk_exp, n_valid, idx_blocks, u2p, w_in, w_in, b_in, b_in, w_out, b_out)


def _routing_metadata(top_idx, n_blocks):
    tm = MOE_TM
    e = top_idx[:, :TOP_K].reshape(-1)
    n_asg = e.shape[0]
    order = jnp.argsort(e, stable=True).astype(I32)
    counts = jnp.sum((e[:, None] == jnp.arange(N_EXPERTS, dtype=I32)[None, :]).astype(I32), axis=0)
    cstart = jnp.cumsum(counts) - counts
    nblk = (counts + tm - 1) // tm
    bend = jnp.cumsum(nblk)
    bstart = bend - nblk
    blk = jnp.arange(n_blocks, dtype=I32)
    blk_exp = jnp.minimum(jnp.sum((bend[None, :] <= blk[:, None]).astype(I32), axis=1), N_EXPERTS - 1)
    first = (blk - bstart[blk_exp]) * tm
    n_valid = jnp.where(blk < bend[-1], jnp.clip(counts[blk_exp] - first, 0, tm), 0).astype(I32)
    r = jnp.arange(tm, dtype=I32)[None, :]
    valid = r < n_valid[:, None]
    asg = order[jnp.where(valid, (cstart[blk_exp] + first)[:, None] + r, 0)]
    src_row = (asg // TOP_K) * SUBLANES
    dst_row = jnp.where(valid, asg, n_asg + r) * SUBLANES
    idx_blocks = jnp.concatenate([src_row, dst_row], axis=1).reshape(-1)
    return blk_exp, n_valid, idx_blocks


def _final_body(h_ref, y_ref, gate_ref, gn_ref, o_ref):
    h = h_ref[0]
    gates = gate_ref[0]
    y_ref = y_ref.at[0]
    lo_acc = jnp.zeros((h.shape[0], HALF), F32)
    hi_acc = jnp.zeros((h.shape[0], HALF), F32)
    for k in range(TOP_K):
        lo, hi = _unpack_halves(_load_token_tiles(y_ref, h.shape[0], k * SUBLANES, TOP_K * SUBLANES))
        g = gates[:, k:k + 1]
        lo_acc = lo_acc + g * lo
        hi_acc = hi_acc + g * hi
    y = h + jnp.concatenate([lo_acc, hi_acc], axis=1)
    o_ref[0] = y * lax.rsqrt(jnp.mean(y * y, axis=-1, keepdims=True) + NORM_EPS) * gn_ref[...]


def _final(h2, y4, gates, gain, n_out, tm):
    b, _, d = h2.shape
    return pl.pallas_call(
        _final_body,
        grid=(b, n_out // tm),
        in_specs=[
            pl.BlockSpec((1, tm, d), lambda i, j: (i, j, 0)),
            pl.BlockSpec((1, tm * TOP_K * SUBLANES, LANES), lambda i, j: (i, j, 0)),
            pl.BlockSpec((1, tm, LANES), lambda i, j: (i, j, 0)),
            pl.BlockSpec((1, d), lambda i, j: (0, 0)),
        ],
        out_specs=pl.BlockSpec((1, tm, d), lambda i, j: (i, j, 0)),
        out_shape=jax.ShapeDtypeStruct((b, n_out, d), F32),
        compiler_params=pltpu.CompilerParams(
            dimension_semantics=("arbitrary", "arbitrary"), vmem_limit_bytes=VMEM_LIMIT),
        name="final",
    )(h2, y4, gates, gain)


def _split_w_in(w_in):
    n_dn = 4 * DN_WIDTH
    n_na = n_dn + 4 * DN_HEADS
    w_main = jnp.concatenate(
        [w_in[:, :n_dn], w_in[:, n_na + 3 * NA_WIDTH:], w_in[:, n_na:n_na + 3 * NA_WIDTH]], axis=1).astype(BF16)
    wb = w_in[:, n_dn:n_dn + 2 * DN_HEADS]
    wa = w_in[:, n_dn + 2 * DN_HEADS:n_dn + 4 * DN_HEADS]
    return w_main, _group_lanes(wb, wa).astype(BF16)


def _group_lanes(b_part, a_part):
    nh, ng = DN_NH, DN_NG
    lead = b_part.shape[:-1]
    bp = b_part.reshape(lead + (2, ng, nh))
    ap = a_part.reshape(lead + (2, ng, nh))
    perm = tuple(range(len(lead))) + (len(lead) + 1, len(lead), len(lead) + 2)
    bp = bp.transpose(perm).reshape(lead + (ng, 2 * nh))
    ap = ap.transpose(perm).reshape(lead + (ng, 2 * nh))
    both = jnp.concatenate([bp, ap], axis=-1)
    pad = [(0, 0)] * (both.ndim - 1) + [(0, LANES - 4 * nh)]
    return jnp.pad(both, pad).reshape(lead + (ng * LANES,))


def kernel(x, meta_tokens, norm_mix, w_in, dn_conv, dn_a_log, dn_dt_bias, dn_out_norm, na_rpb,
           w_branch_dn, w_branch_na, w_out, norm_ffn, w_router, b_router, w_exp_in, b_exp_in,
           w_exp_out, b_exp_out, norm_final):
    bsz, seq, d = x.shape
    l = seq + N_META
    t = bsz * l
    meta = jnp.broadcast_to(meta_tokens[None].astype(x.dtype), (bsz, N_META, d))
    h = jnp.concatenate([x, meta], axis=1).reshape(t, d)
    depth = w_in.shape[0]
    assert depth == 1, "the final kernel fuses the last residual add with the final RMSNorm"
    tm_proj = 768 if t % 768 == 0 else 16 * (l // 16)
    tm_mix = 384 if t % 384 == 0 else 8 * (l // 8)
    tm_final = 512 if seq % 512 == 0 else seq
    n_blocks = t * TOP_K // MOE_TM + N_EXPERTS
    for layer in range(depth):
        w_main, w_ba = _split_w_in(w_in[layer])
        proj, ba = _inproj(h, norm_mix[layer][None, :], w_main, w_ba, tm_proj, 1024)
        proj3 = proj.reshape(bsz, l, N_MAIN_COLS)
        ba3 = ba.reshape(bsz, l, DN_NG * LANES)
        gpar = jnp.stack([_group_lanes(jnp.zeros_like(dn_a_log[layer]).reshape(-1), dn_a_log[layer].reshape(-1)),
                          _group_lanes(jnp.zeros_like(dn_dt_bias[layer]).reshape(-1),
                                       dn_dt_bias[layer].reshape(-1))], axis=0)
        gpar = gpar.reshape(2, DN_NG, LANES).transpose(1, 0, 2)
        y_dn = _deltanet(proj3, ba3, dn_conv[layer], gpar, dn_out_norm[layer][None, :])
        y_na = _natten(proj3, _na_bias_table(na_rpb[layer]))
        w_r = jnp.pad(w_router[layer], ((0, 0), (0, LANES - N_EXPERTS)))
        b_r = jnp.pad(b_router[layer], (0, LANES - N_EXPERTS))[None, :]
        h2, u2p, gates, top_idx = _mix(
            y_dn.reshape(t, DN_WIDTH), y_na.reshape(t, NA_WIDTH), proj, h,
            w_branch_dn[layer].astype(BF16), w_branch_na[layer].astype(BF16), w_out[layer].astype(BF16),
            norm_ffn[layer][None, :], w_r, b_r, tm_mix)
        blk_exp, n_valid, idx_blocks = _routing_metadata(top_idx, n_blocks)
        y4 = _moe(blk_exp, n_valid, idx_blocks, u2p,
                  w_exp_in[layer].astype(BF16), b_exp_in[layer][:, None, :],
                  w_exp_out[layer].astype(BF16), b_exp_out[layer][:, None, :], t * TOP_K, (bsz + 1) * l * TOP_K)
        out = _final(h2.reshape(bsz, l, d), y4.reshape(bsz + 1, l * TOP_K * SUBLANES, LANES),
                     gates.reshape(bsz, l, LANES), norm_final[None, :], seq, tm_final)
    return out
```

```python
import functools

import jax
import jax.numpy as jnp
import numpy as np
from jax import lax
from jax.experimental import pallas as pl
from jax.experimental.pallas import tpu as pltpu

F32 = jnp.float32
BF16 = jnp.bfloat16
I32 = jnp.int32
U32 = jnp.uint32

D_MODEL = 2048
N_META = 16
GRID_W = 64
DN_HEADS = 8
DN_HEAD_DIM = 128
DN_WIDTH = 1024
DN_CONV = 5
DN_CHUNK = 64
NA_HEADS = 8
NA_HEAD_DIM = 128
NA_WIDTH = 1024
NA_WIN_R = 8
NA_WIN_C = 16
N_EXPERTS = 32
TOP_K = 4
D_EXPERT = 2048
SWIGLU_ALPHA = 1.702
SWIGLU_LIMIT = 7.0
NORM_EPS = 1e-6

LANES = 128
VMEM_LIMIT = 56 * 1024 * 1024

COL_DQ, COL_DK, COL_DV, COL_DZ = 0, 8, 16, 24
COL_GA, COL_GB = 32, 48
COL_NQ, COL_NK, COL_NV = 64, 72, 80
N_MAIN_COLS = 88 * LANES

DN_NH = 2
DN_NG = DN_HEADS // DN_NH
DN_NCH = 2 * DN_NH
DN_PAD = (-N_META) % DN_CHUNK
DN_CHUNK_UNROLL = 3

DMA_ISSUE_UNROLL = 8

NA_ROW_GROUP = 4
NA_KEYS = 640

MOE_TM = 512
MOE_SUB = 256
HALF = D_MODEL // 2
SUBLANES = 8
assert HALF == SUBLANES * LANES


def _store_token_tiles(ref, packed, n_tok, base=0, group=SUBLANES):
    for s in range(SUBLANES):
        ref[pl.ds(base + s, n_tok, stride=group), :] = packed[:, s * LANES:(s + 1) * LANES]


def _load_token_tiles(ref, n_tok, base=0, group=SUBLANES):
    return jnp.concatenate([ref[pl.ds(base + s, n_tok, stride=group), :] for s in range(SUBLANES)], axis=1)


def _dot(a, b, **kw):
    return jnp.dot(a, b, preferred_element_type=F32, **kw)


def _dot_nt(a, b, **kw):
    return lax.dot_general(a, b, (((1,), (1,)), ((), ())), preferred_element_type=F32, **kw)


def _dot_tn(a, b, **kw):
    return lax.dot_general(a, b, (((0,), (0,)), ((), ())), preferred_element_type=F32, **kw)


def _sigmoid(x):
    return 1.0 / (1.0 + jnp.exp(-x))


def _pack_halves(x):
    n = x.shape[1] // 2
    lo = pltpu.bitcast(x[:, :n].astype(BF16).astype(F32), U32)
    hi = pltpu.bitcast(x[:, n:].astype(BF16).astype(F32), U32)
    return (hi & jnp.uint32(0xFFFF0000)) | (lo >> 16)


def _unpack_halves(p):
    lo = pltpu.bitcast(p << 16, F32)
    hi = pltpu.bitcast(p & jnp.uint32(0xFFFF0000), F32)
    return lo, hi


def _inproj_body(h_ref, g_ref, w_ref, wba_ref, o_ref, oba_ref, u_ref):
    @pl.when(pl.program_id(1) == 0)
    def _():
        x = h_ref[...]
        y = x * lax.rsqrt(jnp.mean(x * x, axis=-1, keepdims=True) + NORM_EPS) * g_ref[...]
        ub = y.astype(BF16)
        u_ref[...] = ub
        oba_ref[...] = _dot(ub, wba_ref[...])

    o_ref[...] = _dot(u_ref[...], w_ref[...]).astype(BF16)


def _inproj(h, gain, w_main, w_ba, tm, tn):
    t, d = h.shape
    n = w_main.shape[1]
    nba = w_ba.shape[1]
    return pl.pallas_call(
        _inproj_body,
        grid=(t // tm, n // tn),
        in_specs=[
            pl.BlockSpec((tm, d), lambda i, j: (i, 0)),
            pl.BlockSpec((1, d), lambda i, j: (0, 0)),
            pl.BlockSpec((d, tn), lambda i, j: (0, j)),
            pl.BlockSpec((d, nba), lambda i, j: (0, 0)),
        ],
        out_specs=[
            pl.BlockSpec((tm, tn), lambda i, j: (i, j)),
            pl.BlockSpec((tm, nba), lambda i, j: (i, 0)),
        ],
        out_shape=[jax.ShapeDtypeStruct((t, n), BF16), jax.ShapeDtypeStruct((t, nba), F32)],
        scratch_shapes=[pltpu.VMEM((tm, d), BF16)],
        compiler_params=pltpu.CompilerParams(
            dimension_semantics=("arbitrary", "arbitrary"), vmem_limit_bytes=VMEM_LIMIT),
        name="inproj",
    )(h, gain, w_main, w_ba)


def _pair_dot(lhs, rhs):
    w = rhs[0].shape[1]
    r0, r1 = rhs[0].astype(BF16), rhs[1].astype(BF16)
    zero = jnp.zeros(r0.shape, BF16)
    bd = jnp.concatenate([jnp.concatenate([r0, zero], axis=1), jnp.concatenate([zero, r1], axis=1)], axis=0)
    x = _dot(jnp.concatenate([lhs[0].astype(BF16), lhs[1].astype(BF16)], axis=1), bd)
    return x[:, :w], x[:, w:]


def _pair_neumann_inverse(a_pairs, nilpotency):
    c = a_pairs[0][0].shape[0]
    eye = (lax.broadcasted_iota(I32, (c, c), 0) == lax.broadcasted_iota(I32, (c, c), 1)).astype(F32)
    ns = [[-a for a in pair] for pair in a_pairs]
    ps = [[eye + x for x in n] for n in ns]
    ns = [list(_pair_dot(n, n)) for n in ns]
    steps = int(np.log2(nilpotency)) - 1
    for step in range(steps):
        if step < steps - 1:
            pns = [_pair_dot([jnp.concatenate([p[i], n[i]], axis=0) for i in range(2)], n) for p, n in zip(ps, ns)]
            ps = [[p[i] + pn[i][:c] for i in range(2)] for p, pn in zip(ps, pns)]
            ns = [[pn[i][c:] for i in range(2)] for pn in pns]
        else:
            pns = [_pair_dot(p, n) for p, n in zip(ps, ns)]
            ps = [[p[i] + pn[i] for i in range(2)] for p, pn in zip(ps, pns)]
    return ps


def _chunk_cumsum(x, half):
    rows = x.shape[0]
    r = lax.broadcasted_iota(I32, x.shape, 0)
    top = r < half
    s = 1
    while s < half:
        down = pltpu.roll(x, s, axis=0)
        up = pltpu.roll(x, rows - s, axis=0)
        x = x + jnp.where(top, jnp.where(r >= s, down, 0.0), jnp.where(r < rows - s, up, 0.0))
        s *= 2
    return x


def _dn_body(q_ref, k_ref, v_ref, z_ref, ba_ref, cq_ref, ck_ref, cv_ref, gp_ref, on_ref, y_ref,
             xq, xk, xv, qn, kn, vn, gbf, gbb, us, ws, qds, kdts, qks, cds, of_s, ob_s, s_s, *, seq_len):
    nh = DN_NH
    wd = nh * DN_HEAD_DIM
    c64 = DN_CHUNK
    n_chunks = (seq_len + DN_PAD) // c64
    n_rows = n_chunks * c64
    lead = 8 + DN_PAD

    for x in (xq, xk, xv):
        x[0:lead, :] = jnp.zeros((lead, wd), F32)
        x[lead + seq_len:lead + seq_len + 8, :] = jnp.zeros((8, wd), F32)
    gbf[0:DN_PAD, :] = jnp.zeros((DN_PAD, LANES), F32)
    gbb[0:DN_PAD, :] = jnp.zeros((DN_PAD, LANES), F32)

    neg_a = -jnp.exp(gp_ref[0, 0:1, :])
    dt_b = gp_ref[0, 1:2, :]

    def copy_rows(src0, pos0, n):
        src = pl.ds(src0, n)
        dst = pl.ds(pl.multiple_of(8 + pos0, 8), n)
        xq[dst, :] = q_ref[0, src, :].astype(F32)
        xk[dst, :] = k_ref[0, src, :].astype(F32)
        xv[dst, :] = v_ref[0, src, :].astype(F32)
        ba = ba_ref[0, src, :]
        lane = lax.broadcasted_iota(I32, ba.shape, 1)
        beta = _sigmoid(ba)
        sp_in = ba + dt_b
        softplus = jnp.maximum(sp_in, 0.0) + jnp.log(1.0 + jnp.exp(-jnp.abs(sp_in)))
        val = jnp.where(lane < 2 * nh, beta, jnp.where(lane < 4 * nh, neg_a * softplus, 0.0))
        dst_g = pl.ds(pl.multiple_of(pos0, 8), n)
        gbf[dst_g, :] = val
        gbb[dst_g, :] = pltpu.roll(val, LANES - nh, axis=1)

    n_grid = seq_len - N_META
    copy_rows(n_grid, DN_PAD, N_META)

    def copy_step(j, carry):
        copy_rows(pl.multiple_of(j * c64, c64), DN_PAD + N_META + j * c64, c64)
        return carry

    lax.fori_loop(0, n_grid // c64, copy_step, 0)

    row_iota = lax.broadcasted_iota(I32, (c64, wd), 0)

    def conv_step(c, carry):
        r0 = pl.multiple_of(c * c64, c64)
        live = (row_iota + r0) >= DN_PAD
        for x, cw, dst, kind in ((xq, cq_ref, qn, "q"), (xk, ck_ref, kn, "k"), (xv, cv_ref, vn, "v")):
            win = x[pl.ds(r0, c64 + 16), :]
            acc = jnp.zeros((c64, wd), F32)
            for j in range(DN_CONV):
                off = 8 - DN_CONV // 2 + j
                acc = acc + win[off:off + c64, :] * cw[j:j + 1, :]
            y = jnp.where(live, acc * _sigmoid(acc), 0.0)
            if kind == "v":
                dst[pl.ds(r0, c64), :] = y
            else:
                scale = DN_HEAD_DIM ** -0.5 if kind == "q" else 1.0
                for hh in range(nh):
                    yh = y[:, hh * LANES:(hh + 1) * LANES]
                    yn = yh * lax.rsqrt(jnp.sum(yh * yh, axis=-1, keepdims=True) + 1e-6)
                    dst[pl.ds(r0, c64), hh * LANES:(hh + 1) * LANES] = yn * scale
        return carry

    lax.fori_loop(0, n_chunks, conv_step, 0)

    c2 = 2 * c64
    ri = lax.broadcasted_iota(I32, (c2, c2), 0)
    ci = lax.broadcasted_iota(I32, (c2, c2), 1)
    fwd_blk = (ri < c64) & (ci < c64)
    bwd_blk = (ri >= c64) & (ci >= c64)
    incl = (fwd_blk & (ri >= ci)) | (bwd_blk & (ri <= ci))
    strict = (fwd_blk & (ri > ci)) | (bwd_blk & (ri < ci))
    lane_row = lax.broadcasted_iota(I32, (1, c2), 1)
    lane_blk = lax.broadcasted_iota(I32, (c2, c2), 1)

    def chunk_group_step(grp, carry):
        chunks = [grp * DN_CHUNK_UNROLL + j for j in range(DN_CHUNK_UNROLL)]
        r0s = [pl.multiple_of(c * c64, c64) for c in chunks]
        b0s = [pl.multiple_of(c * c2, c2) for c in chunks]
        gb2s, gam2s, gam2_ts, q2s, k2s, v2s, k2_ts, lhs_gs = [], [], [], [], [], [], [], []
        for r0 in r0s:
            gb2 = jnp.concatenate([gbf[pl.ds(r0, c64), :], gbb[pl.ds(r0, c64), :]], axis=0)
            gam2 = _chunk_cumsum(gb2, c64)
            gb2s.append(gb2)
            gam2s.append(gam2)
            gam2_ts.append(gam2.T)
            q2, k2, v2 = [], [], []
            for hh in range(nh):
                hs = slice(hh * LANES, (hh + 1) * LANES)
                q = qn[pl.ds(r0, c64), hs]
                k = kn[pl.ds(r0, c64), hs]
                v = vn[pl.ds(r0, c64), hs]
                q2.append(jnp.concatenate([q, q], axis=0))
                k2.append(jnp.concatenate([k, k], axis=0))
                v2.append(jnp.concatenate([v, v], axis=0))
            q2s.append(q2)
            k2s.append(k2)
            v2s.append(v2)
            k2_ts.append([x.T for x in k2])
            lhs_gs.append([jnp.concatenate([k2[hh], q2[hh]], axis=0) for hh in range(nh)])
        g2s = [_pair_dot(lhs_g, k2_t) for lhs_g, k2_t in zip(lhs_gs, k2_ts)]
        a_pairs, decays, e_gs, gam_rs, gam_lasts, beta_cs = [], [], [], [], [], []
        for j in range(DN_CHUNK_UNROLL):
            a_pair, decay, e_g, gam_r, gam_last, beta_c = [], [], [], [], [], []
            for hh in range(nh):
                gcol = 2 * nh + hh
                beta_c.append(gb2s[j][:, hh:hh + 1])
                gam_c = gam2s[j][:, gcol:gcol + 1]
                gam_r.append(gam2_ts[j][gcol:gcol + 1, :])
                gam_last.append((gam_c[c64 - 1:c64, :], gam_c[c64:c64 + 1, :]))
                decay.append(jnp.exp(jnp.where(incl, gam_c - gam_r[hh], -jnp.inf)))
                a_pair.append(jnp.where(strict, beta_c[hh] * g2s[j][hh][:c2] * decay[hh], 0.0))
                e_g.append(jnp.exp(gam_c))
            a_pairs.append(a_pair)
            decays.append(decay)
            e_gs.append(e_g)
            gam_rs.append(gam_r)
            gam_lasts.append(gam_last)
            beta_cs.append(beta_c)
        t_invs = _pair_neumann_inverse(a_pairs, c64)
        rhss = [[jnp.concatenate([beta_cs[j][hh] * v2s[j][hh], (beta_cs[j][hh] * e_gs[j][hh]) * k2s[j][hh]], axis=1)
                 for hh in range(nh)] for j in range(DN_CHUNK_UNROLL)]
        sols = [_pair_dot(t_inv, rhs) for t_inv, rhs in zip(t_invs, rhss)]
        for j, c in enumerate(chunks):
            b0 = b0s[j]
            for hh in range(nh):
                gl_f, gl_b = gam_lasts[j][hh]
                gl_row = jnp.where(lane_row < c64, gl_f, gl_b)
                us[hh, pl.ds(b0, c2), :] = sols[j][hh][:, :LANES].astype(BF16)
                ws[hh, pl.ds(b0, c2), :] = sols[j][hh][:, LANES:].astype(BF16)
                qds[hh, pl.ds(b0, c2), :] = (q2s[j][hh] * e_gs[j][hh]).astype(BF16)
                qks[hh, pl.ds(b0, c2), :] = (g2s[j][hh][c2:] * decays[j][hh]).astype(BF16)
                kdts[hh, pl.ds(b0, c2), :] = (k2_ts[j][hh] * jnp.exp(gl_row - gam_rs[j][hh])).astype(BF16)
                cds[hh, pl.ds(pl.multiple_of(c * 16, 16), 8), :] = jnp.broadcast_to(jnp.exp(gl_f), (8, LANES))
                cds[hh, pl.ds(pl.multiple_of(c * 16 + 8, 8), 8), :] = jnp.broadcast_to(jnp.exp(gl_b), (8, LANES))
        return carry

    assert n_chunks % DN_CHUNK_UNROLL == 0
    lax.fori_loop(0, n_chunks // DN_CHUNK_UNROLL, chunk_group_step, 0)

    s_s[...] = jnp.zeros(s_s.shape, F32)

    def rec_step(i, carry):
        cs = (i, n_chunks - 1 - i)
        r0s = [pl.multiple_of(c * c64, c64) for c in cs]
        b0s = [pl.multiple_of(c * c2, c2) for c in cs]
        rows = [pl.ds(pl.multiple_of(b0s[d] + d * c64, c64), c64) for d in range(2)]
        keeps = (lane_blk < c64, lane_blk >= c64)
        states = [[s_s[hh * 2 + d] for hh in range(nh)] for d in range(2)]
        wq_s = [_pair_dot([jnp.concatenate([ws[hh, rows[d], :], qds[hh, rows[d], :]], axis=0) for hh in range(nh)],
                          states[d]) for d in range(2)]
        vbs = [[(us[hh, rows[d], :].astype(F32) - wq_s[d][hh][:c64]).astype(BF16) for hh in range(nh)]
               for d in range(2)]
        lhs2 = [[jnp.concatenate([qks[hh, rows[d], :],
                                  jnp.where(keeps[d], kdts[hh, pl.ds(b0s[d], c2), :], jnp.zeros((), BF16))], axis=0)
                 for hh in range(nh)] for d in range(2)]
        r2 = [_pair_dot(lhs2[d], [jnp.concatenate([x, x], axis=0) for x in vbs[d]]) for d in range(2)]
        for d in range(2):
            for hh in range(nh):
                cd = cds[hh, pl.ds(pl.multiple_of(cs[d] * 16 + d * 8, 8), 8), :][0:1, :]
                s_s[hh * 2 + d] = states[d][hh] * cd + r2[d][hh][c64:]
                (of_s if d == 0 else ob_s)[pl.ds(r0s[d], c64), hh * LANES:(hh + 1) * LANES] = (
                    wq_s[d][hh][c64:] + r2[d][hh][:c64])
        return carry

    lax.fori_loop(0, n_chunks, rec_step, 0)

    gain = on_ref[...]

    def emit(o, z):
        outs = []
        for hh in range(nh):
            oh = o[:, hh * LANES:(hh + 1) * LANES]
            outs.append(oh * lax.rsqrt(jnp.mean(oh * oh, axis=-1, keepdims=True) + NORM_EPS) * gain)
        on = jnp.concatenate(outs, axis=1)
        zf = z.astype(F32)
        return (on * (zf * _sigmoid(zf))).astype(BF16)

    o0 = of_s[DN_PAD:c64, :] + ob_s[DN_PAD:c64, :]
    y_ref[0, n_grid:seq_len, :] = emit(o0, z_ref[0, n_grid:seq_len, :])

    def out_step(c, carry):
        r0 = pl.multiple_of(c * c64, c64)
        l0 = pl.multiple_of((c - 1) * c64, c64)
        o = of_s[pl.ds(r0, c64), :] + ob_s[pl.ds(r0, c64), :]
        y_ref[0, pl.ds(l0, c64), :] = emit(o, z_ref[0, pl.ds(l0, c64), :])
        return carry

    lax.fori_loop(1, n_chunks, out_step, 0)


def _deltanet(proj3, ba3, conv_w, gpar, out_norm):
    b, l, _ = proj3.shape
    nh = DN_NH
    wd = nh * DN_HEAD_DIM
    n_rows = l + DN_PAD
    n_chunks = n_rows // DN_CHUNK
    ng = DN_NG
    blk = lambda col0: pl.BlockSpec((1, l, wd), lambda i, g, col0=col0: (i, 0, col0 // nh + g))
    cblk = lambda part: pl.BlockSpec((DN_CONV, wd), lambda i, g, part=part: (0, part * ng + g))
    return pl.pallas_call(
        functools.partial(_dn_body, seq_len=l),
        grid=(b, ng),
        in_specs=[
            blk(COL_DQ), blk(COL_DK), blk(COL_DV), blk(COL_DZ),
            pl.BlockSpec((1, l, LANES), lambda i, g: (i, 0, g)),
            cblk(0), cblk(1), cblk(2),
            pl.BlockSpec((1, 2, LANES), lambda i, g: (g, 0, 0)),
            pl.BlockSpec((1, LANES), lambda i, g: (0, 0)),
        ],
        out_specs=pl.BlockSpec((1, l, wd), lambda i, g: (i, 0, g)),
        out_shape=jax.ShapeDtypeStruct((b, l, DN_WIDTH), BF16),
        scratch_shapes=[
            pltpu.VMEM((n_rows + 16, wd), F32), pltpu.VMEM((n_rows + 16, wd), F32),
            pltpu.VMEM((n_rows + 16, wd), F32),
            pltpu.VMEM((n_rows, wd), F32), pltpu.VMEM((n_rows, wd), F32), pltpu.VMEM((n_rows, wd), F32),
            pltpu.VMEM((n_rows, LANES), F32), pltpu.VMEM((n_rows, LANES), F32),
            pltpu.VMEM((nh, 2 * n_rows, LANES), BF16), pltpu.VMEM((nh, 2 * n_rows, LANES), BF16),
            pltpu.VMEM((nh, 2 * n_rows, LANES), BF16),
            pltpu.VMEM((nh, 2 * n_rows, LANES), BF16),
            pltpu.VMEM((nh, 2 * n_rows, LANES), BF16),
            pltpu.VMEM((nh, n_chunks * 16, LANES), F32),
            pltpu.VMEM((n_rows, wd), F32), pltpu.VMEM((n_rows, wd), F32),
            pltpu.VMEM((DN_NCH, LANES, LANES), F32),
        ],
        compiler_params=pltpu.CompilerParams(
            dimension_semantics=("arbitrary", "arbitrary"), vmem_limit_bytes=VMEM_LIMIT),
        name="deltanet",
    )(proj3, proj3, proj3, proj3, ba3, conv_w, conv_w, conv_w, gpar, out_norm)


def _na_body(q_ref, k_ref, v_ref, tb_ref, o_ref, *, rows):
    scale = NA_HEAD_DIM ** -0.5
    wr = NA_WIN_R
    nk = wr * GRID_W
    n_grid = rows * GRID_W
    qm = q_ref[0, n_grid:n_grid + N_META, :]
    km = k_ref[0, n_grid:n_grid + N_META, :]
    vm = v_ref[0, n_grid:n_grid + N_META, :]

    s = _dot_nt(qm, km) * scale
    p = jnp.exp(s - jnp.max(s, axis=-1, keepdims=True))
    o = _dot(p.astype(BF16), vm) / jnp.sum(p, axis=-1, keepdims=True)
    o_ref[0, n_grid:n_grid + N_META, :] = o.astype(BF16)

    pad = jnp.zeros((NA_KEYS - nk - N_META, LANES), BF16)

    def row_group_step(grp, carry):
        qs, ks, vs, biases, dsts = [], [], [], [], []
        for j in range(NA_ROW_GROUP):
            r = grp * NA_ROW_GROUP + j
            rs = jnp.clip(r - wr // 2, 0, rows - wr)
            dsts.append(pl.ds(pl.multiple_of(r * GRID_W, GRID_W), GRID_W))
            win = pl.ds(pl.multiple_of(rs * GRID_W, GRID_W), nk)
            qs.append(q_ref[0, dsts[j], :])
            ks.append(jnp.concatenate([k_ref[0, win, :], km, pad], axis=0))
            vs.append(jnp.concatenate([v_ref[0, win, :], vm, pad], axis=0))
            biases.append(tb_ref[0, rs - r + NA_WIN_R - 1])
        ss = [_dot_nt(qs[j], ks[j]) * scale + biases[j] for j in range(NA_ROW_GROUP)]
        ps = [jnp.exp(s - jnp.max(s, axis=-1, keepdims=True)) for s in ss]
        os_ = [_dot(ps[j].astype(BF16), vs[j]) / jnp.sum(ps[j], axis=-1, keepdims=True) for j in range(NA_ROW_GROUP)]
        for j in range(NA_ROW_GROUP):
            o_ref[0, dsts[j], :] = os_[j].astype(BF16)
        return carry

    assert rows % NA_ROW_GROUP == 0
    lax.fori_loop(0, rows // NA_ROW_GROUP, row_group_step, 0)


def _na_bias_table(rpb):
    wr, wc = NA_WIN_R, NA_WIN_C
    qc = np.arange(GRID_W)[:, None]
    kc = np.arange(GRID_W)[None, :]
    q_start = np.clip(qc - wc // 2, 0, GRID_W - wc)
    valid = (kc - q_start >= 0) & (kc - q_start < wc)
    dc = np.clip(kc - qc + wc - 1, 0, 2 * wc - 2)
    i0 = np.arange(wr)[:, None]
    jj = np.arange(wr)[None, :]
    dr = i0 + jj
    tb = rpb.astype(F32)[:, dr][:, :, :, dc]
    tb = jnp.where(valid[None, None, None], tb, -jnp.inf)
    tb = tb.transpose(0, 1, 3, 2, 4)
    tb = tb.reshape(NA_HEADS, wr, GRID_W, wr * GRID_W)
    meta = jnp.zeros(tb.shape[:3] + (N_META,), F32)
    fill = jnp.full(tb.shape[:3] + (NA_KEYS - wr * GRID_W - N_META,), -jnp.inf, F32)
    return jnp.concatenate([tb, meta, fill], axis=-1)


def _natten(proj3, tb):
    b, l, _ = proj3.shape
    rows = (l - N_META) // GRID_W
    assert rows >= NA_WIN_R
    blk = lambda col0: pl.BlockSpec((1, l, LANES), lambda h, i, col0=col0: (i, 0, col0 + h))
    return pl.pallas_call(
        functools.partial(_na_body, rows=rows),
        grid=(NA_HEADS, b),
        in_specs=[
            blk(COL_NQ), blk(COL_NK), blk(COL_NV),
            pl.BlockSpec((1, NA_WIN_R, GRID_W, NA_KEYS), lambda h, i: (h, 0, 0, 0)),
        ],
        out_specs=pl.BlockSpec((1, l, LANES), lambda h, i: (i, 0, h)),
        out_shape=jax.ShapeDtypeStruct((b, l, NA_WIDTH), BF16),
        compiler_params=pltpu.CompilerParams(
            dimension_semantics=("arbitrary", "arbitrary"), vmem_limit_bytes=VMEM_LIMIT),
        name="natten",
    )(proj3, proj3, proj3, tb)


def _mix_body(ydn_ref, yna_ref, ga_ref, gb_ref, h_ref, wdn_ref, wna_ref, wo_ref, gn_ref, wr_ref, br_ref,
              h2_ref, u2_ref, gate_ref, idx_ref):
    a = _dot(ydn_ref[...], wdn_ref[...])
    b = _dot(yna_ref[...], wna_ref[...])
    mix = _sigmoid(ga_ref[...].astype(F32)) * a + _sigmoid(gb_ref[...].astype(F32)) * b
    h2 = h_ref[...] + _dot(mix.astype(BF16), wo_ref[...])
    h2_ref[...] = h2
    u2 = h2 * lax.rsqrt(jnp.mean(h2 * h2, axis=-1, keepdims=True) + NORM_EPS) * gn_ref[...]
    _store_token_tiles(u2_ref, _pack_halves(u2), u2.shape[0])
    u_hi = u2.astype(BF16)
    u_lo = (u2 - u_hi.astype(F32)).astype(BF16)
    w_r = wr_ref[...]
    w_hi = w_r.astype(BF16)
    w_lo = (w_r - w_hi.astype(F32)).astype(BF16)
    logits = _dot(u_hi, w_hi) + (_dot(u_hi, w_lo) + _dot(u_lo, w_hi)) + br_ref[...]
    lane = lax.broadcasted_iota(I32, logits.shape, 1)
    logits = jnp.where(lane < N_EXPERTS, logits, -jnp.inf)
    vals, idxs = [], []
    for _ in range(TOP_K):
        m = jnp.max(logits, axis=-1, keepdims=True)
        sel = jnp.min(jnp.where(logits == m, lane, LANES), axis=-1, keepdims=True)
        vals.append(m)
        idxs.append(sel)
        logits = jnp.where(lane == sel, -jnp.inf, logits)
    es = [jnp.exp(v - vals[0]) for v in vals]
    den = es[0] + es[1] + es[2] + es[3]
    gates = jnp.zeros(logits.shape, F32)
    idx = jnp.zeros(logits.shape, I32)
    for k in range(TOP_K):
        gates = jnp.where(lane == k, es[k] / den, gates)
        idx = jnp.where(lane == k, idxs[k], idx)
    gate_ref[...] = gates
    idx_ref[...] = idx


def _mix(y_dn, y_na, proj, h, w_dn, w_na, w_o, gain, w_r, b_r, tm):
    t, d = h.shape
    row = lambda w: pl.BlockSpec((tm, w), lambda i: (i, 0))
    full = lambda a: pl.BlockSpec(a.shape, lambda i: (0,) * a.ndim)
    return pl.pallas_call(
        _mix_body,
        grid=(t // tm,),
        in_specs=[
            row(DN_WIDTH), row(NA_WIDTH),
            pl.BlockSpec((tm, d), lambda i: (i, COL_GA * LANES // d)),
            pl.BlockSpec((tm, d), lambda i: (i, COL_GB * LANES // d)),
            row(d), full(w_dn), full(w_na), full(w_o), full(gain), full(w_r), full(b_r),
        ],
        out_specs=[row(d), pl.BlockSpec((tm * SUBLANES, LANES), lambda i: (i, 0)), row(LANES), row(LANES)],
        out_shape=[
            jax.ShapeDtypeStruct((t, d), F32), jax.ShapeDtypeStruct((t * SUBLANES, LANES), U32),
            jax.ShapeDtypeStruct((t, LANES), F32), jax.ShapeDtypeStruct((t, LANES), I32),
        ],
        compiler_params=pltpu.CompilerParams(
            dimension_semantics=("arbitrary",), vmem_limit_bytes=VMEM_LIMIT),
        name="mix",
    )(y_dn, y_na, proj, proj, h, w_dn, w_na, w_o, gain, w_r, b_r)


def _moe_body(blk_exp, n_valid, idx_hbm, u_hbm, wg_ref, wl_ref, bg_ref, bl_ref, wo_ref, bo_ref, y_hbm,
              idx_s, xbuf, xs, act_s, stage, isem, gsem, ssem, *, n_blocks, n_asg):
    i = pl.program_id(0)
    tm = MOE_TM

    def idx_copy(blk):
        sl = blk % 3
        return pltpu.make_async_copy(idx_hbm.at[pl.ds(pl.multiple_of(blk * (2 * tm), 2 * tm), 2 * tm)],
                                     idx_s.at[pl.ds(pl.multiple_of(sl * (2 * tm), 2 * tm), 2 * tm)], isem.at[sl])

    def tile(ref, row0):
        return ref.at[pl.ds(pl.multiple_of(row0, SUBLANES), SUBLANES), :]

    def gather_rows(blk):
        base = (blk % 3) * (2 * tm)
        sl = blk % 2

        def body(r, carry):
            src = idx_s[base + r]
            pltpu.make_async_copy(tile(u_hbm, src), tile(xbuf.at[sl], r * SUBLANES), gsem.at[sl]).start()
            return carry
        lax.fori_loop(0, tm, body, 0, unroll=DMA_ISSUE_UNROLL)

    def gather_wait(sl):
        pltpu.make_async_copy(xbuf.at[sl], xbuf.at[sl], gsem.at[sl]).wait()

    def scatter_rows(blk):
        base = (blk % 3) * (2 * tm) + tm

        def body(r, carry):
            dst = idx_s[base + r]
            pltpu.make_async_copy(tile(stage, r * SUBLANES), tile(y_hbm, dst), ssem).start()
            return carry
        lax.fori_loop(0, tm, body, 0, unroll=DMA_ISSUE_UNROLL)

    def scatter_wait():
        pltpu.make_async_copy(stage, stage, ssem).wait()

    live = n_valid[i] > 0

    prev_live = (i > 0) & (n_valid[jnp.maximum(i - 1, 0)] > 0)
    next_live = (i + 1 < n_blocks) & (n_valid[jnp.minimum(i + 1, n_blocks - 1)] > 0)
    slot = i % 2

    @pl.when(i == 0)
    def _():
        stage[...] = jnp.zeros(stage.shape, U32)
        n_stage = tm * SUBLANES
        tail0, tail1 = n_asg * SUBLANES, y_hbm.shape[0]
        fills = [pltpu.make_async_copy(stage.at[pl.ds(0, min(n_stage, tail1 - s0)), :],
                                       y_hbm.at[pl.ds(s0, min(n_stage, tail1 - s0)), :], ssem)
                 for s0 in range(tail0, tail1, n_stage)]
        for fill in fills:
            fill.start()
        for fill in fills:
            fill.wait()
        idx_copy(0).start()
        idx_copy(0).wait()

        @pl.when(live)
        def _():
            gather_rows(0)

        if n_blocks > 1:
            idx_copy(1).start()

    @pl.when(i + 1 < n_blocks)
    def _():
        idx_copy(i + 1).wait()

        @pl.when(next_live)
        def _():
            gather_rows(i + 1)

    @pl.when(i + 2 < n_blocks)
    def _():
        idx_copy(i + 2).start()

    @pl.when(jnp.logical_not(live) & prev_live)
    def _():
        scatter_wait()

    @pl.when(live)
    def _():
        gather_wait(slot)
        for s in range(SUBLANES):
            lo, hi = _unpack_halves(xbuf[slot, pl.ds(s, tm, stride=SUBLANES), :])
            xs[:, s * LANES:(s + 1) * LANES] = lo.astype(BF16)
            xs[:, HALF + s * LANES:HALF + (s + 1) * LANES] = hi.astype(BF16)
        x = xs[...]
        for s in range(D_EXPERT // MOE_SUB):
            cols = slice(s * MOE_SUB, (s + 1) * MOE_SUB)
            hg = _dot(x, wg_ref[0, :, cols]) + bg_ref[0][:, cols]
            hl = _dot(x, wl_ref[0, :, cols]) + bl_ref[0][:, cols]
            glu = jnp.minimum(hg, SWIGLU_LIMIT)
            lin = jnp.clip(hl, -SWIGLU_LIMIT, SWIGLU_LIMIT)
            act_s[:, cols] = (glu * _sigmoid(SWIGLU_ALPHA * glu) * (lin + 1.0)).astype(BF16)

        @pl.when(prev_live)
        def _():
            scatter_wait()

        act = act_s[...]
        tiles_per_sub = MOE_SUB // LANES
        for c in range(HALF // MOE_SUB):
            lo_cols = slice(c * MOE_SUB, (c + 1) * MOE_SUB)
            hi_cols = slice(HALF + c * MOE_SUB, HALF + (c + 1) * MOE_SUB)
            out = jnp.concatenate([_dot(act, wo_ref[0, :, lo_cols]) + bo_ref[0][:, lo_cols],
                                   _dot(act, wo_ref[0, :, hi_cols]) + bo_ref[0][:, hi_cols]], axis=1)
            packed = _pack_halves(out)
            for j in range(tiles_per_sub):
                stage[pl.ds(c * tiles_per_sub + j, tm, stride=SUBLANES), :] = packed[:, j * LANES:(j + 1) * LANES]
        scatter_rows(i)

        @pl.when(i == n_blocks - 1)
        def _():
            scatter_wait()


def _moe(blk_exp, n_valid, idx_blocks, u2p, w_in, b_in, w_out, b_out, n_asg, n_rows_out):
    n_blocks = blk_exp.shape[0]
    d = D_MODEL
    f = D_EXPERT

    def half(g):
        return lambda i, be, nv: (be[i], 0, g)

    grid_spec = pltpu.PrefetchScalarGridSpec(
        num_scalar_prefetch=2,
        grid=(n_blocks,),
        in_specs=[
            pl.BlockSpec(memory_space=pl.ANY),
            pl.BlockSpec(memory_space=pl.ANY),
            pl.BlockSpec((1, d, f), half(0), pipeline_mode=pl.Buffered(1)),
            pl.BlockSpec((1, d, f), half(1), pipeline_mode=pl.Buffered(1)),
            pl.BlockSpec((1, 1, f), half(0)),
            pl.BlockSpec((1, 1, f), half(1)),
            pl.BlockSpec((1, f, d), lambda i, be, nv: (be[i], 0, 0)),
            pl.BlockSpec((1, 1, d), lambda i, be, nv: (be[i], 0, 0)),
        ],
        out_specs=pl.BlockSpec(memory_space=pl.ANY),
        scratch_shapes=[
            pltpu.SMEM((3 * 2 * MOE_TM,), I32),
            pltpu.VMEM((2, MOE_TM * SUBLANES, LANES), U32),
            pltpu.VMEM((MOE_TM, d), BF16),
            pltpu.VMEM((MOE_TM, f), BF16),
            pltpu.VMEM((MOE_TM * SUBLANES, LANES), U32),
            pltpu.SemaphoreType.DMA((3,)),
            pltpu.SemaphoreType.DMA((2,)),
            pltpu.SemaphoreType.DMA,
        ],
    )
    return pl.pallas_call(
        functools.partial(_moe_body, n_blocks=n_blocks, n_asg=n_asg),
        grid_spec=grid_spec,
        out_shape=jax.ShapeDtypeStruct((n_rows_out * SUBLANES, LANES), U32),
        compiler_params=pltpu.CompilerParams(
            dimension_semantics=("arbitrary",), vmem_limit_bytes=VMEM_LIMIT),
        name="moe",
    )(blk_exp, n_valid, idx_blocks, u2p, w_in, w_in, b_in, b_in, w_out, b_out)


def _routing_metadata(top_idx, n_blocks):
    tm = MOE_TM
    e = top_idx[:, :TOP_K].reshape(-1)
    n_asg = e.shape[0]
    order = jnp.argsort(e, stable=True).astype(I32)
    counts = jnp.sum((e[:, None] == jnp.arange(N_EXPERTS, dtype=I32)[None, :]).astype(I32), axis=0)
    cstart = jnp.cumsum(counts) - counts
    nblk = (counts + tm - 1) // tm
    bend = jnp.cumsum(nblk)
    bstart = bend - nblk
    blk = jnp.arange(n_blocks, dtype=I32)
    blk_exp = jnp.minimum(jnp.sum((bend[None, :] <= blk[:, None]).astype(I32), axis=1), N_EXPERTS - 1)
    first = (blk - bstart[blk_exp]) * tm
    n_valid = jnp.where(blk < bend[-1], jnp.clip(counts[blk_exp] - first, 0, tm), 0).astype(I32)
    r = jnp.arange(tm, dtype=I32)[None, :]
    valid = r < n_valid[:, None]
    asg = order[jnp.where(valid, (cstart[blk_exp] + first)[:, None] + r, 0)]
    src_row = (asg // TOP_K) * SUBLANES
    dst_row = jnp.where(valid, asg, n_asg + r) * SUBLANES
    idx_blocks = jnp.concatenate([src_row, dst_row], axis=1).reshape(-1)
    return blk_exp, n_valid, idx_blocks


def _final_body(h_ref, y_ref, gate_ref, gn_ref, o_ref):
    h = h_ref[0]
    gates = gate_ref[0]
    y_ref = y_ref.at[0]
    lo_acc = jnp.zeros((h.shape[0], HALF), F32)
    hi_acc = jnp.zeros((h.shape[0], HALF), F32)
    for k in range(TOP_K):
        lo, hi = _unpack_halves(_load_token_tiles(y_ref, h.shape[0], k * SUBLANES, TOP_K * SUBLANES))
        g = gates[:, k:k + 1]
        lo_acc = lo_acc + g * lo
        hi_acc = hi_acc + g * hi
    y = h + jnp.concatenate([lo_acc, hi_acc], axis=1)
    o_ref[0] = y * lax.rsqrt(jnp.mean(y * y, axis=-1, keepdims=True) + NORM_EPS) * gn_ref[...]


def _final(h2, y4, gates, gain, n_out, tm):
    b, _, d = h2.shape
    return pl.pallas_call(
        _final_body,
        grid=(b, n_out // tm),
        in_specs=[
            pl.BlockSpec((1, tm, d), lambda i, j: (i, j, 0)),
            pl.BlockSpec((1, tm * TOP_K * SUBLANES, LANES), lambda i, j: (i, j, 0)),
            pl.BlockSpec((1, tm, LANES), lambda i, j: (i, j, 0)),
            pl.BlockSpec((1, d), lambda i, j: (0, 0)),
        ],
        out_specs=pl.BlockSpec((1, tm, d), lambda i, j: (i, j, 0)),
        out_shape=jax.ShapeDtypeStruct((b, n_out, d), F32),
        compiler_params=pltpu.CompilerParams(
            dimension_semantics=("arbitrary", "arbitrary"), vmem_limit_bytes=VMEM_LIMIT),
        name="final",
    )(h2, y4, gates, gain)


def _split_w_in(w_in):
    n_dn = 4 * DN_WIDTH
    n_na = n_dn + 4 * DN_HEADS
    w_main = jnp.concatenate(
        [w_in[:, :n_dn], w_in[:, n_na + 3 * NA_WIDTH:], w_in[:, n_na:n_na + 3 * NA_WIDTH]], axis=1).astype(BF16)
    wb = w_in[:, n_dn:n_dn + 2 * DN_HEADS]
    wa = w_in[:, n_dn + 2 * DN_HEADS:n_dn + 4 * DN_HEADS]
    return w_main, _group_lanes(wb, wa).astype(BF16)


def _group_lanes(b_part, a_part):
    nh, ng = DN_NH, DN_NG
    lead = b_part.shape[:-1]
    bp = b_part.reshape(lead + (2, ng, nh))
    ap = a_part.reshape(lead + (2, ng, nh))
    perm = tuple(range(len(lead))) + (len(lead) + 1, len(lead), len(lead) + 2)
    bp = bp.transpose(perm).reshape(lead + (ng, 2 * nh))
    ap = ap.transpose(perm).reshape(lead + (ng, 2 * nh))
    both = jnp.concatenate([bp, ap], axis=-1)
    pad = [(0, 0)] * (both.ndim - 1) + [(0, LANES - 4 * nh)]
    return jnp.pad(both, pad).reshape(lead + (ng * LANES,))


def kernel(x, meta_tokens, norm_mix, w_in, dn_conv, dn_a_log, dn_dt_bias, dn_out_norm, na_rpb,
           w_branch_dn, w_branch_na, w_out, norm_ffn, w_router, b_router, w_exp_in, b_exp_in,
           w_exp_out, b_exp_out, norm_final):
    bsz, seq, d = x.shape
    l = seq + N_META
    t = bsz * l
    meta = jnp.broadcast_to(meta_tokens[None].astype(x.dtype), (bsz, N_META, d))
    h = jnp.concatenate([x, meta], axis=1).reshape(t, d)
    depth = w_in.shape[0]
    assert depth == 1, "the final kernel fuses the last residual add with the final RMSNorm"
    tm_proj = 768 if t % 768 == 0 else 16 * (l // 16)
    tm_mix = 384 if t % 384 == 0 else 8 * (l // 8)
    tm_final = 512 if seq % 512 == 0 else seq
    n_blocks = t * TOP_K // MOE_TM + N_EXPERTS
    for layer in range(depth):
        w_main, w_ba = _split_w_in(w_in[layer])
        proj, ba = _inproj(h, norm_mix[layer][None, :], w_main, w_ba, tm_proj, 1024)
        proj3 = proj.reshape(bsz, l, N_MAIN_COLS)
        ba3 = ba.reshape(bsz, l, DN_NG * LANES)
        gpar = jnp.stack([_group_lanes(jnp.zeros_like(dn_a_log[layer]).reshape(-1), dn_a_log[layer].reshape(-1)),
                          _group_lanes(jnp.zeros_like(dn_dt_bias[layer]).reshape(-1),
                                       dn_dt_bias[layer].reshape(-1))], axis=0)
        gpar = gpar.reshape(2, DN_NG, LANES).transpose(1, 0, 2)
        y_dn = _deltanet(proj3, ba3, dn_conv[layer], gpar, dn_out_norm[layer][None, :])
        y_na = _natten(proj3, _na_bias_table(na_rpb[layer]))
        w_r = jnp.pad(w_router[layer], ((0, 0), (0, LANES - N_EXPERTS)))
        b_r = jnp.pad(b_router[layer], (0, LANES - N_EXPERTS))[None, :]
        h2, u2p, gates, top_idx = _mix(
            y_dn.reshape(t, DN_WIDTH), y_na.reshape(t, NA_WIDTH), proj, h,
            w_branch_dn[layer].astype(BF16), w_branch_na[layer].astype(BF16), w_out[layer].astype(BF16),
            norm_ffn[layer][None, :], w_r, b_r, tm_mix)
        blk_exp, n_valid, idx_blocks = _routing_metadata(top_idx, n_blocks)
        y4 = _moe(blk_exp, n_valid, idx_blocks, u2p,
                  w_exp_in[layer].astype(BF16), b_exp_in[layer][:, None, :],
                  w_exp_out[layer].astype(BF16), b_exp_out[layer][:, None, :], t * TOP_K, (bsz + 1) * l * TOP_K)
        out = _final(h2.reshape(bsz, l, d), y4.reshape(bsz + 1, l * TOP_K * SUBLANES, LANES),
                     gates.reshape(bsz, l, LANES), norm_final[None, :], seq, tm_final)
    return out
```

```python
import functools

import jax
import jax.numpy as jnp
import numpy as np
from jax import lax
from jax.experimental import pallas as pl
from jax.experimental.pallas import tpu as pltpu

F32 = jnp.float32
BF16 = jnp.bfloat16
I32 = jnp.int32
U32 = jnp.uint32

D_MODEL = 2048
N_META = 16
GRID_W = 64
DN_HEADS = 8
DN_HEAD_DIM = 128
DN_WIDTH = 1024
DN_CONV = 5
DN_CHUNK = 64
NA_HEADS = 8
NA_HEAD_DIM = 128
NA_WIDTH = 1024
NA_WIN_R = 8
NA_WIN_C = 16
N_EXPERTS = 32
TOP_K = 4
D_EXPERT = 2048
SWIGLU_ALPHA = 1.702
SWIGLU_LIMIT = 7.0
NORM_EPS = 1e-6

LANES = 128
VMEM_LIMIT = 56 * 1024 * 1024

COL_DQ, COL_DK, COL_DV, COL_DZ = 0, 8, 16, 24
COL_GA, COL_GB = 32, 48
COL_NQ, COL_NK, COL_NV = 64, 72, 80
N_MAIN_COLS = 88 * LANES

DN_NH = 2
DN_NG = DN_HEADS // DN_NH
DN_NCH = 2 * DN_NH
DN_PAD = (-N_META) % DN_CHUNK
DN_CHUNK_UNROLL = 3

DMA_ISSUE_UNROLL = 8

NA_ROW_GROUP = 4
NA_KEYS = 640

MOE_TM = 512
MOE_SUB = 256
HALF = D_MODEL // 2
SUBLANES = 8
assert HALF == SUBLANES * LANES


def _store_token_tiles(ref, packed, n_tok, base=0, group=SUBLANES):
    for s in range(SUBLANES):
        ref[pl.ds(base + s, n_tok, stride=group), :] = packed[:, s * LANES:(s + 1) * LANES]


def _load_token_tiles(ref, n_tok, base=0, group=SUBLANES):
    return jnp.concatenate([ref[pl.ds(base + s, n_tok, stride=group), :] for s in range(SUBLANES)], axis=1)


def _dot(a, b, **kw):
    return jnp.dot(a, b, preferred_element_type=F32, **kw)


def _dot_nt(a, b, **kw):
    return lax.dot_general(a, b, (((1,), (1,)), ((), ())), preferred_element_type=F32, **kw)


def _dot_tn(a, b, **kw):
    return lax.dot_general(a, b, (((0,), (0,)), ((), ())), preferred_element_type=F32, **kw)


def _sigmoid(x):
    return 1.0 / (1.0 + jnp.exp(-x))


def _pack_halves(x):
    n = x.shape[1] // 2
    lo = pltpu.bitcast(x[:, :n].astype(BF16).astype(F32), U32)
    hi = pltpu.bitcast(x[:, n:].astype(BF16).astype(F32), U32)
    return (hi & jnp.uint32(0xFFFF0000)) | (lo >> 16)


def _unpack_halves(p):
    lo = pltpu.bitcast(p << 16, F32)
    hi = pltpu.bitcast(p & jnp.uint32(0xFFFF0000), F32)
    return lo, hi


def _inproj_body(h_ref, g_ref, w_ref, wba_ref, o_ref, oba_ref, u_ref):
    @pl.when(pl.program_id(1) == 0)
    def _():
        x = h_ref[...]
        y = x * lax.rsqrt(jnp.mean(x * x, axis=-1, keepdims=True) + NORM_EPS) * g_ref[...]
        ub = y.astype(BF16)
        u_ref[...] = ub
        oba_ref[...] = _dot(ub, wba_ref[...])

    o_ref[...] = _dot(u_ref[...], w_ref[...]).astype(BF16)


def _inproj(h, gain, w_main, w_ba, tm, tn):
    t, d = h.shape
    n = w_main.shape[1]
    nba = w_ba.shape[1]
    return pl.pallas_call(
        _inproj_body,
        grid=(t // tm, n // tn),
        in_specs=[
            pl.BlockSpec((tm, d), lambda i, j: (i, 0)),
            pl.BlockSpec((1, d), lambda i, j: (0, 0)),
            pl.BlockSpec((d, tn), lambda i, j: (0, j)),
            pl.BlockSpec((d, nba), lambda i, j: (0, 0)),
        ],
        out_specs=[
            pl.BlockSpec((tm, tn), lambda i, j: (i, j)),
            pl.BlockSpec((tm, nba), lambda i, j: (i, 0)),
        ],
        out_shape=[jax.ShapeDtypeStruct((t, n), BF16), jax.ShapeDtypeStruct((t, nba), F32)],
        scratch_shapes=[pltpu.VMEM((tm, d), BF16)],
        compiler_params=pltpu.CompilerParams(
            dimension_semantics=("arbitrary", "arbitrary"), vmem_limit_bytes=VMEM_LIMIT),
        name="inproj",
    )(h, gain, w_main, w_ba)


def _pair_dot(lhs, rhs):
    w = rhs[0].shape[1]
    r0, r1 = rhs[0].astype(BF16), rhs[1].astype(BF16)
    zero = jnp.zeros(r0.shape, BF16)
    bd = jnp.concatenate([jnp.concatenate([r0, zero], axis=1), jnp.concatenate([zero, r1], axis=1)], axis=0)
    x = _dot(jnp.concatenate([lhs[0].astype(BF16), lhs[1].astype(BF16)], axis=1), bd)
    return x[:, :w], x[:, w:]


def _pair_neumann_inverse(a_pairs, nilpotency):
    c = a_pairs[0][0].shape[0]
    eye = (lax.broadcasted_iota(I32, (c, c), 0) == lax.broadcasted_iota(I32, (c, c), 1)).astype(F32)
    ns = [[-a for a in pair] for pair in a_pairs]
    ps = [[eye + x for x in n] for n in ns]
    ns = [list(_pair_dot(n, n)) for n in ns]
    steps = int(np.log2(nilpotency)) - 1
    for step in range(steps):
        if step < steps - 1:
            pns = [_pair_dot([jnp.concatenate([p[i], n[i]], axis=0) for i in range(2)], n) for p, n in zip(ps, ns)]
            ps = [[p[i] + pn[i][:c] for i in range(2)] for p, pn in zip(ps, pns)]
            ns = [[pn[i][c:] for i in range(2)] for pn in pns]
        else:
            pns = [_pair_dot(p, n) for p, n in zip(ps, ns)]
            ps = [[p[i] + pn[i] for i in range(2)] for p, pn in zip(ps, pns)]
    return ps


def _chunk_cumsum(x, half):
    rows = x.shape[0]
    r = lax.broadcasted_iota(I32, x.shape, 0)
    top = r < half
    s = 1
    while s < half:
        down = pltpu.roll(x, s, axis=0)
        up = pltpu.roll(x, rows - s, axis=0)
        x = x + jnp.where(top, jnp.where(r >= s, down, 0.0), jnp.where(r < rows - s, up, 0.0))
        s *= 2
    return x


def _dn_body(q_ref, k_ref, v_ref, z_ref, ba_ref, cq_ref, ck_ref, cv_ref, gp_ref, on_ref, y_ref,
             xq, xk, xv, qn, kn, vn, gbf, gbb, us, ws, qds, kdts, qks, cds, of_s, ob_s, s_s, *, seq_len):
    nh = DN_NH
    wd = nh * DN_HEAD_DIM
    c64 = DN_CHUNK
    n_chunks = (seq_len + DN_PAD) // c64
    n_rows = n_chunks * c64
    lead = 8 + DN_PAD

    for x in (xq, xk, xv):
        x[0:lead, :] = jnp.zeros((lead, wd), F32)
        x[lead + seq_len:lead + seq_len + 8, :] = jnp.zeros((8, wd), F32)
    gbf[0:DN_PAD, :] = jnp.zeros((DN_PAD, LANES), F32)
    gbb[0:DN_PAD, :] = jnp.zeros((DN_PAD, LANES), F32)

    neg_a = -jnp.exp(gp_ref[0, 0:1, :])
    dt_b = gp_ref[0, 1:2, :]

    def copy_rows(src0, pos0, n):
        src = pl.ds(src0, n)
        dst = pl.ds(pl.multiple_of(8 + pos0, 8), n)
        xq[dst, :] = q_ref[0, src, :].astype(F32)
        xk[dst, :] = k_ref[0, src, :].astype(F32)
        xv[dst, :] = v_ref[0, src, :].astype(F32)
        ba = ba_ref[0, src, :]
        lane = lax.broadcasted_iota(I32, ba.shape, 1)
        beta = _sigmoid(ba)
        sp_in = ba + dt_b
        softplus = jnp.maximum(sp_in, 0.0) + jnp.log(1.0 + jnp.exp(-jnp.abs(sp_in)))
        val = jnp.where(lane < 2 * nh, beta, jnp.where(lane < 4 * nh, neg_a * softplus, 0.0))
        dst_g = pl.ds(pl.multiple_of(pos0, 8), n)
        gbf[dst_g, :] = val
        gbb[dst_g, :] = pltpu.roll(val, LANES - nh, axis=1)

    n_grid = seq_len - N_META
    copy_rows(n_grid, DN_PAD, N_META)

    def copy_step(j, carry):
        copy_rows(pl.multiple_of(j * c64, c64), DN_PAD + N_META + j * c64, c64)
        return carry

    lax.fori_loop(0, n_grid // c64, copy_step, 0)

    row_iota = lax.broadcasted_iota(I32, (c64, wd), 0)

    def conv_step(c, carry):
        r0 = pl.multiple_of(c * c64, c64)
        live = (row_iota + r0) >= DN_PAD
        for x, cw, dst, kind in ((xq, cq_ref, qn, "q"), (xk, ck_ref, kn, "k"), (xv, cv_ref, vn, "v")):
            win = x[pl.ds(r0, c64 + 16), :]
            acc = jnp.zeros((c64, wd), F32)
            for j in range(DN_CONV):
                off = 8 - DN_CONV // 2 + j
                acc = acc + win[off:off + c64, :] * cw[j:j + 1, :]
            y = jnp.where(live, acc * _sigmoid(acc), 0.0)
            if kind == "v":
                dst[pl.ds(r0, c64), :] = y
            else:
                scale = DN_HEAD_DIM ** -0.5 if kind == "q" else 1.0
                for hh in range(nh):
                    yh = y[:, hh * LANES:(hh + 1) * LANES]
                    yn = yh * lax.rsqrt(jnp.sum(yh * yh, axis=-1, keepdims=True) + 1e-6)
                    dst[pl.ds(r0, c64), hh * LANES:(hh + 1) * LANES] = yn * scale
        return carry

    lax.fori_loop(0, n_chunks, conv_step, 0)

    c2 = 2 * c64
    ri = lax.broadcasted_iota(I32, (c2, c2), 0)
    ci = lax.broadcasted_iota(I32, (c2, c2), 1)
    fwd_blk = (ri < c64) & (ci < c64)
    bwd_blk = (ri >= c64) & (ci >= c64)
    incl = (fwd_blk & (ri >= ci)) | (bwd_blk & (ri <= ci))
    strict = (fwd_blk & (ri > ci)) | (bwd_blk & (ri < ci))
    lane_row = lax.broadcasted_iota(I32, (1, c2), 1)
    lane_blk = lax.broadcasted_iota(I32, (c2, c2), 1)

    def chunk_group_step(grp, carry):
        chunks = [grp * DN_CHUNK_UNROLL + j for j in range(DN_CHUNK_UNROLL)]
        r0s = [pl.multiple_of(c * c64, c64) for c in chunks]
        b0s = [pl.multiple_of(c * c2, c2) for c in chunks]
        gb2s, gam2s, gam2_ts, q2s, k2s, v2s, k2_ts, lhs_gs = [], [], [], [], [], [], [], []
        for r0 in r0s:
            gb2 = jnp.concatenate([gbf[pl.ds(r0, c64), :], gbb[pl.ds(r0, c64), :]], axis=0)
            gam2 = _chunk_cumsum(gb2, c64)
            gb2s.append(gb2)
            gam2s.append(gam2)
            gam2_ts.append(gam2.T)
            q2, k2, v2 = [], [], []
            for hh in range(nh):
                hs = slice(hh * LANES, (hh + 1) * LANES)
                q = qn[pl.ds(r0, c64), hs]
                k = kn[pl.ds(r0, c64), hs]
                v = vn[pl.ds(r0, c64), hs]
                q2.append(jnp.concatenate([q, q], axis=0))
                k2.append(jnp.concatenate([k, k], axis=0))
                v2.append(jnp.concatenate([v, v], axis=0))
            q2s.append(q2)
            k2s.append(k2)
            v2s.append(v2)
            k2_ts.append([x.T for x in k2])
            lhs_gs.append([jnp.concatenate([k2[hh], q2[hh]], axis=0) for hh in range(nh)])
        g2s = [_pair_dot(lhs_g, k2_t) for lhs_g, k2_t in zip(lhs_gs, k2_ts)]
        a_pairs, decays, e_gs, gam_rs, gam_lasts, beta_cs = [], [], [], [], [], []
        for j in range(DN_CHUNK_UNROLL):
            a_pair, decay, e_g, gam_r, gam_last, beta_c = [], [], [], [], [], []
            for hh in range(nh):
                gcol = 2 * nh + hh
                beta_c.append(gb2s[j][:, hh:hh + 1])
                gam_c = gam2s[j][:, gcol:gcol + 1]
                gam_r.append(gam2_ts[j][gcol:gcol + 1, :])
                gam_last.append((gam_c[c64 - 1:c64, :], gam_c[c64:c64 + 1, :]))
                decay.append(jnp.exp(jnp.where(incl, gam_c - gam_r[hh], -jnp.inf)))
                a_pair.append(jnp.where(strict, beta_c[hh] * g2s[j][hh][:c2] * decay[hh], 0.0))
                e_g.append(jnp.exp(gam_c))
            a_pairs.append(a_pair)
            decays.append(decay)
            e_gs.append(e_g)
            gam_rs.append(gam_r)
            gam_lasts.append(gam_last)
            beta_cs.append(beta_c)
        t_invs = _pair_neumann_inverse(a_pairs, c64)
        rhss = [[jnp.concatenate([beta_cs[j][hh] * v2s[j][hh], (beta_cs[j][hh] * e_gs[j][hh]) * k2s[j][hh]], axis=1)
                 for hh in range(nh)] for j in range(DN_CHUNK_UNROLL)]
        sols = [_pair_dot(t_inv, rhs) for t_inv, rhs in zip(t_invs, rhss)]
        for j, c in enumerate(chunks):
            b0 = b0s[j]
            for hh in range(nh):
                gl_f, gl_b = gam_lasts[j][hh]
                gl_row = jnp.where(lane_row < c64, gl_f, gl_b)
                us[hh, pl.ds(b0, c2), :] = sols[j][hh][:, :LANES].astype(BF16)
                ws[hh, pl.ds(b0, c2), :] = sols[j][hh][:, LANES:].astype(BF16)
                qds[hh, pl.ds(b0, c2), :] = (q2s[j][hh] * e_gs[j][hh]).astype(BF16)
                qks[hh, pl.ds(b0, c2), :] = (g2s[j][hh][c2:] * decays[j][hh]).astype(BF16)
                kdts[hh, pl.ds(b0, c2), :] = (k2_ts[j][hh] * jnp.exp(gl_row - gam_rs[j][hh])).astype(BF16)
                cds[hh, pl.ds(pl.multiple_of(c * 16, 16), 8), :] = jnp.broadcast_to(jnp.exp(gl_f), (8, LANES))
                cds[hh, pl.ds(pl.multiple_of(c * 16 + 8, 8), 8), :] = jnp.broadcast_to(jnp.exp(gl_b), (8, LANES))
        return carry

    assert n_chunks % DN_CHUNK_UNROLL == 0
    lax.fori_loop(0, n_chunks // DN_CHUNK_UNROLL, chunk_group_step, 0)

    s_s[...] = jnp.zeros(s_s.shape, F32)

    def rec_step(i, carry):
        cs = (i, n_chunks - 1 - i)
        r0s = [pl.multiple_of(c * c64, c64) for c in cs]
        b0s = [pl.multiple_of(c * c2, c2) for c in cs]
        rows = [pl.ds(pl.multiple_of(b0s[d] + d * c64, c64), c64) for d in range(2)]
        keeps = (lane_blk < c64, lane_blk >= c64)
        states = [[s_s[hh * 2 + d] for hh in range(nh)] for d in range(2)]
        wq_s = [_pair_dot([jnp.concatenate([ws[hh, rows[d], :], qds[hh, rows[d], :]], axis=0) for hh in range(nh)],
                          states[d]) for d in range(2)]
        vbs = [[(us[hh, rows[d], :].astype(F32) - wq_s[d][hh][:c64]).astype(BF16) for hh in range(nh)]
               for d in range(2)]
        lhs2 = [[jnp.concatenate([qks[hh, rows[d], :],
                                  jnp.where(keeps[d], kdts[hh, pl.ds(b0s[d], c2), :], jnp.zeros((), BF16))], axis=0)
                 for hh in range(nh)] for d in range(2)]
        r2 = [_pair_dot(lhs2[d], [jnp.concatenate([x, x], axis=0) for x in vbs[d]]) for d in range(2)]
        for d in range(2):
            for hh in range(nh):
                cd = cds[hh, pl.ds(pl.multiple_of(cs[d] * 16 + d * 8, 8), 8), :][0:1, :]
                s_s[hh * 2 + d] = states[d][hh] * cd + r2[d][hh][c64:]
                (of_s if d == 0 else ob_s)[pl.ds(r0s[d], c64), hh * LANES:(hh + 1) * LANES] = (
                    wq_s[d][hh][c64:] + r2[d][hh][:c64])
        return carry

    lax.fori_loop(0, n_chunks, rec_step, 0)

    gain = on_ref[...]

    def emit(o, z):
        outs = []
        for hh in range(nh):
            oh = o[:, hh * LANES:(hh + 1) * LANES]
            outs.append(oh * lax.rsqrt(jnp.mean(oh * oh, axis=-1, keepdims=True) + NORM_EPS) * gain)
        on = jnp.concatenate(outs, axis=1)
        zf = z.astype(F32)
        return (on * (zf * _sigmoid(zf))).astype(BF16)

    o0 = of_s[DN_PAD:c64, :] + ob_s[DN_PAD:c64, :]
    y_ref[0, n_grid:seq_len, :] = emit(o0, z_ref[0, n_grid:seq_len, :])

    def out_step(c, carry):
        r0 = pl.multiple_of(c * c64, c64)
        l0 = pl.multiple_of((c - 1) * c64, c64)
        o = of_s[pl.ds(r0, c64), :] + ob_s[pl.ds(r0, c64), :]
        y_ref[0, pl.ds(l0, c64), :] = emit(o, z_ref[0, pl.ds(l0, c64), :])
        return carry

    lax.fori_loop(1, n_chunks, out_step, 0)


def _deltanet(proj3, ba3, conv_w, gpar, out_norm):
    b, l, _ = proj3.shape
    nh = DN_NH
    wd = nh * DN_HEAD_DIM
    n_rows = l + DN_PAD
    n_chunks = n_rows // DN_CHUNK
    ng = DN_NG
    blk = lambda col0: pl.BlockSpec((1, l, wd), lambda i, g, col0=col0: (i, 0, col0 // nh + g))
    cblk = lambda part: pl.BlockSpec((DN_CONV, wd), lambda i, g, part=part: (0, part * ng + g))
    return pl.pallas_call(
        functools.partial(_dn_body, seq_len=l),
        grid=(b, ng),
        in_specs=[
            blk(COL_DQ), blk(COL_DK), blk(COL_DV), blk(COL_DZ),
            pl.BlockSpec((1, l, LANES), lambda i, g: (i, 0, g)),
            cblk(0), cblk(1), cblk(2),
            pl.BlockSpec((1, 2, LANES), lambda i, g: (g, 0, 0)),
            pl.BlockSpec((1, LANES), lambda i, g: (0, 0)),
        ],
        out_specs=pl.BlockSpec((1, l, wd), lambda i, g: (i, 0, g)),
        out_shape=jax.ShapeDtypeStruct((b, l, DN_WIDTH), BF16),
        scratch_shapes=[
            pltpu.VMEM((n_rows + 16, wd), F32), pltpu.VMEM((n_rows + 16, wd), F32),
            pltpu.VMEM((n_rows + 16, wd), F32),
            pltpu.VMEM((n_rows, wd), F32), pltpu.VMEM((n_rows, wd), F32), pltpu.VMEM((n_rows, wd), F32),
            pltpu.VMEM((n_rows, LANES), F32), pltpu.VMEM((n_rows, LANES), F32),
            pltpu.VMEM((nh, 2 * n_rows, LANES), BF16), pltpu.VMEM((nh, 2 * n_rows, LANES), BF16),
            pltpu.VMEM((nh, 2 * n_rows, LANES), BF16),
            pltpu.VMEM((nh, 2 * n_rows, LANES), BF16),
            pltpu.VMEM((nh, 2 * n_rows, LANES), BF16),
            pltpu.VMEM((nh, n_chunks * 16, LANES), F32),
            pltpu.VMEM((n_rows, wd), F32), pltpu.VMEM((n_rows, wd), F32),
            pltpu.VMEM((DN_NCH, LANES, LANES), F32),
        ],
        compiler_params=pltpu.CompilerParams(
            dimension_semantics=("arbitrary", "arbitrary"), vmem_limit_bytes=VMEM_LIMIT),
        name="deltanet",
    )(proj3, proj3, proj3, proj3, ba3, conv_w, conv_w, conv_w, gpar, out_norm)


def _na_body(q_ref, k_ref, v_ref, tb_ref, o_ref, *, rows):
    scale = NA_HEAD_DIM ** -0.5
    wr = NA_WIN_R
    nk = wr * GRID_W
    n_grid = rows * GRID_W
    qm = q_ref[0, n_grid:n_grid + N_META, :]
    km = k_ref[0, n_grid:n_grid + N_META, :]
    vm = v_ref[0, n_grid:n_grid + N_META, :]

    s = _dot_nt(qm, km) * scale
    p = jnp.exp(s - jnp.max(s, axis=-1, keepdims=True))
    o = _dot(p.astype(BF16), vm) / jnp.sum(p, axis=-1, keepdims=True)
    o_ref[0, n_grid:n_grid + N_META, :] = o.astype(BF16)

    pad = jnp.zeros((NA_KEYS - nk - N_META, LANES), BF16)

    def row_group_step(grp, carry):
        qs, ks, vs, biases, dsts = [], [], [], [], []
        for j in range(NA_ROW_GROUP):
            r = grp * NA_ROW_GROUP + j
            rs = jnp.clip(r - wr // 2, 0, rows - wr)
            dsts.append(pl.ds(pl.multiple_of(r * GRID_W, GRID_W), GRID_W))
            win = pl.ds(pl.multiple_of(rs * GRID_W, GRID_W), nk)
            qs.append(q_ref[0, dsts[j], :])
            ks.append(jnp.concatenate([k_ref[0, win, :], km, pad], axis=0))
            vs.append(jnp.concatenate([v_ref[0, win, :], vm, pad], axis=0))
            biases.append(tb_ref[0, rs - r + NA_WIN_R - 1])
        ss = [_dot_nt(qs[j], ks[j]) * scale + biases[j] for j in range(NA_ROW_GROUP)]
        ps = [jnp.exp(s - jnp.max(s, axis=-1, keepdims=True)) for s in ss]
        os_ = [_dot(ps[j].astype(BF16), vs[j]) / jnp.sum(ps[j], axis=-1, keepdims=True) for j in range(NA_ROW_GROUP)]
        for j in range(NA_ROW_GROUP):
            o_ref[0, dsts[j], :] = os_[j].astype(BF16)
        return carry

    assert rows % NA_ROW_GROUP == 0
    lax.fori_loop(0, rows // NA_ROW_GROUP, row_group_step, 0)


def _na_bias_table(rpb):
    wr, wc = NA_WIN_R, NA_WIN_C
    qc = np.arange(GRID_W)[:, None]
    kc = np.arange(GRID_W)[None, :]
    q_start = np.clip(qc - wc // 2, 0, GRID_W - wc)
    valid = (kc - q_start >= 0) & (kc - q_start < wc)
    dc = np.clip(kc - qc + wc - 1, 0, 2 * wc - 2)
    i0 = np.arange(wr)[:, None]
    jj = np.arange(wr)[None, :]
    dr = i0 + jj
    tb = rpb.astype(F32)[:, dr][:, :, :, dc]
    tb = jnp.where(valid[None, None, None], tb, -jnp.inf)
    tb = tb.transpose(0, 1, 3, 2, 4)
    tb = tb.reshape(NA_HEADS, wr, GRID_W, wr * GRID_W)
    meta = jnp.zeros(tb.shape[:3] + (N_META,), F32)
    fill = jnp.full(tb.shape[:3] + (NA_KEYS - wr * GRID_W - N_META,), -jnp.inf, F32)
    return jnp.concatenate([tb, meta, fill], axis=-1)


def _natten(proj3, tb):
    b, l, _ = proj3.shape
    rows = (l - N_META) // GRID_W
    assert rows >= NA_WIN_R
    blk = lambda col0: pl.BlockSpec((1, l, LANES), lambda h, i, col0=col0: (i, 0, col0 + h))
    return pl.pallas_call(
        functools.partial(_na_body, rows=rows),
        grid=(NA_HEADS, b),
        in_specs=[
            blk(COL_NQ), blk(COL_NK), blk(COL_NV),
            pl.BlockSpec((1, NA_WIN_R, GRID_W, NA_KEYS), lambda h, i: (h, 0, 0, 0)),
        ],
        out_specs=pl.BlockSpec((1, l, LANES), lambda h, i: (i, 0, h)),
        out_shape=jax.ShapeDtypeStruct((b, l, NA_WIDTH), BF16),
        compiler_params=pltpu.CompilerParams(
            dimension_semantics=("arbitrary", "arbitrary"), vmem_limit_bytes=VMEM_LIMIT),
        name="natten",
    )(proj3, proj3, proj3, tb)


def _mix_body(ydn_ref, yna_ref, ga_ref, gb_ref, h_ref, wdn_ref, wna_ref, wo_ref, gn_ref, wr_ref, br_ref,
              h2_ref, u2_ref, gate_ref, idx_ref):
    a = _dot(ydn_ref[...], wdn_ref[...])
    b = _dot(yna_ref[...], wna_ref[...])
    mix = _sigmoid(ga_ref[...].astype(F32)) * a + _sigmoid(gb_ref[...].astype(F32)) * b
    h2 = h_ref[...] + _dot(mix.astype(BF16), wo_ref[...])
    h2_ref[...] = h2
    u2 = h2 * lax.rsqrt(jnp.mean(h2 * h2, axis=-1, keepdims=True) + NORM_EPS) * gn_ref[...]
    _store_token_tiles(u2_ref, _pack_halves(u2), u2.shape[0])
    u_hi = u2.astype(BF16)
    u_lo = (u2 - u_hi.astype(F32)).astype(BF16)
    w_r = wr_ref[...]
    w_hi = w_r.astype(BF16)
    w_lo = (w_r - w_hi.astype(F32)).astype(BF16)
    logits = _dot(u_hi, w_hi) + (_dot(u_hi, w_lo) + _dot(u_lo, w_hi)) + br_ref[...]
    lane = lax.broadcasted_iota(I32, logits.shape, 1)
    logits = jnp.where(lane < N_EXPERTS, logits, -jnp.inf)
    vals, idxs = [], []
    for _ in range(TOP_K):
        m = jnp.max(logits, axis=-1, keepdims=True)
        sel = jnp.min(jnp.where(logits == m, lane, LANES), axis=-1, keepdims=True)
        vals.append(m)
        idxs.append(sel)
        logits = jnp.where(lane == sel, -jnp.inf, logits)
    es = [jnp.exp(v - vals[0]) for v in vals]
    den = es[0] + es[1] + es[2] + es[3]
    gates = jnp.zeros(logits.shape, F32)
    idx = jnp.zeros(logits.shape, I32)
    for k in range(TOP_K):
        gates = jnp.where(lane == k, es[k] / den, gates)
        idx = jnp.where(lane == k, idxs[k], idx)
    gate_ref[...] = gates
    idx_ref[...] = idx


def _mix(y_dn, y_na, proj, h, w_dn, w_na, w_o, gain, w_r, b_r, tm):
    t, d = h.shape
    row = lambda w: pl.BlockSpec((tm, w), lambda i: (i, 0))
    full = lambda a: pl.BlockSpec(a.shape, lambda i: (0,) * a.ndim)
    return pl.pallas_call(
        _mix_body,
        grid=(t // tm,),
        in_specs=[
            row(DN_WIDTH), row(NA_WIDTH),
            pl.BlockSpec((tm, d), lambda i: (i, COL_GA * LANES // d)),
            pl.BlockSpec((tm, d), lambda i: (i, COL_GB * LANES // d)),
            row(d), full(w_dn), full(w_na), full(w_o), full(gain), full(w_r), full(b_r),
        ],
        out_specs=[row(d), pl.BlockSpec((tm * SUBLANES, LANES), lambda i: (i, 0)), row(LANES), row(LANES)],
        out_shape=[
            jax.ShapeDtypeStruct((t, d), F32), jax.ShapeDtypeStruct((t * SUBLANES, LANES), U32),
            jax.ShapeDtypeStruct((t, LANES), F32), jax.ShapeDtypeStruct((t, LANES), I32),
        ],
        compiler_params=pltpu.CompilerParams(
            dimension_semantics=("arbitrary",), vmem_limit_bytes=VMEM_LIMIT),
        name="mix",
    )(y_dn, y_na, proj, proj, h, w_dn, w_na, w_o, gain, w_r, b_r)


def _moe_body(blk_exp, n_valid, idx_hbm, u_hbm, wg_ref, wl_ref, bg_ref, bl_ref, wo_ref, bo_ref, y_hbm,
              idx_s, xbuf, xs, act_s, stage, isem, gsem, ssem, *, n_blocks, n_asg):
    i = pl.program_id(0)
    tm = MOE_TM
    n_idx = 2 * tm

    def idx_copy(blk, sl):
        return pltpu.make_async_copy(idx_hbm.at[pl.ds(pl.multiple_of(blk * n_idx, n_idx), n_idx)],
                                     idx_s.at[pl.ds(pl.multiple_of(sl * n_idx, n_idx), n_idx)], isem.at[sl])

    def tile(ref, row0):
        return ref.at[pl.ds(pl.multiple_of(row0, SUBLANES), SUBLANES), :]

    def gather_row(base, sl, r):
        src = idx_s[base + r]
        pltpu.make_async_copy(tile(u_hbm, src), tile(xbuf.at[sl], r * SUBLANES), gsem.at[sl]).start()

    def scatter_row(base, sl, r):
        dst = idx_s[base + tm + r]
        pltpu.make_async_copy(tile(stage.at[sl], r * SUBLANES), tile(y_hbm, dst), ssem).start()

    def issue_loop(row_fn, base, sl):
        def body(r, carry):
            row_fn(base, sl, r)
            return carry
        lax.fori_loop(0, tm, body, 0, unroll=DMA_ISSUE_UNROLL)

    def gather_wait(sl):
        pltpu.make_async_copy(xbuf.at[sl], xbuf.at[sl], gsem.at[sl]).wait()

    def scatter_wait():
        pltpu.make_async_copy(stage.at[0], stage.at[0], ssem).wait()

    live = n_valid[i] > 0
    prev_live = (i > 0) & (n_valid[jnp.maximum(i - 1, 0)] > 0)
    slot = i % 2
    own_base = (i % 4) * n_idx
    prev_base = ((i + 3) % 4) * n_idx
    prev_slot = (i + 1) % 2

    @pl.when(i == 0)
    def _():
        stage[...] = jnp.zeros(stage.shape, U32)
        n_stage = tm * SUBLANES
        tail0, tail1 = n_asg * SUBLANES, y_hbm.shape[0]
        fills = [pltpu.make_async_copy(stage.at[0, pl.ds(0, min(n_stage, tail1 - s0)), :],
                                       y_hbm.at[pl.ds(s0, min(n_stage, tail1 - s0)), :], ssem)
                 for s0 in range(tail0, tail1, n_stage)]
        for fill in fills:
            fill.start()
        for fill in fills:
            fill.wait()
        idx_copy(0, 0).start()
        idx_copy(n_blocks, 3).start()
        idx_copy(0, 0).wait()
        idx_copy(n_blocks, 3).wait()
        issue_loop(gather_row, 0, 0)
        idx_copy(1, 1).start()

    @pl.when(i + 1 < n_blocks)
    def _():
        idx_copy(i + 1, (i + 1) % 4).wait()

    @pl.when(i + 2 < n_blocks)
    def _():
        idx_copy(i + 2, (i + 2) % 4).start()

    @pl.when((i == 0) | prev_live)
    def _():
        gather_wait(slot)

    @pl.when(jnp.logical_not(live) & prev_live)
    def _():
        scatter_wait()
        issue_loop(scatter_row, prev_base, prev_slot)
        scatter_wait()

    nxt = jnp.minimum(i + 1, n_blocks - 1)
    nxt_base = (nxt % 4) * n_idx
    nxt_slot = (i + 1) % 2
    n_groups = D_EXPERT // MOE_SUB
    rows_per_group = tm // n_groups
    n_pairs = HALF // MOE_SUB
    rows_per_pair = tm // n_pairs

    @pl.when(live)
    def _():
        for s in range(SUBLANES):
            lo, hi = _unpack_halves(xbuf[slot, pl.ds(s, tm, stride=SUBLANES), :])
            xs[:, s * LANES:(s + 1) * LANES] = lo.astype(BF16)
            xs[:, HALF + s * LANES:HALF + (s + 1) * LANES] = hi.astype(BF16)
        x = xs[...]
        for s in range(n_groups):
            cols = slice(s * MOE_SUB, (s + 1) * MOE_SUB)
            hg = _dot(x, wg_ref[0, :, cols]) + bg_ref[0][:, cols]
            hl = _dot(x, wl_ref[0, :, cols]) + bl_ref[0][:, cols]
            for r in range(s * rows_per_group, (s + 1) * rows_per_group):
                gather_row(nxt_base, nxt_slot, r)
            glu = jnp.minimum(hg, SWIGLU_LIMIT)
            lin = jnp.clip(hl, -SWIGLU_LIMIT, SWIGLU_LIMIT)
            act_s[:, cols] = (glu * _sigmoid(SWIGLU_ALPHA * glu) * (lin + 1.0)).astype(BF16)

        @pl.when(i > 0)
        def _():
            scatter_wait()

        act = act_s[...]
        tiles_per_sub = MOE_SUB // LANES
        for c in range(n_pairs):
            lo_cols = slice(c * MOE_SUB, (c + 1) * MOE_SUB)
            hi_cols = slice(HALF + c * MOE_SUB, HALF + (c + 1) * MOE_SUB)
            out_lo = _dot(act, wo_ref[0, :, lo_cols]) + bo_ref[0][:, lo_cols]
            out_hi = _dot(act, wo_ref[0, :, hi_cols]) + bo_ref[0][:, hi_cols]
            for r in range(c * rows_per_pair, (c + 1) * rows_per_pair):
                scatter_row(prev_base, prev_slot, r)
            packed = _pack_halves(jnp.concatenate([out_lo, out_hi], axis=1))
            for j in range(tiles_per_sub):
                stage[slot, pl.ds(c * tiles_per_sub + j, tm, stride=SUBLANES), :] = packed[:, j * LANES:(j + 1) * LANES]

        @pl.when(i == n_blocks - 1)
        def _():
            gather_wait(nxt_slot)
            scatter_wait()
            issue_loop(scatter_row, own_base, slot)
            scatter_wait()


def _moe(blk_exp, n_valid, idx_blocks, u2p, w_in, b_in, w_out, b_out, n_asg, n_rows_out):
    n_blocks = blk_exp.shape[0]
    d = D_MODEL
    f = D_EXPERT

    def half(g):
        return lambda i, be, nv: (be[i], 0, g)

    grid_spec = pltpu.PrefetchScalarGridSpec(
        num_scalar_prefetch=2,
        grid=(n_blocks,),
        in_specs=[
            pl.BlockSpec(memory_space=pl.ANY),
            pl.BlockSpec(memory_space=pl.ANY),
            pl.BlockSpec((1, d, f), half(0), pipeline_mode=pl.Buffered(1)),
            pl.BlockSpec((1, d, f), half(1), pipeline_mode=pl.Buffered(1)),
            pl.BlockSpec((1, 1, f), half(0)),
            pl.BlockSpec((1, 1, f), half(1)),
            pl.BlockSpec((1, f, d), lambda i, be, nv: (be[i], 0, 0)),
            pl.BlockSpec((1, 1, d), lambda i, be, nv: (be[i], 0, 0)),
        ],
        out_specs=pl.BlockSpec(memory_space=pl.ANY),
        scratch_shapes=[
            pltpu.SMEM((4 * 2 * MOE_TM,), I32),
            pltpu.VMEM((2, MOE_TM * SUBLANES, LANES), U32),
            pltpu.VMEM((MOE_TM, d), BF16),
            pltpu.VMEM((MOE_TM, f), BF16),
            pltpu.VMEM((2, MOE_TM * SUBLANES, LANES), U32),
            pltpu.SemaphoreType.DMA((4,)),
            pltpu.SemaphoreType.DMA((2,)),
            pltpu.SemaphoreType.DMA,
        ],
    )
    return pl.pallas_call(
        functools.partial(_moe_body, n_blocks=n_blocks, n_asg=n_asg),
        grid_spec=grid_spec,
        out_shape=jax.ShapeDtypeStruct((n_rows_out * SUBLANES, LANES), U32),
        compiler_params=pltpu.CompilerParams(
            dimension_semantics=("arbitrary",), vmem_limit_bytes=VMEM_LIMIT),
        name="moe",
    )(blk_exp, n_valid, idx_blocks, u2p, w_in, w_in, b_in, b_in, w_out, b_out)


def _routing_metadata(top_idx, n_blocks):
    tm = MOE_TM
    e = top_idx[:, :TOP_K].reshape(-1)
    n_asg = e.shape[0]
    order = jnp.argsort(e, stable=True).astype(I32)
    counts = jnp.sum((e[:, None] == jnp.arange(N_EXPERTS, dtype=I32)[None, :]).astype(I32), axis=0)
    cstart = jnp.cumsum(counts) - counts
    nblk = (counts + tm - 1) // tm
    bend = jnp.cumsum(nblk)
    bstart = bend - nblk
    blk = jnp.arange(n_blocks, dtype=I32)
    blk_exp = jnp.minimum(jnp.sum((bend[None, :] <= blk[:, None]).astype(I32), axis=1), N_EXPERTS - 1)
    first = (blk - bstart[blk_exp]) * tm
    n_valid = jnp.where(blk < bend[-1], jnp.clip(counts[blk_exp] - first, 0, tm), 0).astype(I32)
    r = jnp.arange(tm, dtype=I32)[None, :]
    valid = r < n_valid[:, None]
    asg = order[jnp.where(valid, (cstart[blk_exp] + first)[:, None] + r, 0)]
    src_row = (asg // TOP_K) * SUBLANES
    dst_row = jnp.where(valid, asg, n_asg + r) * SUBLANES
    src_row = jnp.concatenate([src_row, jnp.zeros((1, tm), I32)], axis=0)
    dst_row = jnp.concatenate([dst_row, (n_asg + r) * SUBLANES], axis=0)
    idx_blocks = jnp.concatenate([src_row, dst_row], axis=1).reshape(-1)
    return blk_exp, n_valid, idx_blocks


def _final_body(h_ref, y_ref, gate_ref, gn_ref, o_ref):
    h = h_ref[0]
    gates = gate_ref[0]
    y_ref = y_ref.at[0]
    lo_acc = jnp.zeros((h.shape[0], HALF), F32)
    hi_acc = jnp.zeros((h.shape[0], HALF), F32)
    for k in range(TOP_K):
        lo, hi = _unpack_halves(_load_token_tiles(y_ref, h.shape[0], k * SUBLANES, TOP_K * SUBLANES))
        g = gates[:, k:k + 1]
        lo_acc = lo_acc + g * lo
        hi_acc = hi_acc + g * hi
    y = h + jnp.concatenate([lo_acc, hi_acc], axis=1)
    o_ref[0] = y * lax.rsqrt(jnp.mean(y * y, axis=-1, keepdims=True) + NORM_EPS) * gn_ref[...]


def _final(h2, y4, gates, gain, n_out, tm):
    b, _, d = h2.shape
    return pl.pallas_call(
        _final_body,
        grid=(b, n_out // tm),
        in_specs=[
            pl.BlockSpec((1, tm, d), lambda i, j: (i, j, 0)),
            pl.BlockSpec((1, tm * TOP_K * SUBLANES, LANES), lambda i, j: (i, j, 0)),
            pl.BlockSpec((1, tm, LANES), lambda i, j: (i, j, 0)),
            pl.BlockSpec((1, d), lambda i, j: (0, 0)),
        ],
        out_specs=pl.BlockSpec((1, tm, d), lambda i, j: (i, j, 0)),
        out_shape=jax.ShapeDtypeStruct((b, n_out, d), F32),
        compiler_params=pltpu.CompilerParams(
            dimension_semantics=("arbitrary", "arbitrary"), vmem_limit_bytes=VMEM_LIMIT),
        name="final",
    )(h2, y4, gates, gain)


def _split_w_in(w_in):
    n_dn = 4 * DN_WIDTH
    n_na = n_dn + 4 * DN_HEADS
    w_main = jnp.concatenate(
        [w_in[:, :n_dn], w_in[:, n_na + 3 * NA_WIDTH:], w_in[:, n_na:n_na + 3 * NA_WIDTH]], axis=1).astype(BF16)
    wb = w_in[:, n_dn:n_dn + 2 * DN_HEADS]
    wa = w_in[:, n_dn + 2 * DN_HEADS:n_dn + 4 * DN_HEADS]
    return w_main, _group_lanes(wb, wa).astype(BF16)


def _group_lanes(b_part, a_part):
    nh, ng = DN_NH, DN_NG
    lead = b_part.shape[:-1]
    bp = b_part.reshape(lead + (2, ng, nh))
    ap = a_part.reshape(lead + (2, ng, nh))
    perm = tuple(range(len(lead))) + (len(lead) + 1, len(lead), len(lead) + 2)
    bp = bp.transpose(perm).reshape(lead + (ng, 2 * nh))
    ap = ap.transpose(perm).reshape(lead + (ng, 2 * nh))
    both = jnp.concatenate([bp, ap], axis=-1)
    pad = [(0, 0)] * (both.ndim - 1) + [(0, LANES - 4 * nh)]
    return jnp.pad(both, pad).reshape(lead + (ng * LANES,))


def kernel(x, meta_tokens, norm_mix, w_in, dn_conv, dn_a_log, dn_dt_bias, dn_out_norm, na_rpb,
           w_branch_dn, w_branch_na, w_out, norm_ffn, w_router, b_router, w_exp_in, b_exp_in,
           w_exp_out, b_exp_out, norm_final):
    bsz, seq, d = x.shape
    l = seq + N_META
    t = bsz * l
    meta = jnp.broadcast_to(meta_tokens[None].astype(x.dtype), (bsz, N_META, d))
    h = jnp.concatenate([x, meta], axis=1).reshape(t, d)
    depth = w_in.shape[0]
    assert depth == 1, "the final kernel fuses the last residual add with the final RMSNorm"
    tm_proj = 768 if t % 768 == 0 else 16 * (l // 16)
    tm_mix = 384 if t % 384 == 0 else 8 * (l // 8)
    tm_final = 512 if seq % 512 == 0 else seq
    n_blocks = t * TOP_K // MOE_TM + N_EXPERTS
    for layer in range(depth):
        w_main, w_ba = _split_w_in(w_in[layer])
        proj, ba = _inproj(h, norm_mix[layer][None, :], w_main, w_ba, tm_proj, 1024)
        proj3 = proj.reshape(bsz, l, N_MAIN_COLS)
        ba3 = ba.reshape(bsz, l, DN_NG * LANES)
        gpar = jnp.stack([_group_lanes(jnp.zeros_like(dn_a_log[layer]).reshape(-1), dn_a_log[layer].reshape(-1)),
                          _group_lanes(jnp.zeros_like(dn_dt_bias[layer]).reshape(-1),
                                       dn_dt_bias[layer].reshape(-1))], axis=0)
        gpar = gpar.reshape(2, DN_NG, LANES).transpose(1, 0, 2)
        y_dn = _deltanet(proj3, ba3, dn_conv[layer], gpar, dn_out_norm[layer][None, :])
        y_na = _natten(proj3, _na_bias_table(na_rpb[layer]))
        w_r = jnp.pad(w_router[layer], ((0, 0), (0, LANES - N_EXPERTS)))
        b_r = jnp.pad(b_router[layer], (0, LANES - N_EXPERTS))[None, :]
        h2, u2p, gates, top_idx = _mix(
            y_dn.reshape(t, DN_WIDTH), y_na.reshape(t, NA_WIDTH), proj, h,
            w_branch_dn[layer].astype(BF16), w_branch_na[layer].astype(BF16), w_out[layer].astype(BF16),
            norm_ffn[layer][None, :], w_r, b_r, tm_mix)
        blk_exp, n_valid, idx_blocks = _routing_metadata(top_idx, n_blocks)
        y4 = _moe(blk_exp, n_valid, idx_blocks, u2p,
                  w_exp_in[layer].astype(BF16), b_exp_in[layer][:, None, :],
                  w_exp_out[layer].astype(BF16), b_exp_out[layer][:, None, :], t * TOP_K, (bsz + 1) * l * TOP_K)
        out = _final(h2.reshape(bsz, l, d), y4.reshape(bsz + 1, l * TOP_K * SUBLANES, LANES),
                     gates.reshape(bsz, l, LANES), norm_final[None, :], seq, tm_final)
    return out
```

```python
import functools

import jax
import jax.numpy as jnp
import numpy as np
from jax import lax
from jax.experimental import pallas as pl
from jax.experimental.pallas import tpu as pltpu

F32 = jnp.float32
BF16 = jnp.bfloat16
I32 = jnp.int32
U32 = jnp.uint32

D_MODEL = 2048
N_META = 16
GRID_W = 64
DN_HEADS = 8
DN_HEAD_DIM = 128
DN_WIDTH = 1024
DN_CONV = 5
DN_CHUNK = 64
NA_HEADS = 8
NA_HEAD_DIM = 128
NA_WIDTH = 1024
NA_WIN_R = 8
NA_WIN_C = 16
N_EXPERTS = 32
TOP_K = 4
D_EXPERT = 2048
SWIGLU_ALPHA = 1.702
SWIGLU_LIMIT = 7.0
NORM_EPS = 1e-6

LANES = 128
VMEM_LIMIT = 56 * 1024 * 1024

COL_DQ, COL_DK, COL_DV, COL_DZ = 0, 8, 16, 24
COL_GA, COL_GB = 32, 48
COL_NQ, COL_NK, COL_NV = 64, 72, 80
N_MAIN_COLS = 88 * LANES

DN_NH = 2
DN_NG = DN_HEADS // DN_NH
DN_NCH = 2 * DN_NH
DN_PAD = (-N_META) % DN_CHUNK
DN_CHUNK_UNROLL = 11

DMA_ISSUE_UNROLL = 8

NA_ROW_GROUP = 8
NA_KEYS = 640

MOE_TM = 512
MOE_SUB = 256
HALF = D_MODEL // 2
SUBLANES = 8
assert HALF == SUBLANES * LANES


def _store_token_tiles(ref, packed, n_tok, base=0, group=SUBLANES):
    for s in range(SUBLANES):
        ref[pl.ds(base + s, n_tok, stride=group), :] = packed[:, s * LANES:(s + 1) * LANES]


def _load_token_tiles(ref, n_tok, base=0, group=SUBLANES):
    return jnp.concatenate([ref[pl.ds(base + s, n_tok, stride=group), :] for s in range(SUBLANES)], axis=1)


def _dot(a, b, **kw):
    return jnp.dot(a, b, preferred_element_type=F32, **kw)


def _dot_nt(a, b, **kw):
    return lax.dot_general(a, b, (((1,), (1,)), ((), ())), preferred_element_type=F32, **kw)


def _dot_tn(a, b, **kw):
    return lax.dot_general(a, b, (((0,), (0,)), ((), ())), preferred_element_type=F32, **kw)


def _sigmoid(x):
    return 1.0 / (1.0 + jnp.exp(-x))


def _pack_halves(x):
    n = x.shape[1] // 2
    lo = pltpu.bitcast(x[:, :n].astype(BF16).astype(F32), U32)
    hi = pltpu.bitcast(x[:, n:].astype(BF16).astype(F32), U32)
    return (hi & jnp.uint32(0xFFFF0000)) | (lo >> 16)


def _unpack_halves(p):
    lo = pltpu.bitcast(p << 16, F32)
    hi = pltpu.bitcast(p & jnp.uint32(0xFFFF0000), F32)
    return lo, hi


def _inproj_body(h_ref, g_ref, w_ref, wba_ref, o_ref, oba_ref, u_ref):
    @pl.when(pl.program_id(1) == 0)
    def _():
        x = h_ref[...]
        y = x * lax.rsqrt(jnp.mean(x * x, axis=-1, keepdims=True) + NORM_EPS) * g_ref[...]
        ub = y.astype(BF16)
        u_ref[...] = ub
        oba_ref[...] = _dot(ub, wba_ref[...])

    o_ref[...] = _dot(u_ref[...], w_ref[...]).astype(BF16)


def _inproj(h, gain, w_main, w_ba, tm, tn):
    t, d = h.shape
    n = w_main.shape[1]
    nba = w_ba.shape[1]
    return pl.pallas_call(
        _inproj_body,
        grid=(t // tm, n // tn),
        in_specs=[
            pl.BlockSpec((tm, d), lambda i, j: (i, 0)),
            pl.BlockSpec((1, d), lambda i, j: (0, 0)),
            pl.BlockSpec((d, tn), lambda i, j: (0, j)),
            pl.BlockSpec((d, nba), lambda i, j: (0, 0)),
        ],
        out_specs=[
            pl.BlockSpec((tm, tn), lambda i, j: (i, j)),
            pl.BlockSpec((tm, nba), lambda i, j: (i, 0)),
        ],
        out_shape=[jax.ShapeDtypeStruct((t, n), BF16), jax.ShapeDtypeStruct((t, nba), F32)],
        scratch_shapes=[pltpu.VMEM((tm, d), BF16)],
        compiler_params=pltpu.CompilerParams(
            dimension_semantics=("arbitrary", "arbitrary"), vmem_limit_bytes=VMEM_LIMIT),
        name="inproj",
    )(h, gain, w_main, w_ba)


def _pair_dot(lhs, rhs):
    w = rhs[0].shape[1]
    r0, r1 = rhs[0].astype(BF16), rhs[1].astype(BF16)
    zero = jnp.zeros(r0.shape, BF16)
    bd = jnp.concatenate([jnp.concatenate([r0, zero], axis=1), jnp.concatenate([zero, r1], axis=1)], axis=0)
    x = _dot(jnp.concatenate([lhs[0].astype(BF16), lhs[1].astype(BF16)], axis=1), bd)
    return x[:, :w], x[:, w:]


def _pair_neumann_inverse(a_pairs, nilpotency):
    c = a_pairs[0][0].shape[0]
    eye = (lax.broadcasted_iota(I32, (c, c), 0) == lax.broadcasted_iota(I32, (c, c), 1)).astype(F32)
    ns = [[-a for a in pair] for pair in a_pairs]
    ps = [[eye + x for x in n] for n in ns]
    ns = [list(_pair_dot(n, n)) for n in ns]
    steps = int(np.log2(nilpotency)) - 1
    for step in range(steps):
        if step < steps - 1:
            pns = [_pair_dot([jnp.concatenate([p[i], n[i]], axis=0) for i in range(2)], n) for p, n in zip(ps, ns)]
            ps = [[p[i] + pn[i][:c] for i in range(2)] for p, pn in zip(ps, pns)]
            ns = [[pn[i][c:] for i in range(2)] for pn in pns]
        else:
            pns = [_pair_dot(p, n) for p, n in zip(ps, ns)]
            ps = [[p[i] + pn[i] for i in range(2)] for p, pn in zip(ps, pns)]
    return ps


def _chunk_cumsum(x, half):
    rows = x.shape[0]
    r = lax.broadcasted_iota(I32, x.shape, 0)
    top = r < half
    s = 1
    while s < half:
        down = pltpu.roll(x, s, axis=0)
        up = pltpu.roll(x, rows - s, axis=0)
        x = x + jnp.where(top, jnp.where(r >= s, down, 0.0), jnp.where(r < rows - s, up, 0.0))
        s *= 2
    return x


def _dn_body(q_ref, k_ref, v_ref, z_ref, ba_ref, cq_ref, ck_ref, cv_ref, gp_ref, on_ref, y_ref,
             xq, xk, xv, qn, kn, vn, gbf, gbb, us, ws, qds, kdts, qks, cds, of_s, ob_s, s_s, *, seq_len):
    nh = DN_NH
    wd = nh * DN_HEAD_DIM
    c64 = DN_CHUNK
    n_chunks = (seq_len + DN_PAD) // c64
    n_rows = n_chunks * c64
    lead = 8 + DN_PAD

    for x in (xq, xk, xv):
        x[0:lead, :] = jnp.zeros((lead, wd), F32)
        x[lead + seq_len:lead + seq_len + 8, :] = jnp.zeros((8, wd), F32)
    gbf[0:DN_PAD, :] = jnp.zeros((DN_PAD, LANES), F32)
    gbb[0:DN_PAD, :] = jnp.zeros((DN_PAD, LANES), F32)

    neg_a = -jnp.exp(gp_ref[0, 0:1, :])
    dt_b = gp_ref[0, 1:2, :]

    def copy_rows(src0, pos0, n):
        src = pl.ds(src0, n)
        dst = pl.ds(pl.multiple_of(8 + pos0, 8), n)
        xq[dst, :] = q_ref[0, src, :].astype(F32)
        xk[dst, :] = k_ref[0, src, :].astype(F32)
        xv[dst, :] = v_ref[0, src, :].astype(F32)
        ba = ba_ref[0, src, :]
        lane = lax.broadcasted_iota(I32, ba.shape, 1)
        beta = _sigmoid(ba)
        sp_in = ba + dt_b
        softplus = jnp.maximum(sp_in, 0.0) + jnp.log(1.0 + jnp.exp(-jnp.abs(sp_in)))
        val = jnp.where(lane < 2 * nh, beta, jnp.where(lane < 4 * nh, neg_a * softplus, 0.0))
        dst_g = pl.ds(pl.multiple_of(pos0, 8), n)
        gbf[dst_g, :] = val
        gbb[dst_g, :] = pltpu.roll(val, LANES - nh, axis=1)

    n_grid = seq_len - N_META
    copy_rows(n_grid, DN_PAD, N_META)

    def copy_step(j, carry):
        copy_rows(pl.multiple_of(j * c64, c64), DN_PAD + N_META + j * c64, c64)
        return carry

    lax.fori_loop(0, n_grid // c64, copy_step, 0)

    row_iota = lax.broadcasted_iota(I32, (c64, wd), 0)

    def conv_step(c, carry):
        r0 = pl.multiple_of(c * c64, c64)
        live = (row_iota + r0) >= DN_PAD
        for x, cw, dst, kind in ((xq, cq_ref, qn, "q"), (xk, ck_ref, kn, "k"), (xv, cv_ref, vn, "v")):
            win = x[pl.ds(r0, c64 + 16), :]
            acc = jnp.zeros((c64, wd), F32)
            for j in range(DN_CONV):
                off = 8 - DN_CONV // 2 + j
                acc = acc + win[off:off + c64, :] * cw[j:j + 1, :]
            y = jnp.where(live, acc * _sigmoid(acc), 0.0)
            if kind == "v":
                dst[pl.ds(r0, c64), :] = y
            else:
                scale = DN_HEAD_DIM ** -0.5 if kind == "q" else 1.0
                for hh in range(nh):
                    yh = y[:, hh * LANES:(hh + 1) * LANES]
                    yn = yh * lax.rsqrt(jnp.sum(yh * yh, axis=-1, keepdims=True) + 1e-6)
                    dst[pl.ds(r0, c64), hh * LANES:(hh + 1) * LANES] = yn * scale
        return carry

    lax.fori_loop(0, n_chunks, conv_step, 0)

    c2 = 2 * c64
    ri = lax.broadcasted_iota(I32, (c2, c2), 0)
    ci = lax.broadcasted_iota(I32, (c2, c2), 1)
    fwd_blk = (ri < c64) & (ci < c64)
    bwd_blk = (ri >= c64) & (ci >= c64)
    incl = (fwd_blk & (ri >= ci)) | (bwd_blk & (ri <= ci))
    strict = (fwd_blk & (ri > ci)) | (bwd_blk & (ri < ci))
    lane_row = lax.broadcasted_iota(I32, (1, c2), 1)
    lane_blk = lax.broadcasted_iota(I32, (c2, c2), 1)

    def chunk_group_step(grp, carry):
        chunks = [grp * DN_CHUNK_UNROLL + j for j in range(DN_CHUNK_UNROLL)]
        r0s = [pl.multiple_of(c * c64, c64) for c in chunks]
        b0s = [pl.multiple_of(c * c2, c2) for c in chunks]
        gb2s, gam2s, gam2_ts, q2s, k2s, v2s, k2_ts, lhs_gs = [], [], [], [], [], [], [], []
        for r0 in r0s:
            gb2 = jnp.concatenate([gbf[pl.ds(r0, c64), :], gbb[pl.ds(r0, c64), :]], axis=0)
            gam2 = _chunk_cumsum(gb2, c64)
            gb2s.append(gb2)
            gam2s.append(gam2)
            gam2_ts.append(gam2.T)
            q2, k2, v2 = [], [], []
            for hh in range(nh):
                hs = slice(hh * LANES, (hh + 1) * LANES)
                q = qn[pl.ds(r0, c64), hs]
                k = kn[pl.ds(r0, c64), hs]
                v = vn[pl.ds(r0, c64), hs]
                q2.append(jnp.concatenate([q, q], axis=0))
                k2.append(jnp.concatenate([k, k], axis=0))
                v2.append(jnp.concatenate([v, v], axis=0))
            q2s.append(q2)
            k2s.append(k2)
            v2s.append(v2)
            k2_ts.append([x.T for x in k2])
            lhs_gs.append([jnp.concatenate([k2[hh], q2[hh]], axis=0) for hh in range(nh)])
        g2s = [_pair_dot(lhs_g, k2_t) for lhs_g, k2_t in zip(lhs_gs, k2_ts)]
        a_pairs, decays, e_gs, gam_rs, gam_lasts, beta_cs = [], [], [], [], [], []
        for j in range(DN_CHUNK_UNROLL):
            a_pair, decay, e_g, gam_r, gam_last, beta_c = [], [], [], [], [], []
            for hh in range(nh):
                gcol = 2 * nh + hh
                beta_c.append(gb2s[j][:, hh:hh + 1])
                gam_c = gam2s[j][:, gcol:gcol + 1]
                gam_r.append(gam2_ts[j][gcol:gcol + 1, :])
                gam_last.append((gam_c[c64 - 1:c64, :], gam_c[c64:c64 + 1, :]))
                decay.append(jnp.exp(jnp.where(incl, gam_c - gam_r[hh], -jnp.inf)))
                a_pair.append(jnp.where(strict, beta_c[hh] * g2s[j][hh][:c2] * decay[hh], 0.0))
                e_g.append(jnp.exp(gam_c))
            a_pairs.append(a_pair)
            decays.append(decay)
            e_gs.append(e_g)
            gam_rs.append(gam_r)
            gam_lasts.append(gam_last)
            beta_cs.append(beta_c)
        t_invs = _pair_neumann_inverse(a_pairs, c64)
        rhss = [[jnp.concatenate([beta_cs[j][hh] * v2s[j][hh], (beta_cs[j][hh] * e_gs[j][hh]) * k2s[j][hh]], axis=1)
                 for hh in range(nh)] for j in range(DN_CHUNK_UNROLL)]
        sols = [_pair_dot(t_inv, rhs) for t_inv, rhs in zip(t_invs, rhss)]
        for j, c in enumerate(chunks):
            b0 = b0s[j]
            for hh in range(nh):
                gl_f, gl_b = gam_lasts[j][hh]
                gl_row = jnp.where(lane_row < c64, gl_f, gl_b)
                us[hh, pl.ds(b0, c2), :] = sols[j][hh][:, :LANES].astype(BF16)
                ws[hh, pl.ds(b0, c2), :] = sols[j][hh][:, LANES:].astype(BF16)
                qds[hh, pl.ds(b0, c2), :] = (q2s[j][hh] * e_gs[j][hh]).astype(BF16)
                qks[hh, pl.ds(b0, c2), :] = (g2s[j][hh][c2:] * decays[j][hh]).astype(BF16)
                kdts[hh, pl.ds(b0, c2), :] = (k2_ts[j][hh] * jnp.exp(gl_row - gam_rs[j][hh])).astype(BF16)
                cds[hh, pl.ds(pl.multiple_of(c * 16, 16), 8), :] = jnp.broadcast_to(jnp.exp(gl_f), (8, LANES))
                cds[hh, pl.ds(pl.multiple_of(c * 16 + 8, 8), 8), :] = jnp.broadcast_to(jnp.exp(gl_b), (8, LANES))
        return carry

    assert n_chunks % DN_CHUNK_UNROLL == 0
    lax.fori_loop(0, n_chunks // DN_CHUNK_UNROLL, chunk_group_step, 0)

    s_s[...] = jnp.zeros(s_s.shape, F32)

    def rec_step(i, carry):
        cs = (i, n_chunks - 1 - i)
        r0s = [pl.multiple_of(c * c64, c64) for c in cs]
        b0s = [pl.multiple_of(c * c2, c2) for c in cs]
        rows = [pl.ds(pl.multiple_of(b0s[d] + d * c64, c64), c64) for d in range(2)]
        keeps = (lane_blk < c64, lane_blk >= c64)
        states = [[s_s[hh * 2 + d] for hh in range(nh)] for d in range(2)]
        wq_s = [_pair_dot([jnp.concatenate([ws[hh, rows[d], :], qds[hh, rows[d], :]], axis=0) for hh in range(nh)],
                          states[d]) for d in range(2)]
        vbs = [[(us[hh, rows[d], :].astype(F32) - wq_s[d][hh][:c64]).astype(BF16) for hh in range(nh)]
               for d in range(2)]
        lhs2 = [[jnp.concatenate([qks[hh, rows[d], :],
                                  jnp.where(keeps[d], kdts[hh, pl.ds(b0s[d], c2), :], jnp.zeros((), BF16))], axis=0)
                 for hh in range(nh)] for d in range(2)]
        r2 = [_pair_dot(lhs2[d], [jnp.concatenate([x, x], axis=0) for x in vbs[d]]) for d in range(2)]
        for d in range(2):
            for hh in range(nh):
                cd = cds[hh, pl.ds(pl.multiple_of(cs[d] * 16 + d * 8, 8), 8), :][0:1, :]
                s_s[hh * 2 + d] = states[d][hh] * cd + r2[d][hh][c64:]
                (of_s if d == 0 else ob_s)[pl.ds(r0s[d], c64), hh * LANES:(hh + 1) * LANES] = (
                    wq_s[d][hh][c64:] + r2[d][hh][:c64])
        return carry

    lax.fori_loop(0, n_chunks, rec_step, 0)

    gain = on_ref[...]

    def emit(o, z):
        outs = []
        for hh in range(nh):
            oh = o[:, hh * LANES:(hh + 1) * LANES]
            outs.append(oh * lax.rsqrt(jnp.mean(oh * oh, axis=-1, keepdims=True) + NORM_EPS) * gain)
        on = jnp.concatenate(outs, axis=1)
        zf = z.astype(F32)
        return (on * (zf * _sigmoid(zf))).astype(BF16)

    o0 = of_s[DN_PAD:c64, :] + ob_s[DN_PAD:c64, :]
    y_ref[0, n_grid:seq_len, :] = emit(o0, z_ref[0, n_grid:seq_len, :])

    def out_step(c, carry):
        r0 = pl.multiple_of(c * c64, c64)
        l0 = pl.multiple_of((c - 1) * c64, c64)
        o = of_s[pl.ds(r0, c64), :] + ob_s[pl.ds(r0, c64), :]
        y_ref[0, pl.ds(l0, c64), :] = emit(o, z_ref[0, pl.ds(l0, c64), :])
        return carry

    lax.fori_loop(1, n_chunks, out_step, 0)


def _deltanet(proj3, ba3, conv_w, gpar, out_norm):
    b, l, _ = proj3.shape
    nh = DN_NH
    wd = nh * DN_HEAD_DIM
    n_rows = l + DN_PAD
    n_chunks = n_rows // DN_CHUNK
    ng = DN_NG
    blk = lambda col0: pl.BlockSpec((1, l, wd), lambda i, g, col0=col0: (i, 0, col0 // nh + g))
    cblk = lambda part: pl.BlockSpec((DN_CONV, wd), lambda i, g, part=part: (0, part * ng + g))
    return pl.pallas_call(
        functools.partial(_dn_body, seq_len=l),
        grid=(b, ng),
        in_specs=[
            blk(COL_DQ), blk(COL_DK), blk(COL_DV), blk(COL_DZ),
            pl.BlockSpec((1, l, LANES), lambda i, g: (i, 0, g)),
            cblk(0), cblk(1), cblk(2),
            pl.BlockSpec((1, 2, LANES), lambda i, g: (g, 0, 0)),
            pl.BlockSpec((1, LANES), lambda i, g: (0, 0)),
        ],
        out_specs=pl.BlockSpec((1, l, wd), lambda i, g: (i, 0, g)),
        out_shape=jax.ShapeDtypeStruct((b, l, DN_WIDTH), BF16),
        scratch_shapes=[
            pltpu.VMEM((n_rows + 16, wd), F32), pltpu.VMEM((n_rows + 16, wd), F32),
            pltpu.VMEM((n_rows + 16, wd), F32),
            pltpu.VMEM((n_rows, wd), F32), pltpu.VMEM((n_rows, wd), F32), pltpu.VMEM((n_rows, wd), F32),
            pltpu.VMEM((n_rows, LANES), F32), pltpu.VMEM((n_rows, LANES), F32),
            pltpu.VMEM((nh, 2 * n_rows, LANES), BF16), pltpu.VMEM((nh, 2 * n_rows, LANES), BF16),
            pltpu.VMEM((nh, 2 * n_rows, LANES), BF16),
            pltpu.VMEM((nh, 2 * n_rows, LANES), BF16),
            pltpu.VMEM((nh, 2 * n_rows, LANES), BF16),
            pltpu.VMEM((nh, n_chunks * 16, LANES), F32),
            pltpu.VMEM((n_rows, wd), F32), pltpu.VMEM((n_rows, wd), F32),
            pltpu.VMEM((DN_NCH, LANES, LANES), F32),
        ],
        compiler_params=pltpu.CompilerParams(
            dimension_semantics=("arbitrary", "arbitrary"), vmem_limit_bytes=VMEM_LIMIT),
        name="deltanet",
    )(proj3, proj3, proj3, proj3, ba3, conv_w, conv_w, conv_w, gpar, out_norm)


def _na_body(q_ref, k_ref, v_ref, tb_ref, o_ref, *, rows):
    scale = NA_HEAD_DIM ** -0.5
    wr = NA_WIN_R
    nk = wr * GRID_W
    n_grid = rows * GRID_W
    qm = q_ref[0, n_grid:n_grid + N_META, :]
    km = k_ref[0, n_grid:n_grid + N_META, :]
    vm = v_ref[0, n_grid:n_grid + N_META, :]

    s = _dot_nt(qm, km) * scale
    p = jnp.exp(s - jnp.max(s, axis=-1, keepdims=True))
    o = _dot(p.astype(BF16), vm) / jnp.sum(p, axis=-1, keepdims=True)
    o_ref[0, n_grid:n_grid + N_META, :] = o.astype(BF16)

    pad = jnp.zeros((NA_KEYS - nk - N_META, LANES), BF16)

    def row_group_step(grp, carry):
        qs, ks, vs, biases, dsts = [], [], [], [], []
        for j in range(NA_ROW_GROUP):
            r = grp * NA_ROW_GROUP + j
            rs = jnp.clip(r - wr // 2, 0, rows - wr)
            dsts.append(pl.ds(pl.multiple_of(r * GRID_W, GRID_W), GRID_W))
            win = pl.ds(pl.multiple_of(rs * GRID_W, GRID_W), nk)
            qs.append(q_ref[0, dsts[j], :])
            ks.append(jnp.concatenate([k_ref[0, win, :], km, pad], axis=0))
            vs.append(jnp.concatenate([v_ref[0, win, :], vm, pad], axis=0))
            biases.append(tb_ref[0, rs - r + NA_WIN_R - 1])
        ss = [_dot_nt(qs[j], ks[j]) * scale + biases[j] for j in range(NA_ROW_GROUP)]
        ps = [jnp.exp(s - jnp.max(s, axis=-1, keepdims=True)) for s in ss]
        os_ = [_dot(ps[j].astype(BF16), vs[j]) / jnp.sum(ps[j], axis=-1, keepdims=True) for j in range(NA_ROW_GROUP)]
        for j in range(NA_ROW_GROUP):
            o_ref[0, dsts[j], :] = os_[j].astype(BF16)
        return carry

    assert rows % NA_ROW_GROUP == 0
    lax.fori_loop(0, rows // NA_ROW_GROUP, row_group_step, 0)


def _na_bias_table(rpb):
    wr, wc = NA_WIN_R, NA_WIN_C
    qc = np.arange(GRID_W)[:, None]
    kc = np.arange(GRID_W)[None, :]
    q_start = np.clip(qc - wc // 2, 0, GRID_W - wc)
    valid = (kc - q_start >= 0) & (kc - q_start < wc)
    dc = np.clip(kc - qc + wc - 1, 0, 2 * wc - 2)
    i0 = np.arange(wr)[:, None]
    jj = np.arange(wr)[None, :]
    dr = i0 + jj
    tb = rpb.astype(F32)[:, dr][:, :, :, dc]
    tb = jnp.where(valid[None, None, None], tb, -jnp.inf)
    tb = tb.transpose(0, 1, 3, 2, 4)
    tb = tb.reshape(NA_HEADS, wr, GRID_W, wr * GRID_W)
    meta = jnp.zeros(tb.shape[:3] + (N_META,), F32)
    fill = jnp.full(tb.shape[:3] + (NA_KEYS - wr * GRID_W - N_META,), -jnp.inf, F32)
    return jnp.concatenate([tb, meta, fill], axis=-1)


def _natten(proj3, tb):
    b, l, _ = proj3.shape
    rows = (l - N_META) // GRID_W
    assert rows >= NA_WIN_R
    blk = lambda col0: pl.BlockSpec((1, l, LANES), lambda h, i, col0=col0: (i, 0, col0 + h))
    return pl.pallas_call(
        functools.partial(_na_body, rows=rows),
        grid=(NA_HEADS, b),
        in_specs=[
            blk(COL_NQ), blk(COL_NK), blk(COL_NV),
            pl.BlockSpec((1, NA_WIN_R, GRID_W, NA_KEYS), lambda h, i: (h, 0, 0, 0)),
        ],
        out_specs=pl.BlockSpec((1, l, LANES), lambda h, i: (i, 0, h)),
        out_shape=jax.ShapeDtypeStruct((b, l, NA_WIDTH), BF16),
        compiler_params=pltpu.CompilerParams(
            dimension_semantics=("arbitrary", "arbitrary"), vmem_limit_bytes=VMEM_LIMIT),
        name="natten",
    )(proj3, proj3, proj3, tb)


def _mix_body(ydn_ref, yna_ref, ga_ref, gb_ref, h_ref, wdn_ref, wna_ref, wo_ref, gn_ref, wr_ref, br_ref,
              h2_ref, u2_ref, gate_ref, idx_ref):
    a = _dot(ydn_ref[...], wdn_ref[...])
    b = _dot(yna_ref[...], wna_ref[...])
    mix = _sigmoid(ga_ref[...].astype(F32)) * a + _sigmoid(gb_ref[...].astype(F32)) * b
    h2 = h_ref[...] + _dot(mix.astype(BF16), wo_ref[...])
    h2_ref[...] = h2
    u2 = h2 * lax.rsqrt(jnp.mean(h2 * h2, axis=-1, keepdims=True) + NORM_EPS) * gn_ref[...]
    _store_token_tiles(u2_ref, _pack_halves(u2), u2.shape[0])
    u_hi = u2.astype(BF16)
    u_lo = (u2 - u_hi.astype(F32)).astype(BF16)
    w_r = wr_ref[...]
    w_hi = w_r.astype(BF16)
    w_lo = (w_r - w_hi.astype(F32)).astype(BF16)
    logits = _dot(u_hi, w_hi) + (_dot(u_hi, w_lo) + _dot(u_lo, w_hi)) + br_ref[...]
    lane = lax.broadcasted_iota(I32, logits.shape, 1)
    logits = jnp.where(lane < N_EXPERTS, logits, -jnp.inf)
    vals, idxs = [], []
    for _ in range(TOP_K):
        m = jnp.max(logits, axis=-1, keepdims=True)
        sel = jnp.min(jnp.where(logits == m, lane, LANES), axis=-1, keepdims=True)
        vals.append(m)
        idxs.append(sel)
        logits = jnp.where(lane == sel, -jnp.inf, logits)
    es = [jnp.exp(v - vals[0]) for v in vals]
    den = es[0] + es[1] + es[2] + es[3]
    gates = jnp.zeros(logits.shape, F32)
    idx = jnp.zeros(logits.shape, I32)
    for k in range(TOP_K):
        gates = jnp.where(lane == k, es[k] / den, gates)
        idx = jnp.where(lane == k, idxs[k], idx)
    gate_ref[...] = gates
    idx_ref[...] = idx


def _mix(y_dn, y_na, proj, h, w_dn, w_na, w_o, gain, w_r, b_r, tm):
    t, d = h.shape
    row = lambda w: pl.BlockSpec((tm, w), lambda i: (i, 0))
    full = lambda a: pl.BlockSpec(a.shape, lambda i: (0,) * a.ndim)
    return pl.pallas_call(
        _mix_body,
        grid=(t // tm,),
        in_specs=[
            row(DN_WIDTH), row(NA_WIDTH),
            pl.BlockSpec((tm, d), lambda i: (i, COL_GA * LANES // d)),
            pl.BlockSpec((tm, d), lambda i: (i, COL_GB * LANES // d)),
            row(d), full(w_dn), full(w_na), full(w_o), full(gain), full(w_r), full(b_r),
        ],
        out_specs=[row(d), pl.BlockSpec((tm * SUBLANES, LANES), lambda i: (i, 0)), row(LANES), row(LANES)],
        out_shape=[
            jax.ShapeDtypeStruct((t, d), F32), jax.ShapeDtypeStruct((t * SUBLANES, LANES), U32),
            jax.ShapeDtypeStruct((t, LANES), F32), jax.ShapeDtypeStruct((t, LANES), I32),
        ],
        compiler_params=pltpu.CompilerParams(
            dimension_semantics=("arbitrary",), vmem_limit_bytes=VMEM_LIMIT),
        name="mix",
    )(y_dn, y_na, proj, proj, h, w_dn, w_na, w_o, gain, w_r, b_r)


def _moe_body(blk_exp, n_valid, idx_hbm, u_hbm, wg_ref, wl_ref, bg_ref, bl_ref, wo_ref, bo_ref, y_hbm,
              idx_s, xbuf, xs, act_s, stage, isem, gsem, ssem, *, n_blocks, n_asg):
    i = pl.program_id(0)
    tm = MOE_TM
    n_idx = 2 * tm

    def idx_copy(blk, sl):
        return pltpu.make_async_copy(idx_hbm.at[pl.ds(pl.multiple_of(blk * n_idx, n_idx), n_idx)],
                                     idx_s.at[pl.ds(pl.multiple_of(sl * n_idx, n_idx), n_idx)], isem.at[sl])

    def tile(ref, row0):
        return ref.at[pl.ds(pl.multiple_of(row0, SUBLANES), SUBLANES), :]

    def gather_row(base, sl, r):
        src = idx_s[base + r]
        pltpu.make_async_copy(tile(u_hbm, src), tile(xbuf.at[sl], r * SUBLANES), gsem.at[sl]).start()

    def scatter_row(base, sl, r):
        dst = idx_s[base + tm + r]
        pltpu.make_async_copy(tile(stage.at[sl], r * SUBLANES), tile(y_hbm, dst), ssem).start()

    def issue_loop(row_fn, base, sl):
        def body(r, carry):
            row_fn(base, sl, r)
            return carry
        lax.fori_loop(0, tm, body, 0, unroll=DMA_ISSUE_UNROLL)

    def gather_wait(sl):
        pltpu.make_async_copy(xbuf.at[sl], xbuf.at[sl], gsem.at[sl]).wait()

    def scatter_wait():
        pltpu.make_async_copy(stage.at[0], stage.at[0], ssem).wait()

    live = n_valid[i] > 0
    prev_live = (i > 0) & (n_valid[jnp.maximum(i - 1, 0)] > 0)
    slot = i % 2
    own_base = (i % 4) * n_idx
    prev_base = ((i + 3) % 4) * n_idx
    prev_slot = (i + 1) % 2

    @pl.when(i == 0)
    def _():
        stage[...] = jnp.zeros(stage.shape, U32)
        n_stage = tm * SUBLANES
        tail0, tail1 = n_asg * SUBLANES, y_hbm.shape[0]
        fills = [pltpu.make_async_copy(stage.at[0, pl.ds(0, min(n_stage, tail1 - s0)), :],
                                       y_hbm.at[pl.ds(s0, min(n_stage, tail1 - s0)), :], ssem)
                 for s0 in range(tail0, tail1, n_stage)]
        for fill in fills:
            fill.start()
        for fill in fills:
            fill.wait()
        idx_copy(0, 0).start()
        idx_copy(n_blocks, 3).start()
        idx_copy(0, 0).wait()
        idx_copy(n_blocks, 3).wait()
        issue_loop(gather_row, 0, 0)
        idx_copy(1, 1).start()

    @pl.when(i + 1 < n_blocks)
    def _():
        idx_copy(i + 1, (i + 1) % 4).wait()

    @pl.when(i + 2 < n_blocks)
    def _():
        idx_copy(i + 2, (i + 2) % 4).start()

    @pl.when((i == 0) | prev_live)
    def _():
        gather_wait(slot)

    @pl.when(jnp.logical_not(live) & prev_live)
    def _():
        scatter_wait()
        issue_loop(scatter_row, prev_base, prev_slot)
        scatter_wait()

    nxt = jnp.minimum(i + 1, n_blocks - 1)
    nxt_base = (nxt % 4) * n_idx
    nxt_slot = (i + 1) % 2
    n_groups = D_EXPERT // MOE_SUB
    rows_per_group = tm // n_groups
    n_pairs = HALF // MOE_SUB
    rows_per_pair = tm // n_pairs

    @pl.when(live)
    def _():
        for s in range(SUBLANES):
            lo, hi = _unpack_halves(xbuf[slot, pl.ds(s, tm, stride=SUBLANES), :])
            xs[:, s * LANES:(s + 1) * LANES] = lo.astype(BF16)
            xs[:, HALF + s * LANES:HALF + (s + 1) * LANES] = hi.astype(BF16)
        x = xs[...]
        for s in range(n_groups):
            cols = slice(s * MOE_SUB, (s + 1) * MOE_SUB)
            hg = _dot(x, wg_ref[0, :, cols]) + bg_ref[0][:, cols]
            hl = _dot(x, wl_ref[0, :, cols]) + bl_ref[0][:, cols]
            for r in range(s * rows_per_group, (s + 1) * rows_per_group):
                gather_row(nxt_base, nxt_slot, r)
            glu = jnp.minimum(hg, SWIGLU_LIMIT)
            lin = jnp.clip(hl, -SWIGLU_LIMIT, SWIGLU_LIMIT)
            act_s[:, cols] = (glu * _sigmoid(SWIGLU_ALPHA * glu) * (lin + 1.0)).astype(BF16)

        @pl.when(i > 0)
        def _():
            scatter_wait()

        act = act_s[...]
        tiles_per_sub = MOE_SUB // LANES
        for c in range(n_pairs):
            lo_cols = slice(c * MOE_SUB, (c + 1) * MOE_SUB)
            hi_cols = slice(HALF + c * MOE_SUB, HALF + (c + 1) * MOE_SUB)
            out_lo = _dot(act, wo_ref[0, :, lo_cols]) + bo_ref[0][:, lo_cols]
            out_hi = _dot(act, wo_ref[0, :, hi_cols]) + bo_ref[0][:, hi_cols]
            for r in range(c * rows_per_pair, (c + 1) * rows_per_pair):
                scatter_row(prev_base, prev_slot, r)
            packed = _pack_halves(jnp.concatenate([out_lo, out_hi], axis=1))
            for j in range(tiles_per_sub):
                stage[slot, pl.ds(c * tiles_per_sub + j, tm, stride=SUBLANES), :] = packed[:, j * LANES:(j + 1) * LANES]

        @pl.when(i == n_blocks - 1)
        def _():
            gather_wait(nxt_slot)
            scatter_wait()
            issue_loop(scatter_row, own_base, slot)
            scatter_wait()


def _moe(blk_exp, n_valid, idx_blocks, u2p, w_in, b_in, w_out, b_out, n_asg, n_rows_out):
    n_blocks = blk_exp.shape[0]
    d = D_MODEL
    f = D_EXPERT

    def half(g):
        return lambda i, be, nv: (be[i], 0, g)

    grid_spec = pltpu.PrefetchScalarGridSpec(
        num_scalar_prefetch=2,
        grid=(n_blocks,),
        in_specs=[
            pl.BlockSpec(memory_space=pl.ANY),
            pl.BlockSpec(memory_space=pl.ANY),
            pl.BlockSpec((1, d, f), half(0), pipeline_mode=pl.Buffered(1)),
            pl.BlockSpec((1, d, f), half(1), pipeline_mode=pl.Buffered(1)),
            pl.BlockSpec((1, 1, f), half(0)),
            pl.BlockSpec((1, 1, f), half(1)),
            pl.BlockSpec((1, f, d), lambda i, be, nv: (be[i], 0, 0)),
            pl.BlockSpec((1, 1, d), lambda i, be, nv: (be[i], 0, 0)),
        ],
        out_specs=pl.BlockSpec(memory_space=pl.ANY),
        scratch_shapes=[
            pltpu.SMEM((4 * 2 * MOE_TM,), I32),
            pltpu.VMEM((2, MOE_TM * SUBLANES, LANES), U32),
            pltpu.VMEM((MOE_TM, d), BF16),
            pltpu.VMEM((MOE_TM, f), BF16),
            pltpu.VMEM((2, MOE_TM * SUBLANES, LANES), U32),
            pltpu.SemaphoreType.DMA((4,)),
            pltpu.SemaphoreType.DMA((2,)),
            pltpu.SemaphoreType.DMA,
        ],
    )
    return pl.pallas_call(
        functools.partial(_moe_body, n_blocks=n_blocks, n_asg=n_asg),
        grid_spec=grid_spec,
        out_shape=jax.ShapeDtypeStruct((n_rows_out * SUBLANES, LANES), U32),
        compiler_params=pltpu.CompilerParams(
            dimension_semantics=("arbitrary",), vmem_limit_bytes=VMEM_LIMIT),
        name="moe",
    )(blk_exp, n_valid, idx_blocks, u2p, w_in, w_in, b_in, b_in, w_out, b_out)


def _routing_metadata(top_idx, n_blocks):
    tm = MOE_TM
    e = top_idx[:, :TOP_K].reshape(-1)
    n_asg = e.shape[0]
    order = jnp.argsort(e, stable=True).astype(I32)
    counts = jnp.sum((e[:, None] == jnp.arange(N_EXPERTS, dtype=I32)[None, :]).astype(I32), axis=0)
    cstart = jnp.cumsum(counts) - counts
    nblk = (counts + tm - 1) // tm
    bend = jnp.cumsum(nblk)
    bstart = bend - nblk
    blk = jnp.arange(n_blocks, dtype=I32)
    blk_exp = jnp.minimum(jnp.sum((bend[None, :] <= blk[:, None]).astype(I32), axis=1), N_EXPERTS - 1)
    first = (blk - bstart[blk_exp]) * tm
    n_valid = jnp.where(blk < bend[-1], jnp.clip(counts[blk_exp] - first, 0, tm), 0).astype(I32)
    r = jnp.arange(tm, dtype=I32)[None, :]
    valid = r < n_valid[:, None]
    asg = order[jnp.where(valid, (cstart[blk_exp] + first)[:, None] + r, 0)]
    src_row = (asg // TOP_K) * SUBLANES
    dst_row = jnp.where(valid, asg, n_asg + r) * SUBLANES
    src_row = jnp.concatenate([src_row, jnp.zeros((1, tm), I32)], axis=0)
    dst_row = jnp.concatenate([dst_row, (n_asg + r) * SUBLANES], axis=0)
    idx_blocks = jnp.concatenate([src_row, dst_row], axis=1).reshape(-1)
    return blk_exp, n_valid, idx_blocks


def _final_body(h_ref, y_ref, gate_ref, gn_ref, o_ref):
    h = h_ref[0]
    gates = gate_ref[0]
    y_ref = y_ref.at[0]
    lo_acc = jnp.zeros((h.shape[0], HALF), F32)
    hi_acc = jnp.zeros((h.shape[0], HALF), F32)
    for k in range(TOP_K):
        lo, hi = _unpack_halves(_load_token_tiles(y_ref, h.shape[0], k * SUBLANES, TOP_K * SUBLANES))
        g = gates[:, k:k + 1]
        lo_acc = lo_acc + g * lo
        hi_acc = hi_acc + g * hi
    y = h + jnp.concatenate([lo_acc, hi_acc], axis=1)
    o_ref[0] = y * lax.rsqrt(jnp.mean(y * y, axis=-1, keepdims=True) + NORM_EPS) * gn_ref[...]


def _final(h2, y4, gates, gain, n_out, tm):
    b, _, d = h2.shape
    return pl.pallas_call(
        _final_body,
        grid=(b, n_out // tm),
        in_specs=[
            pl.BlockSpec((1, tm, d), lambda i, j: (i, j, 0)),
            pl.BlockSpec((1, tm * TOP_K * SUBLANES, LANES), lambda i, j: (i, j, 0)),
            pl.BlockSpec((1, tm, LANES), lambda i, j: (i, j, 0)),
            pl.BlockSpec((1, d), lambda i, j: (0, 0)),
        ],
        out_specs=pl.BlockSpec((1, tm, d), lambda i, j: (i, j, 0)),
        out_shape=jax.ShapeDtypeStruct((b, n_out, d), F32),
        compiler_params=pltpu.CompilerParams(
            dimension_semantics=("arbitrary", "arbitrary"), vmem_limit_bytes=VMEM_LIMIT),
        name="final",
    )(h2, y4, gates, gain)


def _split_w_in(w_in):
    n_dn = 4 * DN_WIDTH
    n_na = n_dn + 4 * DN_HEADS
    w_main = jnp.concatenate(
        [w_in[:, :n_dn], w_in[:, n_na + 3 * NA_WIDTH:], w_in[:, n_na:n_na + 3 * NA_WIDTH]], axis=1).astype(BF16)
    wb = w_in[:, n_dn:n_dn + 2 * DN_HEADS]
    wa = w_in[:, n_dn + 2 * DN_HEADS:n_dn + 4 * DN_HEADS]
    return w_main, _group_lanes(wb, wa).astype(BF16)


def _group_lanes(b_part, a_part):
    nh, ng = DN_NH, DN_NG
    lead = b_part.shape[:-1]
    bp = b_part.reshape(lead + (2, ng, nh))
    ap = a_part.reshape(lead + (2, ng, nh))
    perm = tuple(range(len(lead))) + (len(lead) + 1, len(lead), len(lead) + 2)
    bp = bp.transpose(perm).reshape(lead + (ng, 2 * nh))
    ap = ap.transpose(perm).reshape(lead + (ng, 2 * nh))
    both = jnp.concatenate([bp, ap], axis=-1)
    pad = [(0, 0)] * (both.ndim - 1) + [(0, LANES - 4 * nh)]
    return jnp.pad(both, pad).reshape(lead + (ng * LANES,))


def kernel(x, meta_tokens, norm_mix, w_in, dn_conv, dn_a_log, dn_dt_bias, dn_out_norm, na_rpb,
           w_branch_dn, w_branch_na, w_out, norm_ffn, w_router, b_router, w_exp_in, b_exp_in,
           w_exp_out, b_exp_out, norm_final):
    bsz, seq, d = x.shape
    l = seq + N_META
    t = bsz * l
    meta = jnp.broadcast_to(meta_tokens[None].astype(x.dtype), (bsz, N_META, d))
    h = jnp.concatenate([x, meta], axis=1).reshape(t, d)
    depth = w_in.shape[0]
    assert depth == 1, "the final kernel fuses the last residual add with the final RMSNorm"
    tm_proj = 768 if t % 768 == 0 else 16 * (l // 16)
    tm_mix = 384 if t % 384 == 0 else 8 * (l // 8)
    tm_final = 512 if seq % 512 == 0 else seq
    n_blocks = t * TOP_K // MOE_TM + N_EXPERTS
    for layer in range(depth):
        w_main, w_ba = _split_w_in(w_in[layer])
        proj, ba = _inproj(h, norm_mix[layer][None, :], w_main, w_ba, tm_proj, 1024)
        proj3 = proj.reshape(bsz, l, N_MAIN_COLS)
        ba3 = ba.reshape(bsz, l, DN_NG * LANES)
        gpar = jnp.stack([_group_lanes(jnp.zeros_like(dn_a_log[layer]).reshape(-1), dn_a_log[layer].reshape(-1)),
                          _group_lanes(jnp.zeros_like(dn_dt_bias[layer]).reshape(-1),
                                       dn_dt_bias[layer].reshape(-1))], axis=0)
        gpar = gpar.reshape(2, DN_NG, LANES).transpose(1, 0, 2)
        y_dn = _deltanet(proj3, ba3, dn_conv[layer], gpar, dn_out_norm[layer][None, :])
        y_na = _natten(proj3, _na_bias_table(na_rpb[layer]))
        w_r = jnp.pad(w_router[layer], ((0, 0), (0, LANES - N_EXPERTS)))
        b_r = jnp.pad(b_router[layer], (0, LANES - N_EXPERTS))[None, :]
        h2, u2p, gates, top_idx = _mix(
            y_dn.reshape(t, DN_WIDTH), y_na.reshape(t, NA_WIDTH), proj, h,
            w_branch_dn[layer].astype(BF16), w_branch_na[layer].astype(BF16), w_out[layer].astype(BF16),
            norm_ffn[layer][None, :], w_r, b_r, tm_mix)
        blk_exp, n_valid, idx_blocks = _routing_metadata(top_idx, n_blocks)
        y4 = _moe(blk_exp, n_valid, idx_blocks, u2p,
                  w_exp_in[layer].astype(BF16), b_exp_in[layer][:, None, :],
                  w_exp_out[layer].astype(BF16), b_exp_out[layer][:, None, :], t * TOP_K, (bsz + 1) * l * TOP_K)
        out = _final(h2.reshape(bsz, l, d), y4.reshape(bsz + 1, l * TOP_K * SUBLANES, LANES),
                     gates.reshape(bsz, l, LANES), norm_final[None, :], seq, tm_final)
    return out
```

```python
import functools

import jax
import jax.numpy as jnp
import numpy as np
from jax import lax
from jax.experimental import pallas as pl
from jax.experimental.pallas import tpu as pltpu

F32 = jnp.float32
BF16 = jnp.bfloat16
I32 = jnp.int32
U32 = jnp.uint32

D_MODEL = 2048
N_META = 16
GRID_W = 64
DN_HEADS = 8
DN_HEAD_DIM = 128
DN_WIDTH = 1024
DN_CONV = 5
DN_CHUNK = 64
NA_HEADS = 8
NA_HEAD_DIM = 128
NA_WIDTH = 1024
NA_WIN_R = 8
NA_WIN_C = 16
N_EXPERTS = 32
TOP_K = 4
D_EXPERT = 2048
SWIGLU_ALPHA = 1.702
SWIGLU_LIMIT = 7.0
NORM_EPS = 1e-6

LANES = 128
VMEM_LIMIT = 56 * 1024 * 1024

COL_DQ, COL_DK, COL_DV, COL_DZ = 0, 8, 16, 24
COL_GA, COL_GB = 32, 48
COL_NQ, COL_NK, COL_NV = 64, 72, 80
N_MAIN_COLS = 88 * LANES

DN_NH = 2
DN_NG = DN_HEADS // DN_NH
DN_NCH = 2 * DN_NH
DN_PAD = (-N_META) % DN_CHUNK
DN_CHUNK_UNROLL = 11

DMA_ISSUE_UNROLL = 8

NA_ROW_GROUP = 16
NA_KEYS = 640

MOE_TM = 512
MOE_SUB = 256
HALF = D_MODEL // 2
SUBLANES = 8
assert HALF == SUBLANES * LANES


def _store_token_tiles(ref, packed, n_tok, base=0, group=SUBLANES):
    for s in range(SUBLANES):
        ref[pl.ds(base + s, n_tok, stride=group), :] = packed[:, s * LANES:(s + 1) * LANES]


def _load_token_tiles(ref, n_tok, base=0, group=SUBLANES):
    return jnp.concatenate([ref[pl.ds(base + s, n_tok, stride=group), :] for s in range(SUBLANES)], axis=1)


def _dot(a, b, **kw):
    return jnp.dot(a, b, preferred_element_type=F32, **kw)


def _dot_nt(a, b, **kw):
    return lax.dot_general(a, b, (((1,), (1,)), ((), ())), preferred_element_type=F32, **kw)


def _dot_tn(a, b, **kw):
    return lax.dot_general(a, b, (((0,), (0,)), ((), ())), preferred_element_type=F32, **kw)


def _sigmoid(x):
    return 1.0 / (1.0 + jnp.exp(-x))


def _pack_halves(x):
    n = x.shape[1] // 2
    lo = pltpu.bitcast(x[:, :n].astype(BF16).astype(F32), U32)
    hi = pltpu.bitcast(x[:, n:].astype(BF16).astype(F32), U32)
    return (hi & jnp.uint32(0xFFFF0000)) | (lo >> 16)


def _unpack_halves(p):
    lo = pltpu.bitcast(p << 16, F32)
    hi = pltpu.bitcast(p & jnp.uint32(0xFFFF0000), F32)
    return lo, hi


def _inproj_body(h_ref, g_ref, w_ref, wba_ref, o_ref, oba_ref, u_ref):
    @pl.when(pl.program_id(1) == 0)
    def _():
        x = h_ref[...]
        y = x * lax.rsqrt(jnp.mean(x * x, axis=-1, keepdims=True) + NORM_EPS) * g_ref[...]
        ub = y.astype(BF16)
        u_ref[...] = ub
        oba_ref[...] = _dot(ub, wba_ref[...])

    o_ref[...] = _dot(u_ref[...], w_ref[...]).astype(BF16)


def _inproj(h, gain, w_main, w_ba, tm, tn):
    t, d = h.shape
    n = w_main.shape[1]
    nba = w_ba.shape[1]
    return pl.pallas_call(
        _inproj_body,
        grid=(t // tm, n // tn),
        in_specs=[
            pl.BlockSpec((tm, d), lambda i, j: (i, 0)),
            pl.BlockSpec((1, d), lambda i, j: (0, 0)),
            pl.BlockSpec((d, tn), lambda i, j: (0, j)),
            pl.BlockSpec((d, nba), lambda i, j: (0, 0)),
        ],
        out_specs=[
            pl.BlockSpec((tm, tn), lambda i, j: (i, j)),
            pl.BlockSpec((tm, nba), lambda i, j: (i, 0)),
        ],
        out_shape=[jax.ShapeDtypeStruct((t, n), BF16), jax.ShapeDtypeStruct((t, nba), F32)],
        scratch_shapes=[pltpu.VMEM((tm, d), BF16)],
        compiler_params=pltpu.CompilerParams(
            dimension_semantics=("arbitrary", "arbitrary"), vmem_limit_bytes=VMEM_LIMIT),
        name="inproj",
    )(h, gain, w_main, w_ba)


def _pair_dot(lhs, rhs):
    w = rhs[0].shape[1]
    r0, r1 = rhs[0].astype(BF16), rhs[1].astype(BF16)
    zero = jnp.zeros(r0.shape, BF16)
    bd = jnp.concatenate([jnp.concatenate([r0, zero], axis=1), jnp.concatenate([zero, r1], axis=1)], axis=0)
    x = _dot(jnp.concatenate([lhs[0].astype(BF16), lhs[1].astype(BF16)], axis=1), bd)
    return x[:, :w], x[:, w:]


def _pair_neumann_inverse(a_pairs, nilpotency):
    c = a_pairs[0][0].shape[0]
    eye = (lax.broadcasted_iota(I32, (c, c), 0) == lax.broadcasted_iota(I32, (c, c), 1)).astype(F32)
    ns = [[-a for a in pair] for pair in a_pairs]
    ps = [[eye + x for x in n] for n in ns]
    ns = [list(_pair_dot(n, n)) for n in ns]
    steps = int(np.log2(nilpotency)) - 1
    for step in range(steps):
        if step < steps - 1:
            pns = [_pair_dot([jnp.concatenate([p[i], n[i]], axis=0) for i in range(2)], n) for p, n in zip(ps, ns)]
            ps = [[p[i] + pn[i][:c] for i in range(2)] for p, pn in zip(ps, pns)]
            ns = [[pn[i][c:] for i in range(2)] for pn in pns]
        else:
            pns = [_pair_dot(p, n) for p, n in zip(ps, ns)]
            ps = [[p[i] + pn[i] for i in range(2)] for p, pn in zip(ps, pns)]
    return ps


def _chunk_cumsum(x, half):
    rows = x.shape[0]
    r = lax.broadcasted_iota(I32, x.shape, 0)
    top = r < half
    s = 1
    while s < half:
        down = pltpu.roll(x, s, axis=0)
        up = pltpu.roll(x, rows - s, axis=0)
        x = x + jnp.where(top, jnp.where(r >= s, down, 0.0), jnp.where(r < rows - s, up, 0.0))
        s *= 2
    return x


def _dn_body(q_ref, k_ref, v_ref, z_ref, ba_ref, cq_ref, ck_ref, cv_ref, gp_ref, on_ref, y_ref,
             xq, xk, xv, qn, kn, vn, gbf, gbb, us, ws, qds, kdts, qks, cds, of_s, ob_s, s_s, *, seq_len):
    nh = DN_NH
    wd = nh * DN_HEAD_DIM
    c64 = DN_CHUNK
    n_chunks = (seq_len + DN_PAD) // c64
    n_rows = n_chunks * c64
    lead = 8 + DN_PAD

    for x in (xq, xk, xv):
        x[0:lead, :] = jnp.zeros((lead, wd), F32)
        x[lead + seq_len:lead + seq_len + 8, :] = jnp.zeros((8, wd), F32)
    gbf[0:DN_PAD, :] = jnp.zeros((DN_PAD, LANES), F32)
    gbb[0:DN_PAD, :] = jnp.zeros((DN_PAD, LANES), F32)

    neg_a = -jnp.exp(gp_ref[0, 0:1, :])
    dt_b = gp_ref[0, 1:2, :]

    def copy_rows(src0, pos0, n):
        src = pl.ds(src0, n)
        dst = pl.ds(pl.multiple_of(8 + pos0, 8), n)
        xq[dst, :] = q_ref[0, src, :].astype(F32)
        xk[dst, :] = k_ref[0, src, :].astype(F32)
        xv[dst, :] = v_ref[0, src, :].astype(F32)
        ba = ba_ref[0, src, :]
        lane = lax.broadcasted_iota(I32, ba.shape, 1)
        beta = _sigmoid(ba)
        sp_in = ba + dt_b
        softplus = jnp.maximum(sp_in, 0.0) + jnp.log(1.0 + jnp.exp(-jnp.abs(sp_in)))
        val = jnp.where(lane < 2 * nh, beta, jnp.where(lane < 4 * nh, neg_a * softplus, 0.0))
        dst_g = pl.ds(pl.multiple_of(pos0, 8), n)
        gbf[dst_g, :] = val
        gbb[dst_g, :] = pltpu.roll(val, LANES - nh, axis=1)

    n_grid = seq_len - N_META
    copy_rows(n_grid, DN_PAD, N_META)

    def copy_step(j, carry):
        copy_rows(pl.multiple_of(j * c64, c64), DN_PAD + N_META + j * c64, c64)
        return carry

    lax.fori_loop(0, n_grid // c64, copy_step, 0)

    row_iota = lax.broadcasted_iota(I32, (c64, wd), 0)

    def conv_step(c, carry):
        r0 = pl.multiple_of(c * c64, c64)
        live = (row_iota + r0) >= DN_PAD
        for x, cw, dst, kind in ((xq, cq_ref, qn, "q"), (xk, ck_ref, kn, "k"), (xv, cv_ref, vn, "v")):
            win = x[pl.ds(r0, c64 + 16), :]
            acc = jnp.zeros((c64, wd), F32)
            for j in range(DN_CONV):
                off = 8 - DN_CONV // 2 + j
                acc = acc + win[off:off + c64, :] * cw[j:j + 1, :]
            y = jnp.where(live, acc * _sigmoid(acc), 0.0)
            if kind == "v":
                dst[pl.ds(r0, c64), :] = y
            else:
                scale = DN_HEAD_DIM ** -0.5 if kind == "q" else 1.0
                for hh in range(nh):
                    yh = y[:, hh * LANES:(hh + 1) * LANES]
                    yn = yh * lax.rsqrt(jnp.sum(yh * yh, axis=-1, keepdims=True) + 1e-6)
                    dst[pl.ds(r0, c64), hh * LANES:(hh + 1) * LANES] = yn * scale
        return carry

    lax.fori_loop(0, n_chunks, conv_step, 0)

    c2 = 2 * c64
    ri = lax.broadcasted_iota(I32, (c2, c2), 0)
    ci = lax.broadcasted_iota(I32, (c2, c2), 1)
    fwd_blk = (ri < c64) & (ci < c64)
    bwd_blk = (ri >= c64) & (ci >= c64)
    incl = (fwd_blk & (ri >= ci)) | (bwd_blk & (ri <= ci))
    strict = (fwd_blk & (ri > ci)) | (bwd_blk & (ri < ci))
    lane_row = lax.broadcasted_iota(I32, (1, c2), 1)
    lane_blk = lax.broadcasted_iota(I32, (c2, c2), 1)

    def chunk_group_step(grp, carry):
        chunks = [grp * DN_CHUNK_UNROLL + j for j in range(DN_CHUNK_UNROLL)]
        r0s = [pl.multiple_of(c * c64, c64) for c in chunks]
        b0s = [pl.multiple_of(c * c2, c2) for c in chunks]
        gb2s, gam2s, gam2_ts, q2s, k2s, v2s, k2_ts, lhs_gs = [], [], [], [], [], [], [], []
        for r0 in r0s:
            gb2 = jnp.concatenate([gbf[pl.ds(r0, c64), :], gbb[pl.ds(r0, c64), :]], axis=0)
            gam2 = _chunk_cumsum(gb2, c64)
            gb2s.append(gb2)
            gam2s.append(gam2)
            gam2_ts.append(gam2.T)
            q2, k2, v2 = [], [], []
            for hh in range(nh):
                hs = slice(hh * LANES, (hh + 1) * LANES)
                q = qn[pl.ds(r0, c64), hs]
                k = kn[pl.ds(r0, c64), hs]
                v = vn[pl.ds(r0, c64), hs]
                q2.append(jnp.concatenate([q, q], axis=0))
                k2.append(jnp.concatenate([k, k], axis=0))
                v2.append(jnp.concatenate([v, v], axis=0))
            q2s.append(q2)
            k2s.append(k2)
            v2s.append(v2)
            k2_ts.append([x.T for x in k2])
            lhs_gs.append([jnp.concatenate([k2[hh], q2[hh]], axis=0) for hh in range(nh)])
        g2s = [_pair_dot(lhs_g, k2_t) for lhs_g, k2_t in zip(lhs_gs, k2_ts)]
        a_pairs, decays, e_gs, gam_rs, gam_lasts, beta_cs = [], [], [], [], [], []
        for j in range(DN_CHUNK_UNROLL):
            a_pair, decay, e_g, gam_r, gam_last, beta_c = [], [], [], [], [], []
            for hh in range(nh):
                gcol = 2 * nh + hh
                beta_c.append(gb2s[j][:, hh:hh + 1])
                gam_c = gam2s[j][:, gcol:gcol + 1]
                gam_r.append(gam2_ts[j][gcol:gcol + 1, :])
                gam_last.append((gam_c[c64 - 1:c64, :], gam_c[c64:c64 + 1, :]))
                decay.append(jnp.exp(jnp.where(incl, gam_c - gam_r[hh], -jnp.inf)))
                a_pair.append(jnp.where(strict, beta_c[hh] * g2s[j][hh][:c2] * decay[hh], 0.0))
                e_g.append(jnp.exp(gam_c))
            a_pairs.append(a_pair)
            decays.append(decay)
            e_gs.append(e_g)
            gam_rs.append(gam_r)
            gam_lasts.append(gam_last)
            beta_cs.append(beta_c)
        t_invs = _pair_neumann_inverse(a_pairs, c64)
        rhss = [[jnp.concatenate([beta_cs[j][hh] * v2s[j][hh], (beta_cs[j][hh] * e_gs[j][hh]) * k2s[j][hh]], axis=1)
                 for hh in range(nh)] for j in range(DN_CHUNK_UNROLL)]
        sols = [_pair_dot(t_inv, rhs) for t_inv, rhs in zip(t_invs, rhss)]
        for j, c in enumerate(chunks):
            b0 = b0s[j]
            for hh in range(nh):
                gl_f, gl_b = gam_lasts[j][hh]
                gl_row = jnp.where(lane_row < c64, gl_f, gl_b)
                us[hh, pl.ds(b0, c2), :] = sols[j][hh][:, :LANES].astype(BF16)
                ws[hh, pl.ds(b0, c2), :] = sols[j][hh][:, LANES:].astype(BF16)
                qds[hh, pl.ds(b0, c2), :] = (q2s[j][hh] * e_gs[j][hh]).astype(BF16)
                qks[hh, pl.ds(b0, c2), :] = (g2s[j][hh][c2:] * decays[j][hh]).astype(BF16)
                kdts[hh, pl.ds(b0, c2), :] = (k2_ts[j][hh] * jnp.exp(gl_row - gam_rs[j][hh])).astype(BF16)
                cds[hh, pl.ds(pl.multiple_of(c * 16, 16), 8), :] = jnp.broadcast_to(jnp.exp(gl_f), (8, LANES))
                cds[hh, pl.ds(pl.multiple_of(c * 16 + 8, 8), 8), :] = jnp.broadcast_to(jnp.exp(gl_b), (8, LANES))
        return carry

    assert n_chunks % DN_CHUNK_UNROLL == 0
    lax.fori_loop(0, n_chunks // DN_CHUNK_UNROLL, chunk_group_step, 0)

    s_s[...] = jnp.zeros(s_s.shape, F32)

    def rec_step(i, carry):
        cs = (i, n_chunks - 1 - i)
        r0s = [pl.multiple_of(c * c64, c64) for c in cs]
        b0s = [pl.multiple_of(c * c2, c2) for c in cs]
        rows = [pl.ds(pl.multiple_of(b0s[d] + d * c64, c64), c64) for d in range(2)]
        keeps = (lane_blk < c64, lane_blk >= c64)
        states = [[s_s[hh * 2 + d] for hh in range(nh)] for d in range(2)]
        wq_s = [_pair_dot([jnp.concatenate([ws[hh, rows[d], :], qds[hh, rows[d], :]], axis=0) for hh in range(nh)],
                          states[d]) for d in range(2)]
        vbs = [[(us[hh, rows[d], :].astype(F32) - wq_s[d][hh][:c64]).astype(BF16) for hh in range(nh)]
               for d in range(2)]
        lhs2 = [[jnp.concatenate([qks[hh, rows[d], :],
                                  jnp.where(keeps[d], kdts[hh, pl.ds(b0s[d], c2), :], jnp.zeros((), BF16))], axis=0)
                 for hh in range(nh)] for d in range(2)]
        r2 = [_pair_dot(lhs2[d], [jnp.concatenate([x, x], axis=0) for x in vbs[d]]) for d in range(2)]
        for d in range(2):
            for hh in range(nh):
                cd = cds[hh, pl.ds(pl.multiple_of(cs[d] * 16 + d * 8, 8), 8), :][0:1, :]
                s_s[hh * 2 + d] = states[d][hh] * cd + r2[d][hh][c64:]
                (of_s if d == 0 else ob_s)[pl.ds(r0s[d], c64), hh * LANES:(hh + 1) * LANES] = (
                    wq_s[d][hh][c64:] + r2[d][hh][:c64])
        return carry

    lax.fori_loop(0, n_chunks, rec_step, 0)

    gain = on_ref[...]

    def emit(o, z):
        outs = []
        for hh in range(nh):
            oh = o[:, hh * LANES:(hh + 1) * LANES]
            outs.append(oh * lax.rsqrt(jnp.mean(oh * oh, axis=-1, keepdims=True) + NORM_EPS) * gain)
        on = jnp.concatenate(outs, axis=1)
        zf = z.astype(F32)
        return (on * (zf * _sigmoid(zf))).astype(BF16)

    o0 = of_s[DN_PAD:c64, :] + ob_s[DN_PAD:c64, :]
    y_ref[0, n_grid:seq_len, :] = emit(o0, z_ref[0, n_grid:seq_len, :])

    def out_step(c, carry):
        r0 = pl.multiple_of(c * c64, c64)
        l0 = pl.multiple_of((c - 1) * c64, c64)
        o = of_s[pl.ds(r0, c64), :] + ob_s[pl.ds(r0, c64), :]
        y_ref[0, pl.ds(l0, c64), :] = emit(o, z_ref[0, pl.ds(l0, c64), :])
        return carry

    lax.fori_loop(1, n_chunks, out_step, 0)


def _deltanet(proj3, ba3, conv_w, gpar, out_norm):
    b, l, _ = proj3.shape
    nh = DN_NH
    wd = nh * DN_HEAD_DIM
    n_rows = l + DN_PAD
    n_chunks = n_rows // DN_CHUNK
    ng = DN_NG
    blk = lambda col0: pl.BlockSpec((1, l, wd), lambda i, g, col0=col0: (i, 0, col0 // nh + g))
    cblk = lambda part: pl.BlockSpec((DN_CONV, wd), lambda i, g, part=part: (0, part * ng + g))
    return pl.pallas_call(
        functools.partial(_dn_body, seq_len=l),
        grid=(b, ng),
        in_specs=[
            blk(COL_DQ), blk(COL_DK), blk(COL_DV), blk(COL_DZ),
            pl.BlockSpec((1, l, LANES), lambda i, g: (i, 0, g)),
            cblk(0), cblk(1), cblk(2),
            pl.BlockSpec((1, 2, LANES), lambda i, g: (g, 0, 0)),
            pl.BlockSpec((1, LANES), lambda i, g: (0, 0)),
        ],
        out_specs=pl.BlockSpec((1, l, wd), lambda i, g: (i, 0, g)),
        out_shape=jax.ShapeDtypeStruct((b, l, DN_WIDTH), BF16),
        scratch_shapes=[
            pltpu.VMEM((n_rows + 16, wd), F32), pltpu.VMEM((n_rows + 16, wd), F32),
            pltpu.VMEM((n_rows + 16, wd), F32),
            pltpu.VMEM((n_rows, wd), F32), pltpu.VMEM((n_rows, wd), F32), pltpu.VMEM((n_rows, wd), F32),
            pltpu.VMEM((n_rows, LANES), F32), pltpu.VMEM((n_rows, LANES), F32),
            pltpu.VMEM((nh, 2 * n_rows, LANES), BF16), pltpu.VMEM((nh, 2 * n_rows, LANES), BF16),
            pltpu.VMEM((nh, 2 * n_rows, LANES), BF16),
            pltpu.VMEM((nh, 2 * n_rows, LANES), BF16),
            pltpu.VMEM((nh, 2 * n_rows, LANES), BF16),
            pltpu.VMEM((nh, n_chunks * 16, LANES), F32),
            pltpu.VMEM((n_rows, wd), F32), pltpu.VMEM((n_rows, wd), F32),
            pltpu.VMEM((DN_NCH, LANES, LANES), F32),
        ],
        compiler_params=pltpu.CompilerParams(
            dimension_semantics=("arbitrary", "arbitrary"), vmem_limit_bytes=VMEM_LIMIT),
        name="deltanet",
    )(proj3, proj3, proj3, proj3, ba3, conv_w, conv_w, conv_w, gpar, out_norm)


def _na_body(q_ref, k_ref, v_ref, tb_ref, o_ref, *, rows):
    scale = NA_HEAD_DIM ** -0.5
    wr = NA_WIN_R
    nk = wr * GRID_W
    n_grid = rows * GRID_W
    qm = q_ref[0, n_grid:n_grid + N_META, :]
    km = k_ref[0, n_grid:n_grid + N_META, :]
    vm = v_ref[0, n_grid:n_grid + N_META, :]

    s = _dot_nt(qm, km) * scale
    p = jnp.exp(s - jnp.max(s, axis=-1, keepdims=True))
    o = _dot(p.astype(BF16), vm) / jnp.sum(p, axis=-1, keepdims=True)
    o_ref[0, n_grid:n_grid + N_META, :] = o.astype(BF16)

    pad = jnp.zeros((NA_KEYS - nk - N_META, LANES), BF16)

    def row_group_step(grp, carry):
        qs, ks, vs, biases, dsts = [], [], [], [], []
        for j in range(NA_ROW_GROUP):
            r = grp * NA_ROW_GROUP + j
            rs = jnp.clip(r - wr // 2, 0, rows - wr)
            dsts.append(pl.ds(pl.multiple_of(r * GRID_W, GRID_W), GRID_W))
            win = pl.ds(pl.multiple_of(rs * GRID_W, GRID_W), nk)
            qs.append(q_ref[0, dsts[j], :])
            ks.append(jnp.concatenate([k_ref[0, win, :], km, pad], axis=0))
            vs.append(jnp.concatenate([v_ref[0, win, :], vm, pad], axis=0))
            biases.append(tb_ref[0, rs - r + NA_WIN_R - 1])
        ss = [_dot_nt(qs[j], ks[j]) * scale + biases[j] for j in range(NA_ROW_GROUP)]
        ps = [jnp.exp(s - jnp.max(s, axis=-1, keepdims=True)) for s in ss]
        os_ = [_dot(ps[j].astype(BF16), vs[j]) / jnp.sum(ps[j], axis=-1, keepdims=True) for j in range(NA_ROW_GROUP)]
        for j in range(NA_ROW_GROUP):
            o_ref[0, dsts[j], :] = os_[j].astype(BF16)
        return carry

    assert rows % NA_ROW_GROUP == 0
    lax.fori_loop(0, rows // NA_ROW_GROUP, row_group_step, 0)


def _na_bias_table(rpb):
    wr, wc = NA_WIN_R, NA_WIN_C
    qc = np.arange(GRID_W)[:, None]
    kc = np.arange(GRID_W)[None, :]
    q_start = np.clip(qc - wc // 2, 0, GRID_W - wc)
    valid = (kc - q_start >= 0) & (kc - q_start < wc)
    dc = np.clip(kc - qc + wc - 1, 0, 2 * wc - 2)
    i0 = np.arange(wr)[:, None]
    jj = np.arange(wr)[None, :]
    dr = i0 + jj
    tb = rpb.astype(F32)[:, dr][:, :, :, dc]
    tb = jnp.where(valid[None, None, None], tb, -jnp.inf)
    tb = tb.transpose(0, 1, 3, 2, 4)
    tb = tb.reshape(NA_HEADS, wr, GRID_W, wr * GRID_W)
    meta = jnp.zeros(tb.shape[:3] + (N_META,), F32)
    fill = jnp.full(tb.shape[:3] + (NA_KEYS - wr * GRID_W - N_META,), -jnp.inf, F32)
    return jnp.concatenate([tb, meta, fill], axis=-1)


def _natten(proj3, tb):
    b, l, _ = proj3.shape
    rows = (l - N_META) // GRID_W
    assert rows >= NA_WIN_R
    blk = lambda col0: pl.BlockSpec((1, l, LANES), lambda h, i, col0=col0: (i, 0, col0 + h))
    return pl.pallas_call(
        functools.partial(_na_body, rows=rows),
        grid=(NA_HEADS, b),
        in_specs=[
            blk(COL_NQ), blk(COL_NK), blk(COL_NV),
            pl.BlockSpec((1, NA_WIN_R, GRID_W, NA_KEYS), lambda h, i: (h, 0, 0, 0)),
        ],
        out_specs=pl.BlockSpec((1, l, LANES), lambda h, i: (i, 0, h)),
        out_shape=jax.ShapeDtypeStruct((b, l, NA_WIDTH), BF16),
        compiler_params=pltpu.CompilerParams(
            dimension_semantics=("arbitrary", "arbitrary"), vmem_limit_bytes=VMEM_LIMIT),
        name="natten",
    )(proj3, proj3, proj3, tb)


def _mix_body(ydn_ref, yna_ref, ga_ref, gb_ref, h_ref, wdn_ref, wna_ref, wo_ref, gn_ref, wr_ref, br_ref,
              h2_ref, u2_ref, gate_ref, idx_ref):
    a = _dot(ydn_ref[...], wdn_ref[...])
    b = _dot(yna_ref[...], wna_ref[...])
    mix = _sigmoid(ga_ref[...].astype(F32)) * a + _sigmoid(gb_ref[...].astype(F32)) * b
    h2 = h_ref[...] + _dot(mix.astype(BF16), wo_ref[...])
    h2_ref[...] = h2
    u2 = h2 * lax.rsqrt(jnp.mean(h2 * h2, axis=-1, keepdims=True) + NORM_EPS) * gn_ref[...]
    _store_token_tiles(u2_ref, _pack_halves(u2), u2.shape[0])
    u_hi = u2.astype(BF16)
    u_lo = (u2 - u_hi.astype(F32)).astype(BF16)
    w_r = wr_ref[...]
    w_hi = w_r.astype(BF16)
    w_lo = (w_r - w_hi.astype(F32)).astype(BF16)
    logits = _dot(u_hi, w_hi) + (_dot(u_hi, w_lo) + _dot(u_lo, w_hi)) + br_ref[...]
    lane = lax.broadcasted_iota(I32, logits.shape, 1)
    logits = jnp.where(lane < N_EXPERTS, logits, -jnp.inf)
    vals, idxs = [], []
    for _ in range(TOP_K):
        m = jnp.max(logits, axis=-1, keepdims=True)
        sel = jnp.min(jnp.where(logits == m, lane, LANES), axis=-1, keepdims=True)
        vals.append(m)
        idxs.append(sel)
        logits = jnp.where(lane == sel, -jnp.inf, logits)
    es = [jnp.exp(v - vals[0]) for v in vals]
    den = es[0] + es[1] + es[2] + es[3]
    gates = jnp.zeros(logits.shape, F32)
    idx = jnp.zeros(logits.shape, I32)
    for k in range(TOP_K):
        gates = jnp.where(lane == k, es[k] / den, gates)
        idx = jnp.where(lane == k, idxs[k], idx)
    gate_ref[...] = gates
    idx_ref[...] = idx


def _mix(y_dn, y_na, proj, h, w_dn, w_na, w_o, gain, w_r, b_r, tm):
    t, d = h.shape
    row = lambda w: pl.BlockSpec((tm, w), lambda i: (i, 0))
    full = lambda a: pl.BlockSpec(a.shape, lambda i: (0,) * a.ndim)
    return pl.pallas_call(
        _mix_body,
        grid=(t // tm,),
        in_specs=[
            row(DN_WIDTH), row(NA_WIDTH),
            pl.BlockSpec((tm, d), lambda i: (i, COL_GA * LANES // d)),
            pl.BlockSpec((tm, d), lambda i: (i, COL_GB * LANES // d)),
            row(d), full(w_dn), full(w_na), full(w_o), full(gain), full(w_r), full(b_r),
        ],
        out_specs=[row(d), pl.BlockSpec((tm * SUBLANES, LANES), lambda i: (i, 0)), row(LANES), row(LANES)],
        out_shape=[
            jax.ShapeDtypeStruct((t, d), F32), jax.ShapeDtypeStruct((t * SUBLANES, LANES), U32),
            jax.ShapeDtypeStruct((t, LANES), F32), jax.ShapeDtypeStruct((t, LANES), I32),
        ],
        compiler_params=pltpu.CompilerParams(
            dimension_semantics=("arbitrary",), vmem_limit_bytes=VMEM_LIMIT),
        name="mix",
    )(y_dn, y_na, proj, proj, h, w_dn, w_na, w_o, gain, w_r, b_r)


def _moe_body(blk_exp, n_valid, idx_hbm, u_hbm, wg_ref, wl_ref, bg_ref, bl_ref, wo_ref, bo_ref, y_hbm,
              idx_s, xbuf, xs, act_s, stage, isem, gsem, ssem, *, n_blocks, n_asg):
    i = pl.program_id(0)
    tm = MOE_TM
    n_idx = 2 * tm

    def idx_copy(blk, sl):
        return pltpu.make_async_copy(idx_hbm.at[pl.ds(pl.multiple_of(blk * n_idx, n_idx), n_idx)],
                                     idx_s.at[pl.ds(pl.multiple_of(sl * n_idx, n_idx), n_idx)], isem.at[sl])

    def tile(ref, row0):
        return ref.at[pl.ds(pl.multiple_of(row0, SUBLANES), SUBLANES), :]

    def gather_row(base, sl, r):
        src = idx_s[base + r]
        pltpu.make_async_copy(tile(u_hbm, src), tile(xbuf.at[sl], r * SUBLANES), gsem.at[sl]).start()

    def scatter_row(base, sl, r):
        dst = idx_s[base + tm + r]
        pltpu.make_async_copy(tile(stage.at[sl], r * SUBLANES), tile(y_hbm, dst), ssem).start()

    def issue_loop(row_fn, base, sl):
        def body(r, carry):
            row_fn(base, sl, r)
            return carry
        lax.fori_loop(0, tm, body, 0, unroll=DMA_ISSUE_UNROLL)

    def gather_wait(sl):
        pltpu.make_async_copy(xbuf.at[sl], xbuf.at[sl], gsem.at[sl]).wait()

    def scatter_wait():
        pltpu.make_async_copy(stage.at[0], stage.at[0], ssem).wait()

    live = n_valid[i] > 0
    prev_live = (i > 0) & (n_valid[jnp.maximum(i - 1, 0)] > 0)
    slot = i % 2
    own_base = (i % 4) * n_idx
    prev_base = ((i + 3) % 4) * n_idx
    prev_slot = (i + 1) % 2

    @pl.when(i == 0)
    def _():
        stage[...] = jnp.zeros(stage.shape, U32)
        n_stage = tm * SUBLANES
        tail0, tail1 = n_asg * SUBLANES, y_hbm.shape[0]
        fills = [pltpu.make_async_copy(stage.at[0, pl.ds(0, min(n_stage, tail1 - s0)), :],
                                       y_hbm.at[pl.ds(s0, min(n_stage, tail1 - s0)), :], ssem)
                 for s0 in range(tail0, tail1, n_stage)]
        for fill in fills:
            fill.start()
        for fill in fills:
            fill.wait()
        idx_copy(0, 0).start()
        idx_copy(n_blocks, 3).start()
        idx_copy(0, 0).wait()
        idx_copy(n_blocks, 3).wait()
        issue_loop(gather_row, 0, 0)
        idx_copy(1, 1).start()

    @pl.when(i + 1 < n_blocks)
    def _():
        idx_copy(i + 1, (i + 1) % 4).wait()

    @pl.when(i + 2 < n_blocks)
    def _():
        idx_copy(i + 2, (i + 2) % 4).start()

    @pl.when((i == 0) | prev_live)
    def _():
        gather_wait(slot)

    @pl.when(jnp.logical_not(live) & prev_live)
    def _():
        scatter_wait()
        issue_loop(scatter_row, prev_base, prev_slot)
        scatter_wait()

    nxt = jnp.minimum(i + 1, n_blocks - 1)
    nxt_base = (nxt % 4) * n_idx
    nxt_slot = (i + 1) % 2
    n_groups = D_EXPERT // MOE_SUB
    rows_per_group = tm // n_groups
    n_pairs = HALF // MOE_SUB
    rows_per_pair = tm // n_pairs

    @pl.when(live)
    def _():
        for s in range(SUBLANES):
            lo, hi = _unpack_halves(xbuf[slot, pl.ds(s, tm, stride=SUBLANES), :])
            xs[:, s * LANES:(s + 1) * LANES] = lo.astype(BF16)
            xs[:, HALF + s * LANES:HALF + (s + 1) * LANES] = hi.astype(BF16)
        x = xs[...]
        for s in range(n_groups):
            cols = slice(s * MOE_SUB, (s + 1) * MOE_SUB)
            hg = _dot(x, wg_ref[0, :, cols]) + bg_ref[0][:, cols]
            hl = _dot(x, wl_ref[0, :, cols]) + bl_ref[0][:, cols]
            for r in range(s * rows_per_group, (s + 1) * rows_per_group):
                gather_row(nxt_base, nxt_slot, r)
            glu = jnp.minimum(hg, SWIGLU_LIMIT)
            lin = jnp.clip(hl, -SWIGLU_LIMIT, SWIGLU_LIMIT)
            act_s[:, cols] = (glu * _sigmoid(SWIGLU_ALPHA * glu) * (lin + 1.0)).astype(BF16)

        @pl.when(i > 0)
        def _():
            scatter_wait()

        act = act_s[...]
        tiles_per_sub = MOE_SUB // LANES
        for c in range(n_pairs):
            lo_cols = slice(c * MOE_SUB, (c + 1) * MOE_SUB)
            hi_cols = slice(HALF + c * MOE_SUB, HALF + (c + 1) * MOE_SUB)
            out_lo = _dot(act, wo_ref[0, :, lo_cols]) + bo_ref[0][:, lo_cols]
            out_hi = _dot(act, wo_ref[0, :, hi_cols]) + bo_ref[0][:, hi_cols]
            for r in range(c * rows_per_pair, (c + 1) * rows_per_pair):
                scatter_row(prev_base, prev_slot, r)
            packed = _pack_halves(jnp.concatenate([out_lo, out_hi], axis=1))
            for j in range(tiles_per_sub):
                stage[slot, pl.ds(c * tiles_per_sub + j, tm, stride=SUBLANES), :] = packed[:, j * LANES:(j + 1) * LANES]

        @pl.when(i == n_blocks - 1)
        def _():
            gather_wait(nxt_slot)
            scatter_wait()
            issue_loop(scatter_row, own_base, slot)
            scatter_wait()


def _moe(blk_exp, n_valid, idx_blocks, u2p, w_in, b_in, w_out, b_out, n_asg, n_rows_out):
    n_blocks = blk_exp.shape[0]
    d = D_MODEL
    f = D_EXPERT

    def half(g):
        return lambda i, be, nv: (be[i], 0, g)

    grid_spec = pltpu.PrefetchScalarGridSpec(
        num_scalar_prefetch=2,
        grid=(n_blocks,),
        in_specs=[
            pl.BlockSpec(memory_space=pl.ANY),
            pl.BlockSpec(memory_space=pl.ANY),
            pl.BlockSpec((1, d, f), half(0), pipeline_mode=pl.Buffered(1)),
            pl.BlockSpec((1, d, f), half(1), pipeline_mode=pl.Buffered(1)),
            pl.BlockSpec((1, 1, f), half(0)),
            pl.BlockSpec((1, 1, f), half(1)),
            pl.BlockSpec((1, f, d), lambda i, be, nv: (be[i], 0, 0)),
            pl.BlockSpec((1, 1, d), lambda i, be, nv: (be[i], 0, 0)),
        ],
        out_specs=pl.BlockSpec(memory_space=pl.ANY),
        scratch_shapes=[
            pltpu.SMEM((4 * 2 * MOE_TM,), I32),
            pltpu.VMEM((2, MOE_TM * SUBLANES, LANES), U32),
            pltpu.VMEM((MOE_TM, d), BF16),
            pltpu.VMEM((MOE_TM, f), BF16),
            pltpu.VMEM((2, MOE_TM * SUBLANES, LANES), U32),
            pltpu.SemaphoreType.DMA((4,)),
            pltpu.SemaphoreType.DMA((2,)),
            pltpu.SemaphoreType.DMA,
        ],
    )
    return pl.pallas_call(
        functools.partial(_moe_body, n_blocks=n_blocks, n_asg=n_asg),
        grid_spec=grid_spec,
        out_shape=jax.ShapeDtypeStruct((n_rows_out * SUBLANES, LANES), U32),
        compiler_params=pltpu.CompilerParams(
            dimension_semantics=("arbitrary",), vmem_limit_bytes=VMEM_LIMIT),
        name="moe",
    )(blk_exp, n_valid, idx_blocks, u2p, w_in, w_in, b_in, b_in, w_out, b_out)


def _routing_metadata(top_idx, n_blocks):
    tm = MOE_TM
    e = top_idx[:, :TOP_K].reshape(-1)
    n_asg = e.shape[0]
    order = jnp.argsort(e, stable=True).astype(I32)
    counts = jnp.sum((e[:, None] == jnp.arange(N_EXPERTS, dtype=I32)[None, :]).astype(I32), axis=0)
    cstart = jnp.cumsum(counts) - counts
    nblk = (counts + tm - 1) // tm
    bend = jnp.cumsum(nblk)
    bstart = bend - nblk
    blk = jnp.arange(n_blocks, dtype=I32)
    blk_exp = jnp.minimum(jnp.sum((bend[None, :] <= blk[:, None]).astype(I32), axis=1), N_EXPERTS - 1)
    first = (blk - bstart[blk_exp]) * tm
    n_valid = jnp.where(blk < bend[-1], jnp.clip(counts[blk_exp] - first, 0, tm), 0).astype(I32)
    r = jnp.arange(tm, dtype=I32)[None, :]
    valid = r < n_valid[:, None]
    asg = order[jnp.where(valid, (cstart[blk_exp] + first)[:, None] + r, 0)]
    src_row = (asg // TOP_K) * SUBLANES
    dst_row = jnp.where(valid, asg, n_asg + r) * SUBLANES
    src_row = jnp.concatenate([src_row, jnp.zeros((1, tm), I32)], axis=0)
    dst_row = jnp.concatenate([dst_row, (n_asg + r) * SUBLANES], axis=0)
    idx_blocks = jnp.concatenate([src_row, dst_row], axis=1).reshape(-1)
    return blk_exp, n_valid, idx_blocks


def _final_body(h_ref, y_ref, gate_ref, gn_ref, o_ref):
    h = h_ref[0]
    gates = gate_ref[0]
    y_ref = y_ref.at[0]
    lo_acc = jnp.zeros((h.shape[0], HALF), F32)
    hi_acc = jnp.zeros((h.shape[0], HALF), F32)
    for k in range(TOP_K):
        lo, hi = _unpack_halves(_load_token_tiles(y_ref, h.shape[0], k * SUBLANES, TOP_K * SUBLANES))
        g = gates[:, k:k + 1]
        lo_acc = lo_acc + g * lo
        hi_acc = hi_acc + g * hi
    y = h + jnp.concatenate([lo_acc, hi_acc], axis=1)
    o_ref[0] = y * lax.rsqrt(jnp.mean(y * y, axis=-1, keepdims=True) + NORM_EPS) * gn_ref[...]


def _final(h2, y4, gates, gain, n_out, tm):
    b, _, d = h2.shape
    return pl.pallas_call(
        _final_body,
        grid=(b, n_out // tm),
        in_specs=[
            pl.BlockSpec((1, tm, d), lambda i, j: (i, j, 0)),
            pl.BlockSpec((1, tm * TOP_K * SUBLANES, LANES), lambda i, j: (i, j, 0)),
            pl.BlockSpec((1, tm, LANES), lambda i, j: (i, j, 0)),
            pl.BlockSpec((1, d), lambda i, j: (0, 0)),
        ],
        out_specs=pl.BlockSpec((1, tm, d), lambda i, j: (i, j, 0)),
        out_shape=jax.ShapeDtypeStruct((b, n_out, d), F32),
        compiler_params=pltpu.CompilerParams(
            dimension_semantics=("arbitrary", "arbitrary"), vmem_limit_bytes=VMEM_LIMIT),
        name="final",
    )(h2, y4, gates, gain)


def _split_w_in(w_in):
    n_dn = 4 * DN_WIDTH
    n_na = n_dn + 4 * DN_HEADS
    w_main = jnp.concatenate(
        [w_in[:, :n_dn], w_in[:, n_na + 3 * NA_WIDTH:], w_in[:, n_na:n_na + 3 * NA_WIDTH]], axis=1).astype(BF16)
    wb = w_in[:, n_dn:n_dn + 2 * DN_HEADS]
    wa = w_in[:, n_dn + 2 * DN_HEADS:n_dn + 4 * DN_HEADS]
    return w_main, _group_lanes(wb, wa).astype(BF16)


def _group_lanes(b_part, a_part):
    nh, ng = DN_NH, DN_NG
    lead = b_part.shape[:-1]
    bp = b_part.reshape(lead + (2, ng, nh))
    ap = a_part.reshape(lead + (2, ng, nh))
    perm = tuple(range(len(lead))) + (len(lead) + 1, len(lead), len(lead) + 2)
    bp = bp.transpose(perm).reshape(lead + (ng, 2 * nh))
    ap = ap.transpose(perm).reshape(lead + (ng, 2 * nh))
    both = jnp.concatenate([bp, ap], axis=-1)
    pad = [(0, 0)] * (both.ndim - 1) + [(0, LANES - 4 * nh)]
    return jnp.pad(both, pad).reshape(lead + (ng * LANES,))


def kernel(x, meta_tokens, norm_mix, w_in, dn_conv, dn_a_log, dn_dt_bias, dn_out_norm, na_rpb,
           w_branch_dn, w_branch_na, w_out, norm_ffn, w_router, b_router, w_exp_in, b_exp_in,
           w_exp_out, b_exp_out, norm_final):
    bsz, seq, d = x.shape
    l = seq + N_META
    t = bsz * l
    meta = jnp.broadcast_to(meta_tokens[None].astype(x.dtype), (bsz, N_META, d))
    h = jnp.concatenate([x, meta], axis=1).reshape(t, d)
    depth = w_in.shape[0]
    assert depth == 1, "the final kernel fuses the last residual add with the final RMSNorm"
    tm_proj = 768 if t % 768 == 0 else 16 * (l // 16)
    tm_mix = 384 if t % 384 == 0 else 8 * (l // 8)
    tm_final = 512 if seq % 512 == 0 else seq
    n_blocks = t * TOP_K // MOE_TM + N_EXPERTS
    for layer in range(depth):
        w_main, w_ba = _split_w_in(w_in[layer])
        proj, ba = _inproj(h, norm_mix[layer][None, :], w_main, w_ba, tm_proj, 1024)
        proj3 = proj.reshape(bsz, l, N_MAIN_COLS)
        ba3 = ba.reshape(bsz, l, DN_NG * LANES)
        gpar = jnp.stack([_group_lanes(jnp.zeros_like(dn_a_log[layer]).reshape(-1), dn_a_log[layer].reshape(-1)),
                          _group_lanes(jnp.zeros_like(dn_dt_bias[layer]).reshape(-1),
                                       dn_dt_bias[layer].reshape(-1))], axis=0)
        gpar = gpar.reshape(2, DN_NG, LANES).transpose(1, 0, 2)
        y_dn = _deltanet(proj3, ba3, dn_conv[layer], gpar, dn_out_norm[layer][None, :])
        y_na = _natten(proj3, _na_bias_table(na_rpb[layer]))
        w_r = jnp.pad(w_router[layer], ((0, 0), (0, LANES - N_EXPERTS)))
        b_r = jnp.pad(b_router[layer], (0, LANES - N_EXPERTS))[None, :]
        h2, u2p, gates, top_idx = _mix(
            y_dn.reshape(t, DN_WIDTH), y_na.reshape(t, NA_WIDTH), proj, h,
            w_branch_dn[layer].astype(BF16), w_branch_na[layer].astype(BF16), w_out[layer].astype(BF16),
            norm_ffn[layer][None, :], w_r, b_r, tm_mix)
        blk_exp, n_valid, idx_blocks = _routing_metadata(top_idx, n_blocks)
        y4 = _moe(blk_exp, n_valid, idx_blocks, u2p,
                  w_exp_in[layer].astype(BF16), b_exp_in[layer][:, None, :],
                  w_exp_out[layer].astype(BF16), b_exp_out[layer][:, None, :], t * TOP_K, (bsz + 1) * l * TOP_K)
        out = _final(h2.reshape(bsz, l, d), y4.reshape(bsz + 1, l * TOP_K * SUBLANES, LANES),
                     gates.reshape(bsz, l, LANES), norm_final[None, :], seq, tm_final)
    return out
```

```python
import functools

import jax
import jax.numpy as jnp
import numpy as np
from jax import lax
from jax.experimental import pallas as pl
from jax.experimental.pallas import tpu as pltpu

F32 = jnp.float32
BF16 = jnp.bfloat16
I32 = jnp.int32
U32 = jnp.uint32

D_MODEL = 2048
N_META = 16
GRID_W = 64
DN_HEADS = 8
DN_HEAD_DIM = 128
DN_WIDTH = 1024
DN_CONV = 5
DN_CHUNK = 64
NA_HEADS = 8
NA_HEAD_DIM = 128
NA_WIDTH = 1024
NA_WIN_R = 8
NA_WIN_C = 16
N_EXPERTS = 32
TOP_K = 4
D_EXPERT = 2048
SWIGLU_ALPHA = 1.702
SWIGLU_LIMIT = 7.0
NORM_EPS = 1e-6

LANES = 128
VMEM_LIMIT = 56 * 1024 * 1024

COL_DQ, COL_DK, COL_DV, COL_DZ = 0, 8, 16, 24
COL_GA, COL_GB = 32, 48
COL_NQ, COL_NK, COL_NV = 64, 72, 80
N_MAIN_COLS = 88 * LANES

DN_NH = 2
DN_NG = DN_HEADS // DN_NH
DN_NCH = 2 * DN_NH
DN_PAD = (-N_META) % DN_CHUNK
DN_CHUNK_UNROLL = 11

DMA_ISSUE_UNROLL = 8

NA_ROW_GROUP = 8
NA_KEYS = 640

MOE_TM = 512
MOE_SUB = 256
HALF = D_MODEL // 2
SUBLANES = 8
assert HALF == SUBLANES * LANES


def _store_token_tiles(ref, packed, n_tok, base=0, group=SUBLANES):
    for s in range(SUBLANES):
        ref[pl.ds(base + s, n_tok, stride=group), :] = packed[:, s * LANES:(s + 1) * LANES]


def _load_token_tiles(ref, n_tok, base=0, group=SUBLANES):
    return jnp.concatenate([ref[pl.ds(base + s, n_tok, stride=group), :] for s in range(SUBLANES)], axis=1)


def _dot(a, b, **kw):
    return jnp.dot(a, b, preferred_element_type=F32, **kw)


def _dot_nt(a, b, **kw):
    return lax.dot_general(a, b, (((1,), (1,)), ((), ())), preferred_element_type=F32, **kw)


def _dot_tn(a, b, **kw):
    return lax.dot_general(a, b, (((0,), (0,)), ((), ())), preferred_element_type=F32, **kw)


def _sigmoid(x):
    return 1.0 / (1.0 + jnp.exp(-x))


def _pack_halves(x):
    n = x.shape[1] // 2
    lo = pltpu.bitcast(x[:, :n].astype(BF16).astype(F32), U32)
    hi = pltpu.bitcast(x[:, n:].astype(BF16).astype(F32), U32)
    return (hi & jnp.uint32(0xFFFF0000)) | (lo >> 16)


def _unpack_halves(p):
    lo = pltpu.bitcast(p << 16, F32)
    hi = pltpu.bitcast(p & jnp.uint32(0xFFFF0000), F32)
    return lo, hi


def _inproj_body(h_ref, g_ref, w_ref, wba_ref, o_ref, oba_ref, u_ref):
    @pl.when(pl.program_id(1) == 0)
    def _():
        x = h_ref[...]
        y = x * lax.rsqrt(jnp.mean(x * x, axis=-1, keepdims=True) + NORM_EPS) * g_ref[...]
        ub = y.astype(BF16)
        u_ref[...] = ub
        oba_ref[...] = _dot(ub, wba_ref[...])

    o_ref[...] = _dot(u_ref[...], w_ref[...]).astype(BF16)


def _inproj(h, gain, w_main, w_ba, tm, tn):
    t, d = h.shape
    n = w_main.shape[1]
    nba = w_ba.shape[1]
    return pl.pallas_call(
        _inproj_body,
        grid=(t // tm, n // tn),
        in_specs=[
            pl.BlockSpec((tm, d), lambda i, j: (i, 0)),
            pl.BlockSpec((1, d), lambda i, j: (0, 0)),
            pl.BlockSpec((d, tn), lambda i, j: (0, j)),
            pl.BlockSpec((d, nba), lambda i, j: (0, 0)),
        ],
        out_specs=[
            pl.BlockSpec((tm, tn), lambda i, j: (i, j)),
            pl.BlockSpec((tm, nba), lambda i, j: (i, 0)),
        ],
        out_shape=[jax.ShapeDtypeStruct((t, n), BF16), jax.ShapeDtypeStruct((t, nba), F32)],
        scratch_shapes=[pltpu.VMEM((tm, d), BF16)],
        compiler_params=pltpu.CompilerParams(
            dimension_semantics=("arbitrary", "arbitrary"), vmem_limit_bytes=VMEM_LIMIT),
        name="inproj",
    )(h, gain, w_main, w_ba)


def _pair_dot(lhs, rhs):
    w = rhs[0].shape[1]
    r0, r1 = rhs[0].astype(BF16), rhs[1].astype(BF16)
    zero = jnp.zeros(r0.shape, BF16)
    bd = jnp.concatenate([jnp.concatenate([r0, zero], axis=1), jnp.concatenate([zero, r1], axis=1)], axis=0)
    x = _dot(jnp.concatenate([lhs[0].astype(BF16), lhs[1].astype(BF16)], axis=1), bd)
    return x[:, :w], x[:, w:]


def _pair_neumann_inverse(a_pairs, nilpotency):
    c = a_pairs[0][0].shape[0]
    eye = (lax.broadcasted_iota(I32, (c, c), 0) == lax.broadcasted_iota(I32, (c, c), 1)).astype(F32)
    ns = [[-a for a in pair] for pair in a_pairs]
    ps = [[eye + x for x in n] for n in ns]
    ns = [list(_pair_dot(n, n)) for n in ns]
    steps = int(np.log2(nilpotency)) - 1
    for step in range(steps):
        if step < steps - 1:
            pns = [_pair_dot([jnp.concatenate([p[i], n[i]], axis=0) for i in range(2)], n) for p, n in zip(ps, ns)]
            ps = [[p[i] + pn[i][:c] for i in range(2)] for p, pn in zip(ps, pns)]
            ns = [[pn[i][c:] for i in range(2)] for pn in pns]
        else:
            pns = [_pair_dot(p, n) for p, n in zip(ps, ns)]
            ps = [[p[i] + pn[i] for i in range(2)] for p, pn in zip(ps, pns)]
    return ps


def _chunk_cumsum(x, half):
    rows = x.shape[0]
    r = lax.broadcasted_iota(I32, x.shape, 0)
    top = r < half
    s = 1
    while s < half:
        down = pltpu.roll(x, s, axis=0)
        up = pltpu.roll(x, rows - s, axis=0)
        x = x + jnp.where(top, jnp.where(r >= s, down, 0.0), jnp.where(r < rows - s, up, 0.0))
        s *= 2
    return x


def _dn_body(q_ref, k_ref, v_ref, z_ref, ba_ref, cq_ref, ck_ref, cv_ref, gp_ref, on_ref, y_ref,
             xq, xk, xv, qn, kn, vn, gbf, gbb, us, ws, qds, kdts, qks, cds, of_s, ob_s, s_s, *, seq_len):
    nh = DN_NH
    wd = nh * DN_HEAD_DIM
    c64 = DN_CHUNK
    n_chunks = (seq_len + DN_PAD) // c64
    n_rows = n_chunks * c64
    lead = 8 + DN_PAD

    for x in (xq, xk, xv):
        x[0:lead, :] = jnp.zeros((lead, wd), F32)
        x[lead + seq_len:lead + seq_len + 8, :] = jnp.zeros((8, wd), F32)
    gbf[0:DN_PAD, :] = jnp.zeros((DN_PAD, LANES), F32)
    gbb[0:DN_PAD, :] = jnp.zeros((DN_PAD, LANES), F32)

    neg_a = -jnp.exp(gp_ref[0, 0:1, :])
    dt_b = gp_ref[0, 1:2, :]

    def copy_rows(src0, pos0, n):
        src = pl.ds(src0, n)
        dst = pl.ds(pl.multiple_of(8 + pos0, 8), n)
        xq[dst, :] = q_ref[0, src, :].astype(F32)
        xk[dst, :] = k_ref[0, src, :].astype(F32)
        xv[dst, :] = v_ref[0, src, :].astype(F32)
        ba = ba_ref[0, src, :]
        lane = lax.broadcasted_iota(I32, ba.shape, 1)
        beta = _sigmoid(ba)
        sp_in = ba + dt_b
        softplus = jnp.maximum(sp_in, 0.0) + jnp.log(1.0 + jnp.exp(-jnp.abs(sp_in)))
        val = jnp.where(lane < 2 * nh, beta, jnp.where(lane < 4 * nh, neg_a * softplus, 0.0))
        dst_g = pl.ds(pl.multiple_of(pos0, 8), n)
        gbf[dst_g, :] = val
        gbb[dst_g, :] = pltpu.roll(val, LANES - nh, axis=1)

    n_grid = seq_len - N_META
    copy_rows(n_grid, DN_PAD, N_META)

    def copy_step(j, carry):
        copy_rows(pl.multiple_of(j * c64, c64), DN_PAD + N_META + j * c64, c64)
        return carry

    lax.fori_loop(0, n_grid // c64, copy_step, 0)

    row_iota = lax.broadcasted_iota(I32, (c64, wd), 0)

    def conv_step(c, carry):
        r0 = pl.multiple_of(c * c64, c64)
        live = (row_iota + r0) >= DN_PAD
        for x, cw, dst, kind in ((xq, cq_ref, qn, "q"), (xk, ck_ref, kn, "k"), (xv, cv_ref, vn, "v")):
            win = x[pl.ds(r0, c64 + 16), :]
            acc = jnp.zeros((c64, wd), F32)
            for j in range(DN_CONV):
                off = 8 - DN_CONV // 2 + j
                acc = acc + win[off:off + c64, :] * cw[j:j + 1, :]
            y = jnp.where(live, acc * _sigmoid(acc), 0.0)
            if kind == "v":
                dst[pl.ds(r0, c64), :] = y
            else:
                scale = DN_HEAD_DIM ** -0.5 if kind == "q" else 1.0
                for hh in range(nh):
                    yh = y[:, hh * LANES:(hh + 1) * LANES]
                    yn = yh * lax.rsqrt(jnp.sum(yh * yh, axis=-1, keepdims=True) + 1e-6)
                    dst[pl.ds(r0, c64), hh * LANES:(hh + 1) * LANES] = yn * scale
        return carry

    lax.fori_loop(0, n_chunks, conv_step, 0)

    c2 = 2 * c64
    ri = lax.broadcasted_iota(I32, (c2, c2), 0)
    ci = lax.broadcasted_iota(I32, (c2, c2), 1)
    fwd_blk = (ri < c64) & (ci < c64)
    bwd_blk = (ri >= c64) & (ci >= c64)
    incl = (fwd_blk & (ri >= ci)) | (bwd_blk & (ri <= ci))
    strict = (fwd_blk & (ri > ci)) | (bwd_blk & (ri < ci))
    lane_row = lax.broadcasted_iota(I32, (1, c2), 1)
    lane_blk = lax.broadcasted_iota(I32, (c2, c2), 1)

    def chunk_group_step(grp, carry):
        chunks = [grp * DN_CHUNK_UNROLL + j for j in range(DN_CHUNK_UNROLL)]
        r0s = [pl.multiple_of(c * c64, c64) for c in chunks]
        b0s = [pl.multiple_of(c * c2, c2) for c in chunks]
        gb2s, gam2s, gam2_ts, q2s, k2s, v2s, k2_ts, lhs_gs = [], [], [], [], [], [], [], []
        for r0 in r0s:
            gb2 = jnp.concatenate([gbf[pl.ds(r0, c64), :], gbb[pl.ds(r0, c64), :]], axis=0)
            gam2 = _chunk_cumsum(gb2, c64)
            gb2s.append(gb2)
            gam2s.append(gam2)
            gam2_ts.append(gam2.T)
            q2, k2, v2 = [], [], []
            for hh in range(nh):
                hs = slice(hh * LANES, (hh + 1) * LANES)
                q = qn[pl.ds(r0, c64), hs]
                k = kn[pl.ds(r0, c64), hs]
                v = vn[pl.ds(r0, c64), hs]
                q2.append(jnp.concatenate([q, q], axis=0))
                k2.append(jnp.concatenate([k, k], axis=0))
                v2.append(jnp.concatenate([v, v], axis=0))
            q2s.append(q2)
            k2s.append(k2)
            v2s.append(v2)
            k2_ts.append([x.T for x in k2])
            lhs_gs.append([jnp.concatenate([k2[hh], q2[hh]], axis=0) for hh in range(nh)])
        g2s = [_pair_dot(lhs_g, k2_t) for lhs_g, k2_t in zip(lhs_gs, k2_ts)]
        a_pairs, decays, e_gs, gam_rs, gam_lasts, beta_cs = [], [], [], [], [], []
        for j in range(DN_CHUNK_UNROLL):
            a_pair, decay, e_g, gam_r, gam_last, beta_c = [], [], [], [], [], []
            for hh in range(nh):
                gcol = 2 * nh + hh
                beta_c.append(gb2s[j][:, hh:hh + 1])
                gam_c = gam2s[j][:, gcol:gcol + 1]
                gam_r.append(gam2_ts[j][gcol:gcol + 1, :])
                gam_last.append((gam_c[c64 - 1:c64, :], gam_c[c64:c64 + 1, :]))
                decay.append(jnp.exp(jnp.where(incl, gam_c - gam_r[hh], -jnp.inf)))
                a_pair.append(jnp.where(strict, beta_c[hh] * g2s[j][hh][:c2] * decay[hh], 0.0))
                e_g.append(jnp.exp(gam_c))
            a_pairs.append(a_pair)
            decays.append(decay)
            e_gs.append(e_g)
            gam_rs.append(gam_r)
            gam_lasts.append(gam_last)
            beta_cs.append(beta_c)
        t_invs = _pair_neumann_inverse(a_pairs, c64)
        rhss = [[jnp.concatenate([beta_cs[j][hh] * v2s[j][hh], (beta_cs[j][hh] * e_gs[j][hh]) * k2s[j][hh]], axis=1)
                 for hh in range(nh)] for j in range(DN_CHUNK_UNROLL)]
        sols = [_pair_dot(t_inv, rhs) for t_inv, rhs in zip(t_invs, rhss)]
        for j, c in enumerate(chunks):
            b0 = b0s[j]
            for hh in range(nh):
                gl_f, gl_b = gam_lasts[j][hh]
                gl_row = jnp.where(lane_row < c64, gl_f, gl_b)
                us[hh, pl.ds(b0, c2), :] = sols[j][hh][:, :LANES].astype(BF16)
                ws[hh, pl.ds(b0, c2), :] = sols[j][hh][:, LANES:].astype(BF16)
                qds[hh, pl.ds(b0, c2), :] = (q2s[j][hh] * e_gs[j][hh]).astype(BF16)
                qks[hh, pl.ds(b0, c2), :] = (g2s[j][hh][c2:] * decays[j][hh]).astype(BF16)
                kdts[hh, pl.ds(b0, c2), :] = (k2_ts[j][hh] * jnp.exp(gl_row - gam_rs[j][hh])).astype(BF16)
                cds[hh, pl.ds(pl.multiple_of(c * 16, 16), 8), :] = jnp.broadcast_to(jnp.exp(gl_f), (8, LANES))
                cds[hh, pl.ds(pl.multiple_of(c * 16 + 8, 8), 8), :] = jnp.broadcast_to(jnp.exp(gl_b), (8, LANES))
        return carry

    assert n_chunks % DN_CHUNK_UNROLL == 0
    lax.fori_loop(0, n_chunks // DN_CHUNK_UNROLL, chunk_group_step, 0)

    s_s[...] = jnp.zeros(s_s.shape, F32)

    def rec_step(i, carry):
        cs = (i, n_chunks - 1 - i)
        r0s = [pl.multiple_of(c * c64, c64) for c in cs]
        b0s = [pl.multiple_of(c * c2, c2) for c in cs]
        rows = [pl.ds(pl.multiple_of(b0s[d] + d * c64, c64), c64) for d in range(2)]
        keeps = (lane_blk < c64, lane_blk >= c64)
        states = [[s_s[hh * 2 + d] for hh in range(nh)] for d in range(2)]
        wq_s = [_pair_dot([jnp.concatenate([ws[hh, rows[d], :], qds[hh, rows[d], :]], axis=0) for hh in range(nh)],
                          states[d]) for d in range(2)]
        vbs = [[(us[hh, rows[d], :].astype(F32) - wq_s[d][hh][:c64]).astype(BF16) for hh in range(nh)]
               for d in range(2)]
        lhs2 = [[jnp.concatenate([qks[hh, rows[d], :],
                                  jnp.where(keeps[d], kdts[hh, pl.ds(b0s[d], c2), :], jnp.zeros((), BF16))], axis=0)
                 for hh in range(nh)] for d in range(2)]
        r2 = [_pair_dot(lhs2[d], [jnp.concatenate([x, x], axis=0) for x in vbs[d]]) for d in range(2)]
        for d in range(2):
            for hh in range(nh):
                cd = cds[hh, pl.ds(pl.multiple_of(cs[d] * 16 + d * 8, 8), 8), :][0:1, :]
                s_s[hh * 2 + d] = states[d][hh] * cd + r2[d][hh][c64:]
                (of_s if d == 0 else ob_s)[pl.ds(r0s[d], c64), hh * LANES:(hh + 1) * LANES] = (
                    wq_s[d][hh][c64:] + r2[d][hh][:c64])
        return carry

    lax.fori_loop(0, n_chunks, rec_step, 0)

    gain = on_ref[...]

    def emit(o, z):
        outs = []
        for hh in range(nh):
            oh = o[:, hh * LANES:(hh + 1) * LANES]
            outs.append(oh * lax.rsqrt(jnp.mean(oh * oh, axis=-1, keepdims=True) + NORM_EPS) * gain)
        on = jnp.concatenate(outs, axis=1)
        zf = z.astype(F32)
        return (on * (zf * _sigmoid(zf))).astype(BF16)

    o0 = of_s[DN_PAD:c64, :] + ob_s[DN_PAD:c64, :]
    y_ref[0, n_grid:seq_len, :] = emit(o0, z_ref[0, n_grid:seq_len, :])

    def out_step(c, carry):
        r0 = pl.multiple_of(c * c64, c64)
        l0 = pl.multiple_of((c - 1) * c64, c64)
        o = of_s[pl.ds(r0, c64), :] + ob_s[pl.ds(r0, c64), :]
        y_ref[0, pl.ds(l0, c64), :] = emit(o, z_ref[0, pl.ds(l0, c64), :])
        return carry

    lax.fori_loop(1, n_chunks, out_step, 0)


def _deltanet(proj3, ba3, conv_w, gpar, out_norm):
    b, l, _ = proj3.shape
    nh = DN_NH
    wd = nh * DN_HEAD_DIM
    n_rows = l + DN_PAD
    n_chunks = n_rows // DN_CHUNK
    ng = DN_NG
    blk = lambda col0: pl.BlockSpec((1, l, wd), lambda i, g, col0=col0: (i, 0, col0 // nh + g))
    cblk = lambda part: pl.BlockSpec((DN_CONV, wd), lambda i, g, part=part: (0, part * ng + g))
    return pl.pallas_call(
        functools.partial(_dn_body, seq_len=l),
        grid=(b, ng),
        in_specs=[
            blk(COL_DQ), blk(COL_DK), blk(COL_DV), blk(COL_DZ),
            pl.BlockSpec((1, l, LANES), lambda i, g: (i, 0, g)),
            cblk(0), cblk(1), cblk(2),
            pl.BlockSpec((1, 2, LANES), lambda i, g: (g, 0, 0)),
            pl.BlockSpec((1, LANES), lambda i, g: (0, 0)),
        ],
        out_specs=pl.BlockSpec((1, l, wd), lambda i, g: (i, 0, g)),
        out_shape=jax.ShapeDtypeStruct((b, l, DN_WIDTH), BF16),
        scratch_shapes=[
            pltpu.VMEM((n_rows + 16, wd), F32), pltpu.VMEM((n_rows + 16, wd), F32),
            pltpu.VMEM((n_rows + 16, wd), F32),
            pltpu.VMEM((n_rows, wd), F32), pltpu.VMEM((n_rows, wd), F32), pltpu.VMEM((n_rows, wd), F32),
            pltpu.VMEM((n_rows, LANES), F32), pltpu.VMEM((n_rows, LANES), F32),
            pltpu.VMEM((nh, 2 * n_rows, LANES), BF16), pltpu.VMEM((nh, 2 * n_rows, LANES), BF16),
            pltpu.VMEM((nh, 2 * n_rows, LANES), BF16),
            pltpu.VMEM((nh, 2 * n_rows, LANES), BF16),
            pltpu.VMEM((nh, 2 * n_rows, LANES), BF16),
            pltpu.VMEM((nh, n_chunks * 16, LANES), F32),
            pltpu.VMEM((n_rows, wd), F32), pltpu.VMEM((n_rows, wd), F32),
            pltpu.VMEM((DN_NCH, LANES, LANES), F32),
        ],
        compiler_params=pltpu.CompilerParams(
            dimension_semantics=("arbitrary", "arbitrary"), vmem_limit_bytes=VMEM_LIMIT),
        name="deltanet",
    )(proj3, proj3, proj3, proj3, ba3, conv_w, conv_w, conv_w, gpar, out_norm)


def _na_body(q_ref, k_ref, v_ref, tb_ref, o_ref, *, rows):
    scale = NA_HEAD_DIM ** -0.5
    wr = NA_WIN_R
    nk = wr * GRID_W
    n_grid = rows * GRID_W
    qm = q_ref[0, n_grid:n_grid + N_META, :]
    km = k_ref[0, n_grid:n_grid + N_META, :]
    vm = v_ref[0, n_grid:n_grid + N_META, :]

    s = _dot_nt(qm, km) * scale
    p = jnp.exp(s - jnp.max(s, axis=-1, keepdims=True))
    o = _dot(p.astype(BF16), vm) / jnp.sum(p, axis=-1, keepdims=True)
    o_ref[0, n_grid:n_grid + N_META, :] = o.astype(BF16)

    pad = jnp.zeros((NA_KEYS - nk - N_META, LANES), BF16)

    def row_group_step(grp, carry):
        qs, ks, vs, biases, dsts = [], [], [], [], []
        for j in range(NA_ROW_GROUP):
            r = grp * NA_ROW_GROUP + j
            rs = jnp.clip(r - wr // 2, 0, rows - wr)
            dsts.append(pl.ds(pl.multiple_of(r * GRID_W, GRID_W), GRID_W))
            win = pl.ds(pl.multiple_of(rs * GRID_W, GRID_W), nk)
            qs.append(q_ref[0, dsts[j], :])
            ks.append(jnp.concatenate([k_ref[0, win, :], km, pad], axis=0))
            vs.append(jnp.concatenate([v_ref[0, win, :], vm, pad], axis=0))
            biases.append(tb_ref[0, rs - r + NA_WIN_R - 1])
        ss = [_dot_nt(qs[j], ks[j]) * scale + biases[j] for j in range(NA_ROW_GROUP)]
        ps = [jnp.exp(s - jnp.max(s, axis=-1, keepdims=True)) for s in ss]
        os_ = [_dot(ps[j].astype(BF16), vs[j]) / jnp.sum(ps[j], axis=-1, keepdims=True) for j in range(NA_ROW_GROUP)]
        for j in range(NA_ROW_GROUP):
            o_ref[0, dsts[j], :] = os_[j].astype(BF16)
        return carry

    assert rows % NA_ROW_GROUP == 0
    lax.fori_loop(0, rows // NA_ROW_GROUP, row_group_step, 0)


def _na_bias_table(rpb):
    wr, wc = NA_WIN_R, NA_WIN_C
    qc = np.arange(GRID_W)[:, None]
    kc = np.arange(GRID_W)[None, :]
    q_start = np.clip(qc - wc // 2, 0, GRID_W - wc)
    valid = (kc - q_start >= 0) & (kc - q_start < wc)
    dc = np.clip(kc - qc + wc - 1, 0, 2 * wc - 2)
    i0 = np.arange(wr)[:, None]
    jj = np.arange(wr)[None, :]
    dr = i0 + jj
    tb = rpb.astype(F32)[:, dr][:, :, :, dc]
    tb = jnp.where(valid[None, None, None], tb, -jnp.inf)
    tb = tb.transpose(0, 1, 3, 2, 4)
    tb = tb.reshape(NA_HEADS, wr, GRID_W, wr * GRID_W)
    meta = jnp.zeros(tb.shape[:3] + (N_META,), F32)
    fill = jnp.full(tb.shape[:3] + (NA_KEYS - wr * GRID_W - N_META,), -jnp.inf, F32)
    return jnp.concatenate([tb, meta, fill], axis=-1)


def _natten(proj3, tb):
    b, l, _ = proj3.shape
    rows = (l - N_META) // GRID_W
    assert rows >= NA_WIN_R
    blk = lambda col0: pl.BlockSpec((1, l, LANES), lambda h, i, col0=col0: (i, 0, col0 + h))
    return pl.pallas_call(
        functools.partial(_na_body, rows=rows),
        grid=(NA_HEADS, b),
        in_specs=[
            blk(COL_NQ), blk(COL_NK), blk(COL_NV),
            pl.BlockSpec((1, NA_WIN_R, GRID_W, NA_KEYS), lambda h, i: (h, 0, 0, 0)),
        ],
        out_specs=pl.BlockSpec((1, l, LANES), lambda h, i: (i, 0, h)),
        out_shape=jax.ShapeDtypeStruct((b, l, NA_WIDTH), BF16),
        compiler_params=pltpu.CompilerParams(
            dimension_semantics=("arbitrary", "arbitrary"), vmem_limit_bytes=VMEM_LIMIT),
        name="natten",
    )(proj3, proj3, proj3, tb)


def _mix_body(ydn_ref, yna_ref, ga_ref, gb_ref, h_ref, wdn_ref, wna_ref, wo_ref, gn_ref, wr_ref, br_ref,
              h2_ref, u2_ref, gate_ref, idx_ref):
    a = _dot(ydn_ref[...], wdn_ref[...])
    b = _dot(yna_ref[...], wna_ref[...])
    mix = _sigmoid(ga_ref[...].astype(F32)) * a + _sigmoid(gb_ref[...].astype(F32)) * b
    h2 = h_ref[...] + _dot(mix.astype(BF16), wo_ref[...])
    h2_ref[...] = h2
    u2 = h2 * lax.rsqrt(jnp.mean(h2 * h2, axis=-1, keepdims=True) + NORM_EPS) * gn_ref[...]
    _store_token_tiles(u2_ref, _pack_halves(u2), u2.shape[0])
    u_hi = u2.astype(BF16)
    u_lo = (u2 - u_hi.astype(F32)).astype(BF16)
    w_r = wr_ref[...]
    w_hi = w_r.astype(BF16)
    w_lo = (w_r - w_hi.astype(F32)).astype(BF16)
    logits = _dot(u_hi, w_hi) + (_dot(u_hi, w_lo) + _dot(u_lo, w_hi)) + br_ref[...]
    lane = lax.broadcasted_iota(I32, logits.shape, 1)
    logits = jnp.where(lane < N_EXPERTS, logits, -jnp.inf)
    vals, idxs = [], []
    for _ in range(TOP_K):
        m = jnp.max(logits, axis=-1, keepdims=True)
        sel = jnp.min(jnp.where(logits == m, lane, LANES), axis=-1, keepdims=True)
        vals.append(m)
        idxs.append(sel)
        logits = jnp.where(lane == sel, -jnp.inf, logits)
    es = [jnp.exp(v - vals[0]) for v in vals]
    den = es[0] + es[1] + es[2] + es[3]
    gates = jnp.zeros(logits.shape, F32)
    idx = jnp.zeros(logits.shape, I32)
    for k in range(TOP_K):
        gates = jnp.where(lane == k, es[k] / den, gates)
        idx = jnp.where(lane == k, idxs[k], idx)
    gate_ref[...] = gates
    idx_ref[...] = idx


def _mix(y_dn, y_na, proj, h, w_dn, w_na, w_o, gain, w_r, b_r, tm):
    t, d = h.shape
    row = lambda w: pl.BlockSpec((tm, w), lambda i: (i, 0))
    full = lambda a: pl.BlockSpec(a.shape, lambda i: (0,) * a.ndim)
    return pl.pallas_call(
        _mix_body,
        grid=(t // tm,),
        in_specs=[
            row(DN_WIDTH), row(NA_WIDTH),
            pl.BlockSpec((tm, d), lambda i: (i, COL_GA * LANES // d)),
            pl.BlockSpec((tm, d), lambda i: (i, COL_GB * LANES // d)),
            row(d), full(w_dn), full(w_na), full(w_o), full(gain), full(w_r), full(b_r),
        ],
        out_specs=[row(d), pl.BlockSpec((tm * SUBLANES, LANES), lambda i: (i, 0)), row(LANES), row(LANES)],
        out_shape=[
            jax.ShapeDtypeStruct((t, d), F32), jax.ShapeDtypeStruct((t * SUBLANES, LANES), U32),
            jax.ShapeDtypeStruct((t, LANES), F32), jax.ShapeDtypeStruct((t, LANES), I32),
        ],
        compiler_params=pltpu.CompilerParams(
            dimension_semantics=("arbitrary",), vmem_limit_bytes=VMEM_LIMIT),
        name="mix",
    )(y_dn, y_na, proj, proj, h, w_dn, w_na, w_o, gain, w_r, b_r)


def _moe_body(blk_exp, n_valid, idx_hbm, u_hbm, wg_ref, wl_ref, bg_ref, bl_ref, wo_ref, bo_ref, y_hbm,
              idx_s, xbuf, xs, act_s, stage, isem, gsem, ssem, *, n_blocks, n_asg):
    i = pl.program_id(0)
    tm = MOE_TM
    n_idx = 2 * tm

    def idx_copy(blk, sl):
        return pltpu.make_async_copy(idx_hbm.at[pl.ds(pl.multiple_of(blk * n_idx, n_idx), n_idx)],
                                     idx_s.at[pl.ds(pl.multiple_of(sl * n_idx, n_idx), n_idx)], isem.at[sl])

    def tile(ref, row0):
        return ref.at[pl.ds(pl.multiple_of(row0, SUBLANES), SUBLANES), :]

    def gather_row(base, sl, r):
        src = idx_s[base + r]
        prio = r % 2 if isinstance(r, int) else 0
        pltpu.make_async_copy(tile(u_hbm, src), tile(xbuf.at[sl], r * SUBLANES), gsem.at[sl]).start(priority=prio)

    def scatter_row(base, sl, r):
        dst = idx_s[base + tm + r]
        prio = r % 2 if isinstance(r, int) else 0
        pltpu.make_async_copy(tile(stage.at[sl], r * SUBLANES), tile(y_hbm, dst), ssem).start(priority=prio)

    def issue_loop(row_fn, base, sl):
        def body(r, carry):
            row_fn(base, sl, r)
            return carry
        lax.fori_loop(0, tm, body, 0, unroll=DMA_ISSUE_UNROLL)

    def gather_wait(sl):
        pltpu.make_async_copy(xbuf.at[sl], xbuf.at[sl], gsem.at[sl]).wait()

    def scatter_wait():
        pltpu.make_async_copy(stage.at[0], stage.at[0], ssem).wait()

    live = n_valid[i] > 0
    prev_live = (i > 0) & (n_valid[jnp.maximum(i - 1, 0)] > 0)
    slot = i % 2
    own_base = (i % 4) * n_idx
    prev_base = ((i + 3) % 4) * n_idx
    prev_slot = (i + 1) % 2

    @pl.when(i == 0)
    def _():
        stage[...] = jnp.zeros(stage.shape, U32)
        n_stage = tm * SUBLANES
        tail0, tail1 = n_asg * SUBLANES, y_hbm.shape[0]
        fills = [pltpu.make_async_copy(stage.at[0, pl.ds(0, min(n_stage, tail1 - s0)), :],
                                       y_hbm.at[pl.ds(s0, min(n_stage, tail1 - s0)), :], ssem)
                 for s0 in range(tail0, tail1, n_stage)]
        for fill in fills:
            fill.start()
        for fill in fills:
            fill.wait()
        idx_copy(0, 0).start()
        idx_copy(n_blocks, 3).start()
        idx_copy(0, 0).wait()
        idx_copy(n_blocks, 3).wait()
        issue_loop(gather_row, 0, 0)
        idx_copy(1, 1).start()

    @pl.when(i + 1 < n_blocks)
    def _():
        idx_copy(i + 1, (i + 1) % 4).wait()

    @pl.when(i + 2 < n_blocks)
    def _():
        idx_copy(i + 2, (i + 2) % 4).start()

    @pl.when((i == 0) | prev_live)
    def _():
        gather_wait(slot)

    @pl.when(jnp.logical_not(live) & prev_live)
    def _():
        scatter_wait()
        issue_loop(scatter_row, prev_base, prev_slot)
        scatter_wait()

    nxt = jnp.minimum(i + 1, n_blocks - 1)
    nxt_base = (nxt % 4) * n_idx
    nxt_slot = (i + 1) % 2
    n_groups = D_EXPERT // MOE_SUB
    rows_per_group = tm // n_groups
    n_pairs = HALF // MOE_SUB
    rows_per_pair = tm // n_pairs

    @pl.when(live)
    def _():
        for s in range(SUBLANES):
            lo, hi = _unpack_halves(xbuf[slot, pl.ds(s, tm, stride=SUBLANES), :])
            xs[:, s * LANES:(s + 1) * LANES] = lo.astype(BF16)
            xs[:, HALF + s * LANES:HALF + (s + 1) * LANES] = hi.astype(BF16)
        x = xs[...]
        for s in range(n_groups):
            cols = slice(s * MOE_SUB, (s + 1) * MOE_SUB)
            hg = _dot(x, wg_ref[0, :, cols]) + bg_ref[0][:, cols]
            hl = _dot(x, wl_ref[0, :, cols]) + bl_ref[0][:, cols]
            for r in range(s * rows_per_group, (s + 1) * rows_per_group):
                gather_row(nxt_base, nxt_slot, r)
            glu = jnp.minimum(hg, SWIGLU_LIMIT)
            lin = jnp.clip(hl, -SWIGLU_LIMIT, SWIGLU_LIMIT)
            act_s[:, cols] = (glu * _sigmoid(SWIGLU_ALPHA * glu) * (lin + 1.0)).astype(BF16)

        @pl.when(i > 0)
        def _():
            scatter_wait()

        act = act_s[...]
        tiles_per_sub = MOE_SUB // LANES
        for c in range(n_pairs):
            lo_cols = slice(c * MOE_SUB, (c + 1) * MOE_SUB)
            hi_cols = slice(HALF + c * MOE_SUB, HALF + (c + 1) * MOE_SUB)
            out_lo = _dot(act, wo_ref[0, :, lo_cols]) + bo_ref[0][:, lo_cols]
            out_hi = _dot(act, wo_ref[0, :, hi_cols]) + bo_ref[0][:, hi_cols]
            for r in range(c * rows_per_pair, (c + 1) * rows_per_pair):
                scatter_row(prev_base, prev_slot, r)
            packed = _pack_halves(jnp.concatenate([out_lo, out_hi], axis=1))
            for j in range(tiles_per_sub):
                stage[slot, pl.ds(c * tiles_per_sub + j, tm, stride=SUBLANES), :] = packed[:, j * LANES:(j + 1) * LANES]

        @pl.when(i == n_blocks - 1)
        def _():
            gather_wait(nxt_slot)
            scatter_wait()
            issue_loop(scatter_row, own_base, slot)
            scatter_wait()


def _moe(blk_exp, n_valid, idx_blocks, u2p, w_in, b_in, w_out, b_out, n_asg, n_rows_out):
    n_blocks = blk_exp.shape[0]
    d = D_MODEL
    f = D_EXPERT

    def half(g):
        return lambda i, be, nv: (be[i], 0, g)

    grid_spec = pltpu.PrefetchScalarGridSpec(
        num_scalar_prefetch=2,
        grid=(n_blocks,),
        in_specs=[
            pl.BlockSpec(memory_space=pl.ANY),
            pl.BlockSpec(memory_space=pl.ANY),
            pl.BlockSpec((1, d, f), half(0), pipeline_mode=pl.Buffered(1)),
            pl.BlockSpec((1, d, f), half(1), pipeline_mode=pl.Buffered(1)),
            pl.BlockSpec((1, 1, f), half(0)),
            pl.BlockSpec((1, 1, f), half(1)),
            pl.BlockSpec((1, f, d), lambda i, be, nv: (be[i], 0, 0)),
            pl.BlockSpec((1, 1, d), lambda i, be, nv: (be[i], 0, 0)),
        ],
        out_specs=pl.BlockSpec(memory_space=pl.ANY),
        scratch_shapes=[
            pltpu.SMEM((4 * 2 * MOE_TM,), I32),
            pltpu.VMEM((2, MOE_TM * SUBLANES, LANES), U32),
            pltpu.VMEM((MOE_TM, d), BF16),
            pltpu.VMEM((MOE_TM, f), BF16),
            pltpu.VMEM((2, MOE_TM * SUBLANES, LANES), U32),
            pltpu.SemaphoreType.DMA((4,)),
            pltpu.SemaphoreType.DMA((2,)),
            pltpu.SemaphoreType.DMA,
        ],
    )
    return pl.pallas_call(
        functools.partial(_moe_body, n_blocks=n_blocks, n_asg=n_asg),
        grid_spec=grid_spec,
        out_shape=jax.ShapeDtypeStruct((n_rows_out * SUBLANES, LANES), U32),
        compiler_params=pltpu.CompilerParams(
            dimension_semantics=("arbitrary",), vmem_limit_bytes=VMEM_LIMIT),
        name="moe",
    )(blk_exp, n_valid, idx_blocks, u2p, w_in, w_in, b_in, b_in, w_out, b_out)


def _routing_metadata(top_idx, n_blocks):
    tm = MOE_TM
    e = top_idx[:, :TOP_K].reshape(-1)
    n_asg = e.shape[0]
    order = jnp.argsort(e, stable=True).astype(I32)
    counts = jnp.sum((e[:, None] == jnp.arange(N_EXPERTS, dtype=I32)[None, :]).astype(I32), axis=0)
    cstart = jnp.cumsum(counts) - counts
    nblk = (counts + tm - 1) // tm
    bend = jnp.cumsum(nblk)
    bstart = bend - nblk
    blk = jnp.arange(n_blocks, dtype=I32)
    blk_exp = jnp.minimum(jnp.sum((bend[None, :] <= blk[:, None]).astype(I32), axis=1), N_EXPERTS - 1)
    first = (blk - bstart[blk_exp]) * tm
    n_valid = jnp.where(blk < bend[-1], jnp.clip(counts[blk_exp] - first, 0, tm), 0).astype(I32)
    r = jnp.arange(tm, dtype=I32)[None, :]
    valid = r < n_valid[:, None]
    asg = order[jnp.where(valid, (cstart[blk_exp] + first)[:, None] + r, 0)]
    src_row = (asg // TOP_K) * SUBLANES
    dst_row = jnp.where(valid, asg, n_asg + r) * SUBLANES
    src_row = jnp.concatenate([src_row, jnp.zeros((1, tm), I32)], axis=0)
    dst_row = jnp.concatenate([dst_row, (n_asg + r) * SUBLANES], axis=0)
    idx_blocks = jnp.concatenate([src_row, dst_row], axis=1).reshape(-1)
    return blk_exp, n_valid, idx_blocks


def _final_body(h_ref, y_ref, gate_ref, gn_ref, o_ref):
    h = h_ref[0]
    gates = gate_ref[0]
    y_ref = y_ref.at[0]
    lo_acc = jnp.zeros((h.shape[0], HALF), F32)
    hi_acc = jnp.zeros((h.shape[0], HALF), F32)
    for k in range(TOP_K):
        lo, hi = _unpack_halves(_load_token_tiles(y_ref, h.shape[0], k * SUBLANES, TOP_K * SUBLANES))
        g = gates[:, k:k + 1]
        lo_acc = lo_acc + g * lo
        hi_acc = hi_acc + g * hi
    y = h + jnp.concatenate([lo_acc, hi_acc], axis=1)
    o_ref[0] = y * lax.rsqrt(jnp.mean(y * y, axis=-1, keepdims=True) + NORM_EPS) * gn_ref[...]


def _final(h2, y4, gates, gain, n_out, tm):
    b, _, d = h2.shape
    return pl.pallas_call(
        _final_body,
        grid=(b, n_out // tm),
        in_specs=[
            pl.BlockSpec((1, tm, d), lambda i, j: (i, j, 0)),
            pl.BlockSpec((1, tm * TOP_K * SUBLANES, LANES), lambda i, j: (i, j, 0)),
            pl.BlockSpec((1, tm, LANES), lambda i, j: (i, j, 0)),
            pl.BlockSpec((1, d), lambda i, j: (0, 0)),
        ],
        out_specs=pl.BlockSpec((1, tm, d), lambda i, j: (i, j, 0)),
        out_shape=jax.ShapeDtypeStruct((b, n_out, d), F32),
        compiler_params=pltpu.CompilerParams(
            dimension_semantics=("arbitrary", "arbitrary"), vmem_limit_bytes=VMEM_LIMIT),
        name="final",
    )(h2, y4, gates, gain)


def _split_w_in(w_in):
    n_dn = 4 * DN_WIDTH
    n_na = n_dn + 4 * DN_HEADS
    w_main = jnp.concatenate(
        [w_in[:, :n_dn], w_in[:, n_na + 3 * NA_WIDTH:], w_in[:, n_na:n_na + 3 * NA_WIDTH]], axis=1).astype(BF16)
    wb = w_in[:, n_dn:n_dn + 2 * DN_HEADS]
    wa = w_in[:, n_dn + 2 * DN_HEADS:n_dn + 4 * DN_HEADS]
    return w_main, _group_lanes(wb, wa).astype(BF16)


def _group_lanes(b_part, a_part):
    nh, ng = DN_NH, DN_NG
    lead = b_part.shape[:-1]
    bp = b_part.reshape(lead + (2, ng, nh))
    ap = a_part.reshape(lead + (2, ng, nh))
    perm = tuple(range(len(lead))) + (len(lead) + 1, len(lead), len(lead) + 2)
    bp = bp.transpose(perm).reshape(lead + (ng, 2 * nh))
    ap = ap.transpose(perm).reshape(lead + (ng, 2 * nh))
    both = jnp.concatenate([bp, ap], axis=-1)
    pad = [(0, 0)] * (both.ndim - 1) + [(0, LANES - 4 * nh)]
    return jnp.pad(both, pad).reshape(lead + (ng * LANES,))


def kernel(x, meta_tokens, norm_mix, w_in, dn_conv, dn_a_log, dn_dt_bias, dn_out_norm, na_rpb,
           w_branch_dn, w_branch_na, w_out, norm_ffn, w_router, b_router, w_exp_in, b_exp_in,
           w_exp_out, b_exp_out, norm_final):
    bsz, seq, d = x.shape
    l = seq + N_META
    t = bsz * l
    meta = jnp.broadcast_to(meta_tokens[None].astype(x.dtype), (bsz, N_META, d))
    h = jnp.concatenate([x, meta], axis=1).reshape(t, d)
    depth = w_in.shape[0]
    assert depth == 1, "the final kernel fuses the last residual add with the final RMSNorm"
    tm_proj = 768 if t % 768 == 0 else 16 * (l // 16)
    tm_mix = 384 if t % 384 == 0 else 8 * (l // 8)
    tm_final = 512 if seq % 512 == 0 else seq
    n_blocks = t * TOP_K // MOE_TM + N_EXPERTS
    for layer in range(depth):
        w_main, w_ba = _split_w_in(w_in[layer])
        proj, ba = _inproj(h, norm_mix[layer][None, :], w_main, w_ba, tm_proj, 1024)
        proj3 = proj.reshape(bsz, l, N_MAIN_COLS)
        ba3 = ba.reshape(bsz, l, DN_NG * LANES)
        gpar = jnp.stack([_group_lanes(jnp.zeros_like(dn_a_log[layer]).reshape(-1), dn_a_log[layer].reshape(-1)),
                          _group_lanes(jnp.zeros_like(dn_dt_bias[layer]).reshape(-1),
                                       dn_dt_bias[layer].reshape(-1))], axis=0)
        gpar = gpar.reshape(2, DN_NG, LANES).transpose(1, 0, 2)
        y_dn = _deltanet(proj3, ba3, dn_conv[layer], gpar, dn_out_norm[layer][None, :])
        y_na = _natten(proj3, _na_bias_table(na_rpb[layer]))
        w_r = jnp.pad(w_router[layer], ((0, 0), (0, LANES - N_EXPERTS)))
        b_r = jnp.pad(b_router[layer], (0, LANES - N_EXPERTS))[None, :]
        h2, u2p, gates, top_idx = _mix(
            y_dn.reshape(t, DN_WIDTH), y_na.reshape(t, NA_WIDTH), proj, h,
            w_branch_dn[layer].astype(BF16), w_branch_na[layer].astype(BF16), w_out[layer].astype(BF16),
            norm_ffn[layer][None, :], w_r, b_r, tm_mix)
        blk_exp, n_valid, idx_blocks = _routing_metadata(top_idx, n_blocks)
        y4 = _moe(blk_exp, n_valid, idx_blocks, u2p,
                  w_exp_in[layer].astype(BF16), b_exp_in[layer][:, None, :],
                  w_exp_out[layer].astype(BF16), b_exp_out[layer][:, None, :], t * TOP_K, (bsz + 1) * l * TOP_K)
        out = _final(h2.reshape(bsz, l, d), y4.reshape(bsz + 1, l * TOP_K * SUBLANES, LANES),
                     gates.reshape(bsz, l, LANES), norm_final[None, :], seq, tm_final)
    return out
```
